```python
import jax, jax.numpy as jnp
from jax import lax
import numpy as np

D_MODEL = 1024
BATCH = 8
SEQ = 2048
DEPTH = 4

HEAD_DIM = 64
ROPE_THETA = 10000.0
GM_HEADS = D_MODEL // 2 // HEAD_DIM
GM_CHUNK = 128
NSA_HEADS = D_MODEL // 2 // HEAD_DIM
NSA_KV = NSA_HEADS // 4
NSA_HPG = NSA_HEADS // NSA_KV
CMP_LEN = 32
CMP_STRIDE = 16
SEL_BLOCK = 64
SEL_TOPN = 8
WINDOW = 512
Q_BLOCK = 128
POOL_WIDTH = D_MODEL // 2
POOL_WINDOWS = (2, 4, 8, 16)
POOL_GROUP = POOL_WIDTH // len(POOL_WINDOWS)
S5_WIDTH = D_MODEL // 2
S5_GROUP_CH = 16
S5_GROUPS = S5_WIDTH // S5_GROUP_CH
S5_STATE = 64
FFN_DIM = ((8 * D_MODEL // 3 + 255) // 256) * 256
N_EVEN = (DEPTH + 1) // 2
N_ODD = DEPTH // 2
GM_W = GM_HEADS * HEAD_DIM
NSA_W = NSA_HEADS * HEAD_DIM
KV_W = NSA_KV * HEAD_DIM
EVEN_SPLITS = (GM_W, GM_W, NSA_W, KV_W, KV_W, KV_W, KV_W, KV_W, KV_W, 3 * NSA_HEADS)
EVEN_IN = sum(EVEN_SPLITS)
EVEN_OUT = GM_W + NSA_W
ODD_IN = POOL_WIDTH + S5_WIDTH
ODD_OUT = POOL_WIDTH + S5_WIDTH
RMS_EPS = 1e-6
LN_EPS = 1e-5
NEG_INF = -1e30
SEL_FORCE = 1e4

kernel_name = 'hybrid_gmlp_nsa_pool_s5_macaron'


def rmsnorm(x, g):
    xf = x.astype(jnp.float32)
    y = xf * lax.rsqrt(jnp.mean(xf * xf, axis=-1, keepdims=True) + RMS_EPS)
    return (y * g.astype(jnp.float32)).astype(x.dtype)


def swiglu(h, w_gate, w_up, w_down):
    return (jax.nn.silu(h @ w_gate) * (h @ w_up)) @ w_down


def rope(x, pos):
    half = HEAD_DIM // 2
    inv = ROPE_THETA ** (-jnp.arange(half, dtype=jnp.float32) / half)
    ang = pos[:, None] * inv[None, :]
    cos, sin = jnp.cos(ang)[:, None, :], jnp.sin(ang)[:, None, :]
    xf = x.astype(jnp.float32)
    x1, x2 = xf[..., :half], xf[..., half:]
    return jnp.concatenate([x1 * cos - x2 * sin, x2 * cos + x1 * sin], axis=-1).astype(x.dtype)


def gmlp_spatial_gate(u, v, w_s, b):
    B, S = u.shape[0], u.shape[1]
    u = jax.nn.gelu(u)
    v = jax.nn.gelu(v).astype(jnp.float32)
    mu = jnp.mean(v, axis=-1, keepdims=True)
    var = jnp.mean(jnp.square(v - mu), axis=-1, keepdims=True)
    v = ((v - mu) * lax.rsqrt(var + LN_EPS)).astype(u.dtype)
    causal = jnp.tril(jnp.ones((GM_CHUNK, GM_CHUNK), dtype=w_s.dtype))
    vc = v.reshape(B, S // GM_CHUNK, GM_CHUNK, GM_HEADS, HEAD_DIM)
    s = jnp.einsum('hts,bcshd->bcthd', w_s * causal, vc) + b.T[None, None, :, :, None]
    return u * s.reshape(B, S, GM_HEADS, HEAD_DIM)


def nsa_compress(t, pe, w1, w2, idx):
    B = t.shape[0]
    nc = idx.shape[0]
    blocks = t[:, idx] + pe[:, None, :]
    flat = blocks.transpose(0, 1, 3, 2, 4).reshape(B, nc, NSA_KV, CMP_LEN * HEAD_DIM)
    return jax.nn.gelu(flat @ w1) @ w2


def nsa_attention(q, k_c, v_c, k_s, v_s, k_w, v_w, gate_logits, pe, w1, w2):
    B, S = q.shape[0], q.shape[1]
    G, HPG = NSA_KV, NSA_HPG
    scale = HEAD_DIM ** -0.5
    t = jnp.arange(S, dtype=jnp.int32)
    qg = q.reshape(B, S, G, HPG, HEAD_DIM)

    nc = (S - CMP_LEN) // CMP_STRIDE + 1
    cmp_start = jnp.arange(nc, dtype=jnp.int32) * CMP_STRIDE
    cidx = cmp_start[:, None] + jnp.arange(CMP_LEN, dtype=jnp.int32)[None, :]
    kc = nsa_compress(k_c, pe[0], w1[0], w2[0], cidx)
    vc = nsa_compress(v_c, pe[1], w1[1], w2[1], cidx)
    end_pos = cmp_start + (CMP_LEN - 1)
    kc = rope(kc, end_pos.astype(jnp.float32))
    s_c = jnp.einsum('btgqd,bngd->bgqtn', qg, kc).astype(jnp.float32) * scale
    valid_c = end_pos[None, :] <= t[:, None]
    s_c = jnp.where(valid_c, s_c, NEG_INF)
    p_c = jax.nn.softmax(s_c, axis=-1) * jnp.any(valid_c, axis=-1)[:, None].astype(jnp.float32)
    o_c = jnp.einsum('bgqtn,bngd->btgqd', p_c.astype(vc.dtype), vc)

    nsel = S // SEL_BLOCK
    blk_start = jnp.arange(nsel, dtype=jnp.int32) * SEL_BLOCK
    overlap = ((cmp_start[:, None] < blk_start[None, :] + SEL_BLOCK)
               & (cmp_start[:, None] + CMP_LEN > blk_start[None, :])).astype(jnp.float32)
    imp = jnp.einsum('bgqtn,nj->btgj', p_c, overlap)
    cur = t // SEL_BLOCK
    j = jnp.arange(nsel, dtype=jnp.int32)
    forced = (j[None, :] == 0) | (j[None, :] == cur[:, None]) | (j[None, :] == cur[:, None] - 1)
    causal_blk = blk_start[None, :] <= t[:, None]
    imp = jnp.where(forced[:, None, :], SEL_FORCE,
                    jnp.where(causal_blk[:, None, :], imp, -SEL_FORCE))
    n_top = min(SEL_TOPN, nsel)
    _, sel_idx = lax.top_k(imp, n_top)

    kb = k_s.reshape(B, nsel, SEL_BLOCK, G, HEAD_DIM).transpose(0, 3, 1, 2, 4)
    vb = v_s.reshape(B, nsel, SEL_BLOCK, G, HEAD_DIM).transpose(0, 3, 1, 2, 4)
    nq = S // Q_BLOCK
    q_chunks = qg.reshape(B, nq, Q_BLOCK, G, HPG, HEAD_DIM).transpose(1, 0, 2, 3, 4, 5)
    i_chunks = sel_idx.reshape(B, nq, Q_BLOCK, G, n_top).transpose(1, 0, 2, 3, 4)
    t_chunks = t.reshape(nq, Q_BLOCK)
    bi = jnp.arange(B)[:, None, None, None]
    gi = jnp.arange(G)[None, None, :, None]
    in_blk = jnp.arange(SEL_BLOCK, dtype=jnp.int32)

    def selected_block(args):
        q_i, idx_i, t_i = args
        ks = kb[bi, gi, idx_i]
        vs = vb[bi, gi, idx_i]
        kpos = idx_i[..., None] * SEL_BLOCK + in_blk
        s = jnp.einsum('btgqd,btgnld->btgqnl', q_i, ks).astype(jnp.float32) * scale
        mask = kpos <= t_i[None, :, None, None, None]
        s = jnp.where(mask[:, :, :, None], s, NEG_INF)
        s = s.reshape(B, Q_BLOCK, G, HPG, n_top * SEL_BLOCK)
        p = jax.nn.softmax(s, axis=-1).astype(vs.dtype)
        return jnp.einsum('btgqk,btgkd->btgqd', p, vs.reshape(B, Q_BLOCK, G, n_top * SEL_BLOCK, HEAD_DIM))

    o_s = lax.map(selected_block, (q_chunks, i_chunks, t_chunks))
    o_s = o_s.transpose(1, 0, 2, 3, 4, 5).reshape(B, S, G, HPG, HEAD_DIM)

    span = WINDOW + Q_BLOCK
    kp = jnp.pad(k_w, ((0, 0), (WINDOW, 0), (0, 0), (0, 0)))
    vp = jnp.pad(v_w, ((0, 0), (WINDOW, 0), (0, 0), (0, 0)))
    widx = jnp.arange(nq, dtype=jnp.int32)[:, None] * Q_BLOCK + jnp.arange(span, dtype=jnp.int32)[None, :]
    kw = kp[:, widx]
    vw = vp[:, widx]
    kpos = widx - WINDOW
    qw = qg.reshape(B, nq, Q_BLOCK, G, HPG, HEAD_DIM)
    s_w = jnp.einsum('bitgqd,bikgd->bigqtk', qw, kw).astype(jnp.float32) * scale
    diff = t_chunks[:, :, None] - kpos[:, None, :]
    wmask = (kpos[:, None, :] >= 0) & (diff >= 0) & (diff < WINDOW)
    s_w = jnp.where(wmask[None, :, None, None], s_w, NEG_INF)
    p_w = jax.nn.softmax(s_w, axis=-1).astype(vw.dtype)
    o_w = jnp.einsum('bigqtk,bikgd->bitgqd', p_w, vw).reshape(B, S, G, HPG, HEAD_DIM)

    g = jax.nn.sigmoid(gate_logits.astype(jnp.float32)).reshape(B, S, G, HPG, 3)
    o = g[..., 0:1] * o_c + g[..., 1:2] * o_s + g[..., 2:3] * o_w
    return o.reshape(B, S, NSA_W).astype(q.dtype)


def even_mixer(h, w_in, w_out, gm_w_s, gm_b, cmp_pe, cmp_w1, cmp_w2):
    B, S = h.shape[0], h.shape[1]
    z = h @ w_in
    cuts = list(np.cumsum(EVEN_SPLITS)[:-1])
    u, v, q, kc, vc, ks, vs, kw, vw, gl = jnp.split(z, cuts, axis=-1)
    heads = lambda a, n: a.reshape(B, S, n, HEAD_DIM)
    pos = jnp.arange(S, dtype=jnp.float32)
    out_a = gmlp_spatial_gate(heads(u, GM_HEADS), heads(v, GM_HEADS), gm_w_s, gm_b).reshape(B, S, GM_W)
    out_b = nsa_attention(rope(heads(q, NSA_HEADS), pos),
                          heads(kc, NSA_KV), heads(vc, NSA_KV),
                          rope(heads(ks, NSA_KV), pos), heads(vs, NSA_KV),
                          rope(heads(kw, NSA_KV), pos), heads(vw, NSA_KV),
                          gl, cmp_pe, cmp_w1, cmp_w2)
    return jnp.concatenate([out_a, out_b.astype(out_a.dtype)], axis=-1) @ w_out


def odd_mixer(h, w_in, w_out, pool_w, pool_scale, lam_re, lam_im, b_re, b_im, c_re, c_im, d_skip, log_dt, w_glu):
    B, S = h.shape[0], h.shape[1]
    z = h @ w_in
    zc, zd = z[..., :POOL_WIDTH], z[..., POOL_WIDTH:]
    t = jnp.arange(S, dtype=jnp.int32)

    n_grp = len(POOL_WINDOWS)
    zg = zc.reshape(B, S, n_grp, POOL_GROUP).astype(jnp.float32)
    cs = jnp.concatenate([jnp.zeros((B, 1, n_grp, POOL_GROUP), jnp.float32), jnp.cumsum(zg, axis=1)], axis=1)
    win = jnp.array(POOL_WINDOWS, dtype=jnp.int32)
    lo = jnp.maximum(t[:, None] + 1 - win[None, :], 0)
    lower = cs[:, lo, jnp.arange(n_grp)[None, :]]
    cnt = (t[:, None] + 1 - lo).astype(jnp.float32)
    pooled = (cs[:, 1:] - lower) / cnt[..., None] - zg
    y_c = jnp.einsum('bsgc,gcd->bsgd', pooled, pool_w.astype(jnp.float32))
    y_c = (y_c.reshape(B, S, POOL_WIDTH) * pool_scale.astype(jnp.float32)).astype(h.dtype)

    u = zd.reshape(B, S, S5_GROUPS, S5_GROUP_CH).astype(jnp.float32)
    lam = lax.complex(lam_re.astype(jnp.float32), lam_im.astype(jnp.float32))
    dt = jnp.exp(log_dt.astype(jnp.float32))[:, None]
    lam_bar = jnp.exp(lam * dt)
    b_c = lax.complex(b_re.astype(jnp.float32), b_im.astype(jnp.float32))
    c_c = lax.complex(c_re.astype(jnp.float32), c_im.astype(jnp.float32))
    b_bar = ((lam_bar - 1.0) / lam)[..., None] * b_c
    bu = jnp.einsum('gpc,bsgc->sbgp', b_bar, u.astype(jnp.complex64))
    a = jnp.broadcast_to(lam_bar, (S, 1, S5_GROUPS, S5_STATE))

    def combine(e1, e2):
        a1, b1 = e1
        a2, b2 = e2
        return a2 * a1, a2 * b1 + b2

    _, states = lax.associative_scan(combine, (a, bu), axis=0)
    y = jnp.einsum('gcp,sbgp->bsgc', c_c, states).real + d_skip.astype(jnp.float32) * u
    y = jax.nn.gelu(y.reshape(B, S, S5_WIDTH))
    ab = y @ w_glu.astype(jnp.float32)
    y_d = (ab[..., :S5_WIDTH] * jax.nn.sigmoid(ab[..., S5_WIDTH:])).astype(h.dtype)

    return jnp.concatenate([y_c, y_d], axis=-1) @ w_out


def setup_inputs(seed: int = 0) -> dict:
    key = jax.random.key(seed)
    k = jax.random.split(key, 26)
    f32 = jnp.float32
    nrm = lambda kk, shape, sc: jax.random.normal(kk, shape, f32) * sc
    lam_im = jnp.pi * jnp.broadcast_to(jnp.arange(S5_STATE, dtype=f32), (N_ODD, S5_GROUPS, S5_STATE))
    return {
        'x': nrm(k[0], (BATCH, SEQ, D_MODEL), 1.0),
        'norm_w': 1.0 + nrm(k[1], (DEPTH, 3, D_MODEL), 0.01),
        'ffn_w_gate': nrm(k[2], (DEPTH, 2, D_MODEL, FFN_DIM), D_MODEL ** -0.5),
        'ffn_w_up': nrm(k[3], (DEPTH, 2, D_MODEL, FFN_DIM), D_MODEL ** -0.5),
        'ffn_w_down': nrm(k[4], (DEPTH, 2, FFN_DIM, D_MODEL), FFN_DIM ** -0.5),
        'final_norm_w': 1.0 + nrm(k[5], (D_MODEL,), 0.01),
        'ev_w_in': nrm(k[6], (N_EVEN, D_MODEL, EVEN_IN), D_MODEL ** -0.5),
        'ev_w_out': nrm(k[7], (N_EVEN, EVEN_OUT, D_MODEL), EVEN_OUT ** -0.5),
        'gm_w_s': nrm(k[8], (N_EVEN, GM_HEADS, GM_CHUNK, GM_CHUNK), GM_CHUNK ** -0.5),
        'gm_b': 1.0 + nrm(k[9], (N_EVEN, GM_HEADS, GM_CHUNK), 0.01),
        'nsa_cmp_pe': nrm(k[10], (N_EVEN, 2, CMP_LEN, HEAD_DIM), 0.02),
        'nsa_cmp_w1': nrm(k[11], (N_EVEN, 2, CMP_LEN * HEAD_DIM, HEAD_DIM), (CMP_LEN * HEAD_DIM) ** -0.5),
        'nsa_cmp_w2': nrm(k[12], (N_EVEN, 2, HEAD_DIM, HEAD_DIM), HEAD_DIM ** -0.5),
        'od_w_in': nrm(k[13], (N_ODD, D_MODEL, ODD_IN), D_MODEL ** -0.5),
        'od_w_out': nrm(k[14], (N_ODD, ODD_OUT, D_MODEL), ODD_OUT ** -0.5),
        'pool_w': nrm(k[15], (N_ODD, len(POOL_WINDOWS), POOL_GROUP, POOL_GROUP), POOL_GROUP ** -0.5),
        'pool_scale': 1.0 + nrm(k[16], (N_ODD, POOL_WIDTH), 0.02),
        's5_lam_re': -0.5 + nrm(k[17], (N_ODD, S5_GROUPS, S5_STATE), 0.01),
        's5_lam_im': lam_im + nrm(k[18], (N_ODD, S5_GROUPS, S5_STATE), 0.01),
        's5_b_re': nrm(k[19], (N_ODD, S5_GROUPS, S5_STATE, S5_GROUP_CH), (2 * S5_GROUP_CH) ** -0.5),
        's5_b_im': nrm(k[20], (N_ODD, S5_GROUPS, S5_STATE, S5_GROUP_CH), (2 * S5_GROUP_CH) ** -0.5),
        's5_c_re': nrm(k[21], (N_ODD, S5_GROUPS, S5_GROUP_CH, S5_STATE), (2 * S5_STATE) ** -0.5),
        's5_c_im': nrm(k[22], (N_ODD, S5_GROUPS, S5_GROUP_CH, S5_STATE), (2 * S5_STATE) ** -0.5),
        's5_d': nrm(k[23], (N_ODD, S5_GROUPS, S5_GROUP_CH), 1.0),
        's5_log_dt': jax.random.uniform(k[24], (N_ODD, S5_GROUPS), f32, float(np.log(1e-3)), float(np.log(1e-1))),
        's5_w_glu': nrm(k[25], (N_ODD, S5_WIDTH, 2 * S5_WIDTH), S5_WIDTH ** -0.5),
    }


def reference(x, norm_w, ffn_w_gate, ffn_w_up, ffn_w_down, final_norm_w, ev_w_in, ev_w_out,
              gm_w_s, gm_b, nsa_cmp_pe, nsa_cmp_w1, nsa_cmp_w2, od_w_in, od_w_out, pool_w, pool_scale,
              s5_lam_re, s5_lam_im, s5_b_re, s5_b_im, s5_c_re, s5_c_im, s5_d, s5_log_dt, s5_w_glu):
    for l in range(DEPTH):
        x = x + 0.5 * swiglu(rmsnorm(x, norm_w[l, 0]), ffn_w_gate[l, 0], ffn_w_up[l, 0], ffn_w_down[l, 0])
        h = rmsnorm(x, norm_w[l, 1])
        i = l // 2
        if l % 2 == 0:
            m = even_mixer(h, ev_w_in[i], ev_w_out[i], gm_w_s[i], gm_b[i],
                           nsa_cmp_pe[i], nsa_cmp_w1[i], nsa_cmp_w2[i])
        else:
            m = odd_mixer(h, od_w_in[i], od_w_out[i], pool_w[i], pool_scale[i],
                          s5_lam_re[i], s5_lam_im[i], s5_b_re[i], s5_b_im[i], s5_c_re[i], s5_c_im[i],
                          s5_d[i], s5_log_dt[i], s5_w_glu[i])
        x = x + m.astype(x.dtype)
        x = x + 0.5 * swiglu(rmsnorm(x, norm_w[l, 2]), ffn_w_gate[l, 1], ffn_w_up[l, 1], ffn_w_down[l, 1])
    return rmsnorm(x, final_norm_w)
```

```python
import functools

import jax
import jax.numpy as jnp
from jax import lax
from jax.experimental import pallas as pl
from jax.experimental.pallas import tpu as pltpu

F32 = jnp.float32
BF16 = jnp.bfloat16

D_MODEL = 1024
DEPTH = 4
HEAD_DIM = 64
HALF = HEAD_DIM // 2
ROPE_THETA = 10000.0
N_HEADS = 8
GM_CHUNK = 128
NSA_KV = 2
NSA_HPG = 4
CMP_LEN = 32
CMP_STRIDE = 16
SEL_BLOCK = 64
SEL_TOPN = 8
WINDOW = 512
Q_BLOCK = 128
POOL_WINDOWS = (2, 4, 8, 16)
POOL_GROUP = 128
S5_GROUPS = 32
S5_GROUP_CH = 16
S5_STATE = 64
S5_WIDTH = 512
S5_MODES = S5_GROUPS * S5_STATE
FFN_DIM = 2816
RMS_EPS = 1e-6
LN_EPS = 1e-5
NEG_INF = -1e30
SEL_FORCE = 1e4

LANES = 128
HALF_W = 512
EVEN_IN = 2328
EVEN_IN_PAD = 2432
COL_U, COL_V, COL_Q = 0, 512, 1024
COL_KC, COL_VC, COL_KS, COL_VS, COL_KW, COL_VW, COL_GL = 1536, 1664, 1792, 1920, 2048, 2176, 2304
ROPE_SLABS = (1024, 1152, 1280, 1408, COL_KS, COL_KW)

VMEM_LIMIT = 56 * 1024 * 1024


def _cparams(n_axes):
    return pltpu.CompilerParams(dimension_semantics=("arbitrary",) * n_axes,
                                vmem_limit_bytes=VMEM_LIMIT)


def _resident(shape):
    nd = len(shape)
    return pl.BlockSpec(shape, lambda *_: (0,) * nd, pipeline_mode=pl.Buffered(1))


def _rms(x, g):
    ms = jnp.mean(x * x, axis=-1, keepdims=True)
    return x * lax.rsqrt(ms + RMS_EPS) * g


def _dot(a, b):
    return jnp.dot(a, b, preferred_element_type=F32)


def _dot_nt(a, b):
    return lax.dot_general(a, b, (((1,), (1,)), ((), ())), preferred_element_type=F32)


def _ffn_kernel(x_ref, g_ref, wg_ref, wu_ref, wd_ref, o_ref, *, fc):
    x = x_ref[...]
    h = _rms(x, g_ref[...]).astype(BF16)
    acc = jnp.zeros_like(x)
    for c in range(FFN_DIM // fc):
        sl = slice(c * fc, (c + 1) * fc)
        a = _dot(h, wg_ref[:, sl])
        b = _dot(h, wu_ref[:, sl])
        t = (a * jax.nn.sigmoid(a) * b).astype(BF16)
        acc = acc + _dot(t, wd_ref[sl, :])
    o_ref[...] = x + 0.5 * acc


def _ffn(x, g, wg, wu, wd, *, tm=512, fc=256):
    T = x.shape[0]
    return pl.pallas_call(
        functools.partial(_ffn_kernel, fc=fc),
        out_shape=jax.ShapeDtypeStruct((T, D_MODEL), F32),
        grid=(T // tm,),
        in_specs=[pl.BlockSpec((tm, D_MODEL), lambda i: (i, 0)),
                  _resident((1, D_MODEL)),
                  _resident((D_MODEL, FFN_DIM)),
                  _resident((D_MODEL, FFN_DIM)),
                  _resident((FFN_DIM, D_MODEL))],
        out_specs=pl.BlockSpec((tm, D_MODEL), lambda i: (i, 0)),
        compiler_params=_cparams(1),
        name="ffn",
    )(x, g, wg, wu, wd)


def _proj_kernel(x_ref, g_ref, w_ref, o_ref):
    h = _rms(x_ref[...], g_ref[...]).astype(BF16)
    o_ref[...] = _dot(h, w_ref[...])


def _odd_proj(x, g, w, *, tm=512):
    T = x.shape[0]
    N = w.shape[1]
    return pl.pallas_call(
        _proj_kernel,
        out_shape=jax.ShapeDtypeStruct((T, N), F32),
        grid=(T // tm,),
        in_specs=[pl.BlockSpec((tm, D_MODEL), lambda i: (i, 0)),
                  _resident((1, D_MODEL)),
                  _resident((D_MODEL, N))],
        out_specs=pl.BlockSpec((tm, N), lambda i: (i, 0)),
        compiler_params=_cparams(1),
        name="odd_proj",
    )(x, g, w)


def _even_proj_kernel(x_ref, g_ref, w_ref, cos_ref, sin_ref, o_ref):
    h = _rms(x_ref[...], g_ref[...]).astype(BF16)
    o_ref[...] = _dot(h, w_ref[...])
    cos = cos_ref[...]
    sin = sin_ref[...]
    lane = lax.broadcasted_iota(jnp.int32, cos.shape, 1)
    first_half = (lane % HEAD_DIM) < HALF
    for c0 in ROPE_SLABS:
        xs = o_ref[:, c0:c0 + LANES]
        rot = jnp.where(first_half, pltpu.roll(xs, LANES - HALF, 1), pltpu.roll(xs, HALF, 1))
        o_ref[:, c0:c0 + LANES] = xs * cos + rot * sin


def _even_proj(x, g, w, cos, sin, *, S, tm=512):
    T = x.shape[0]
    spt = S // tm
    return pl.pallas_call(
        _even_proj_kernel,
        out_shape=jax.ShapeDtypeStruct((T, EVEN_IN_PAD), F32),
        grid=(T // tm,),
        in_specs=[pl.BlockSpec((tm, D_MODEL), lambda i: (i, 0)),
                  _resident((1, D_MODEL)),
                  _resident((D_MODEL, EVEN_IN_PAD)),
                  pl.BlockSpec((tm, LANES), lambda i: (i % spt, 0)),
                  pl.BlockSpec((tm, LANES), lambda i: (i % spt, 0))],
        out_specs=pl.BlockSpec((tm, EVEN_IN_PAD), lambda i: (i, 0)),
        compiler_params=_cparams(1),
        name="even_proj",
    )(x, g, w, cos, sin)


def _outproj_kernel(a_ref, b_ref, w_ref, x_ref, o_ref):
    acc = _dot(a_ref[...].astype(BF16), w_ref[0:HALF_W, :])
    acc = acc + _dot(b_ref[...].astype(BF16), w_ref[HALF_W:2 * HALF_W, :])
    o_ref[...] = x_ref[...] + acc


def _outproj(a, b, w, x, *, tm=512):
    T = x.shape[0]
    return pl.pallas_call(
        _outproj_kernel,
        out_shape=jax.ShapeDtypeStruct((T, D_MODEL), F32),
        grid=(T // tm,),
        in_specs=[pl.BlockSpec((tm, HALF_W), lambda i: (i, 0)),
                  pl.BlockSpec((tm, HALF_W), lambda i: (i, 0)),
                  _resident((D_MODEL, D_MODEL)),
                  pl.BlockSpec((tm, D_MODEL), lambda i: (i, 0))],
        out_specs=pl.BlockSpec((tm, D_MODEL), lambda i: (i, 0)),
        compiler_params=_cparams(1),
        name="outproj",
    )(a, b, w, x)


def _final_norm_kernel(x_ref, g_ref, o_ref):
    o_ref[...] = _rms(x_ref[...], g_ref[...])


def _final_norm(x, g, *, tm=512):
    T = x.shape[0]
    return pl.pallas_call(
        _final_norm_kernel,
        out_shape=jax.ShapeDtypeStruct((T, D_MODEL), F32),
        grid=(T // tm,),
        in_specs=[pl.BlockSpec((tm, D_MODEL), lambda i: (i, 0)), _resident((1, D_MODEL))],
        out_specs=pl.BlockSpec((tm, D_MODEL), lambda i: (i, 0)),
        compiler_params=_cparams(1),
        name="final_norm",
    )(x, g)


def _gmlp_kernel(u_ref, v_ref, ws_ref, bt_ref, o_ref, *, chunks):
    r = lax.broadcasted_iota(jnp.int32, (GM_CHUNK, GM_CHUNK), 0)
    c = lax.broadcasted_iota(jnp.int32, (GM_CHUNK, GM_CHUNK), 1)
    causal = c <= r
    bt = bt_ref[...]
    for ci in range(chunks):
        rows = slice(ci * GM_CHUNK, (ci + 1) * GM_CHUNK)
        u = jax.nn.gelu(u_ref[rows, :])
        v = jax.nn.gelu(v_ref[rows, :])
        for h in range(N_HEADS):
            cols = slice(h * HEAD_DIM, (h + 1) * HEAD_DIM)
            vh = v[:, cols]
            mu = jnp.mean(vh, axis=-1, keepdims=True)
            d = vh - mu
            var = jnp.mean(d * d, axis=-1, keepdims=True)
            vn = d * lax.rsqrt(var + LN_EPS)
            w = jnp.where(causal, ws_ref[h], 0.0)
            s = _dot(w.astype(BF16), vn.astype(BF16)) + bt[:, h:h + 1]
            o_ref[rows, cols] = u[:, cols] * s


def _gmlp(z, ws, bt, *, chunks=4):
    T = z.shape[0]
    tm = chunks * GM_CHUNK
    return pl.pallas_call(
        functools.partial(_gmlp_kernel, chunks=chunks),
        out_shape=jax.ShapeDtypeStruct((T, HALF_W), F32),
        grid=(T // tm,),
        in_specs=[pl.BlockSpec((tm, HALF_W), lambda i: (i, COL_U // HALF_W)),
                  pl.BlockSpec((tm, HALF_W), lambda i: (i, COL_V // HALF_W)),
                  _resident((N_HEADS, GM_CHUNK, GM_CHUNK)),
                  _resident((GM_CHUNK, N_HEADS))],
        out_specs=pl.BlockSpec((tm, HALF_W), lambda i: (i, 0)),
        compiler_params=_cparams(1),
        name="gmlp",
    )(z, z, ws, bt)


def _cmp_kernel(k2_ref, v2_ref, pe_ref, w1_ref, w2_ref, cos_ref, sin_ref, ko_ref, vo_ref):
    half_flat = CMP_STRIDE * HEAD_DIM
    for which, (src, dst) in enumerate(((k2_ref, ko_ref), (v2_ref, vo_ref))):
        pe = pe_ref[which]
        w1 = w1_ref[which]
        w2 = w2_ref[which].astype(BF16)
        for g in range(NSA_KV):
            a = src[0, g]
            a_next = pltpu.roll(a, a.shape[0] - 1, 0)
            pre = _dot((a + pe[:, :half_flat]).astype(BF16), w1[:half_flat, :].astype(BF16))
            pre = pre + _dot((a_next + pe[:, half_flat:]).astype(BF16), w1[half_flat:, :].astype(BF16))
            cmp = _dot(jax.nn.gelu(pre).astype(BF16), w2)
            if which == 0:
                rot = jnp.concatenate([-cmp[:, HALF:], cmp[:, :HALF]], axis=1)
                cmp = cmp * cos_ref[...] + rot * sin_ref[...]
            dst[0, :, g * HEAD_DIM:(g + 1) * HEAD_DIM] = cmp


def _compress(k2, v2, pe, w1, w2, cosc, sinc):
    B, _, nblk, flat = k2.shape
    return pl.pallas_call(
        _cmp_kernel,
        out_shape=(jax.ShapeDtypeStruct((B, nblk, LANES), F32),
                   jax.ShapeDtypeStruct((B, nblk, LANES), F32)),
        grid=(B,),
        in_specs=[pl.BlockSpec((1, NSA_KV, nblk, flat), lambda b: (b, 0, 0, 0)),
                  pl.BlockSpec((1, NSA_KV, nblk, flat), lambda b: (b, 0, 0, 0)),
                  _resident((2, 1, 2 * flat)),
                  _resident((2, 2 * flat, HEAD_DIM)),
                  _resident((2, HEAD_DIM, HEAD_DIM)),
                  _resident((nblk, HEAD_DIM)),
                  _resident((nblk, HEAD_DIM))],
        out_specs=(pl.BlockSpec((1, nblk, LANES), lambda b: (b, 0, 0)),
                   pl.BlockSpec((1, nblk, LANES), lambda b: (b, 0, 0))),
        compiler_params=_cparams(1),
        name="nsa_compress",
    )(k2, v2, pe, w1, w2, cosc, sinc)


def _softmax_parts(s):
    m = jnp.max(s, axis=-1, keepdims=True)
    e = jnp.exp(s - m)
    return e, jnp.sum(e, axis=-1, keepdims=True)


def _nsa_kernel(q_ref, gl_ref, ks_ref, vs_ref, kw_ref, vw_ref, kc_ref, vc_ref, ovt_ref, ex_ref,
                o_ref, *, S):
    TQ = Q_BLOCK
    ncmp = S // CMP_STRIDE
    nsel = S // SEL_BLOCK
    span = WINDOW + TQ
    t0 = pl.program_id(1) * TQ
    scale = HEAD_DIM ** -0.5

    q = q_ref[...] * scale
    gate = jax.nn.sigmoid(gl_ref[...])
    lane = lax.broadcasted_iota(jnp.int32, (TQ, LANES), 1)
    trow = t0 + lax.broadcasted_iota(jnp.int32, (TQ, 1), 0)

    ncol = lax.broadcasted_iota(jnp.int32, (TQ, ncmp), 1)
    valid_c = (ncol * CMP_STRIDE + (CMP_LEN - 1) <= trow) & (ncol < ncmp - 1)
    any_c = (trow >= CMP_LEN - 1).astype(F32)

    jrow = lax.broadcasted_iota(jnp.int32, (nsel, TQ), 0)
    tl = t0 + lax.broadcasted_iota(jnp.int32, (nsel, TQ), 1)
    cur = jnp.right_shift(tl, 6)
    forced = (jrow == 0) | (jrow == cur) | (jrow == cur - 1)
    causal_blk = jrow * SEL_BLOCK <= tl

    kcol = lax.broadcasted_iota(jnp.int32, (TQ, S), 1)
    causal_key = kcol <= trow

    wstart = pl.multiple_of(jnp.maximum(t0 - WINDOW, 0), TQ)
    wpos = wstart + lax.broadcasted_iota(jnp.int32, (TQ, span), 1)
    allowed_w = (wpos <= trow) & (trow - wpos < WINDOW)

    kc = kc_ref[0].astype(BF16)
    vc = vc_ref[0].astype(BF16)
    ks = ks_ref[...].astype(BF16)
    vs = vs_ref[...].astype(BF16)
    kw = kw_ref[pl.ds(wstart, span), :].astype(BF16)
    vw = vw_ref[pl.ds(wstart, span), :].astype(BF16)

    for g in range(NSA_KV):
        in_g = (lane >= g * HEAD_DIM) & (lane < (g + 1) * HEAD_DIM)
        qh, e_c, l_c = [], [], []
        for hq in range(NSA_HPG):
            h = g * NSA_HPG + hq
            slab = q[:, (h // 2) * LANES:(h // 2 + 1) * LANES]
            if h % 2 != g:
                slab = pltpu.roll(slab, HEAD_DIM, 1)
            qp = jnp.where(in_g, slab, 0.0).astype(BF16)
            qh.append(qp)
            s_c = jnp.where(valid_c, _dot_nt(qp, kc), NEG_INF)
            e, l = _softmax_parts(s_c)
            e_c.append(e)
            l_c.append(l)
        p_c = [e_c[i] / l_c[i] * any_c for i in range(NSA_HPG)]
        p_sum = (p_c[0] + p_c[1]) + (p_c[2] + p_c[3])

        p_hi = p_sum.astype(BF16)
        p_lo = (p_sum - p_hi.astype(F32)).astype(BF16)
        ovt = ovt_ref[...]
        imp_t = (_dot_nt(ovt, p_hi) + _dot_nt(ovt, p_lo))[0:nsel, :]
        imp_t = jnp.where(forced, SEL_FORCE, jnp.where(causal_blk, imp_t, -SEL_FORCE))
        rank = jnp.zeros((nsel, TQ), jnp.int32)
        for jp in range(nsel):
            row = imp_t[jp:jp + 1, :]
            beats = (row > imp_t) | ((row == imp_t) & (jrow > jp))
            rank = rank + beats.astype(jnp.int32)
        sel_t = (rank < SEL_TOPN).astype(F32)
        if nsel < LANES:
            sel_t = jnp.concatenate([sel_t, jnp.zeros((LANES - nsel, TQ), F32)], axis=0)
        sel_q = sel_t.T.astype(BF16)
        allowed_s = (_dot(sel_q, ex_ref[...]) > 0.5) & causal_key

        outs = []
        for hq in range(NSA_HPG):
            h = g * NSA_HPG + hq
            qp = qh[hq]
            o_c = _dot(p_c[hq].astype(BF16), vc)
            e_s, l_s = _softmax_parts(jnp.where(allowed_s, _dot_nt(qp, ks), NEG_INF))
            o_s = _dot(e_s.astype(BF16), vs) / l_s
            e_w, l_w = _softmax_parts(jnp.where(allowed_w, _dot_nt(qp, kw), NEG_INF))
            o_w = _dot(e_w.astype(BF16), vw) / l_w
            outs.append(gate[:, 3 * h:3 * h + 1] * o_c + gate[:, 3 * h + 1:3 * h + 2] * o_s
                        + gate[:, 3 * h + 2:3 * h + 3] * o_w)
        for pair in range(NSA_HPG // 2):
            a, b = outs[2 * pair], outs[2 * pair + 1]
            if g == 0:
                b = pltpu.roll(b, HEAD_DIM, 1)
            else:
                a = pltpu.roll(a, HEAD_DIM, 1)
            c0 = (g * NSA_HPG + 2 * pair) * HEAD_DIM
            o_ref[:, c0:c0 + LANES] = jnp.where(lane < HEAD_DIM, a, b)


def _nsa(z, kcmp, vcmp, ovt, ex, *, B, S):
    T = z.shape[0]
    nq = S // Q_BLOCK
    ncmp = S // CMP_STRIDE
    seq_block = lambda col: pl.BlockSpec((S, LANES), lambda b, i: (b, col // LANES))
    return pl.pallas_call(
        functools.partial(_nsa_kernel, S=S),
        out_shape=jax.ShapeDtypeStruct((T, HALF_W), F32),
        grid=(B, nq),
        in_specs=[pl.BlockSpec((Q_BLOCK, HALF_W), lambda b, i: (b * nq + i, COL_Q // HALF_W)),
                  pl.BlockSpec((Q_BLOCK, LANES), lambda b, i: (b * nq + i, COL_GL // LANES)),
                  seq_block(COL_KS), seq_block(COL_VS), seq_block(COL_KW), seq_block(COL_VW),
                  pl.BlockSpec((1, ncmp, LANES), lambda b, i: (b, 0, 0)),
                  pl.BlockSpec((1, ncmp, LANES), lambda b, i: (b, 0, 0)),
                  _resident((LANES, ncmp)),
                  _resident((LANES, S))],
        out_specs=pl.BlockSpec((Q_BLOCK, HALF_W), lambda b, i: (b * nq + i, 0)),
        compiler_params=_cparams(2),
        name="nsa_attn",
    )(z, z, z, z, z, z, kcmp, vcmp, ovt, ex)


def _pool_kernel(z_ref, pw_ref, ps_ref, o_ref):
    S = z_ref.shape[0]
    row = lax.broadcasted_iota(jnp.int32, (S, POOL_GROUP), 0)
    for gi, w in enumerate(POOL_WINDOWS):
        cols = slice(gi * POOL_GROUP, (gi + 1) * POOL_GROUP)
        zg = z_ref[:, cols]
        acc = zg
        k = 1
        while k < w:
            acc = acc + jnp.where(row >= k, pltpu.roll(acc, k, 0), 0.0)
            k *= 2
        cnt = jnp.minimum(row + 1, w).astype(F32)
        pooled = acc / cnt - zg
        y = _dot(pooled.astype(BF16), pw_ref[gi].astype(BF16))
        o_ref[:, cols] = y * ps_ref[:, cols]


def _pool(z, pw, ps, *, B, S):
    T = z.shape[0]
    return pl.pallas_call(
        _pool_kernel,
        out_shape=jax.ShapeDtypeStruct((T, HALF_W), F32),
        grid=(B,),
        in_specs=[pl.BlockSpec((S, HALF_W), lambda b: (b, 0)),
                  _resident((len(POOL_WINDOWS), POOL_GROUP, POOL_GROUP)),
                  _resident((1, HALF_W))],
        out_specs=pl.BlockSpec((S, HALF_W), lambda b: (b, 0)),
        compiler_params=_cparams(1),
        name="pool",
    )(z, pw, ps)


def _s5_disc_kernel(lr_ref, li_ref, ldt_ref, lbr_ref, lbi_ref, cr_ref, ci_ref):
    lr = lr_ref[...]
    li = li_ref[...]
    dt = jnp.exp(ldt_ref[...])
    mag = jnp.exp(lr * dt)
    ang = li * dt
    lbr = mag * jnp.cos(ang)
    lbi = mag * jnp.sin(ang)
    nr = lbr - 1.0
    den = lr * lr + li * li
    lbr_ref[...] = lbr
    lbi_ref[...] = lbi
    cr_ref[...] = (nr * lr + lbi * li) / den
    ci_ref[...] = (lbi * lr - nr * li) / den


def _s5_bbar_kernel(cr_ref, ci_ref, br_ref, bi_ref, or_ref, oi_ref):
    cr = cr_ref[...]
    ci = ci_ref[...]
    br = br_ref[...]
    bi = bi_ref[...]
    or_ref[...] = cr * br - ci * bi
    oi_ref[...] = cr * bi + ci * br


def _s5_params(lam_re, lam_im, log_dt, b_re, b_im):
    gp = jax.ShapeDtypeStruct((S5_GROUPS, S5_STATE), F32)
    lbr, lbi, cr, ci = pl.pallas_call(_s5_disc_kernel, out_shape=(gp, gp, gp, gp), name="s5_disc")(
        lam_re, lam_im, log_dt.reshape(S5_GROUPS, 1))
    flat = jax.ShapeDtypeStruct((S5_MODES, S5_GROUP_CH), F32)
    bbr, bbi = pl.pallas_call(_s5_bbar_kernel, out_shape=(flat, flat), name="s5_bbar")(
        cr.reshape(S5_MODES, 1), ci.reshape(S5_MODES, 1),
        b_re.reshape(S5_MODES, S5_GROUP_CH), b_im.reshape(S5_MODES, S5_GROUP_CH))
    return lbr, lbi, bbr, bbi


def _s5_kernel(u_ref, bb_ref, lr_ref, li_ref, cr_ref, ci_ref, d_ref, wglu_ref, o_ref,
               xs_ref, st_ref, *, steps, nb):
    M = S5_MODES

    @pl.when(pl.program_id(0) == 0)
    def _():
        st_ref[...] = jnp.zeros_like(st_ref)

    u = u_ref[...]
    xs_ref[...] = _dot(u.astype(BF16), bb_ref[...])

    def step(t, carry):
        xr, xi = carry
        r0 = pl.multiple_of(t * nb, nb)
        lr = lr_ref[...]
        li = li_ref[...]
        nr = lr * xr - li * xi + xs_ref[pl.ds(r0, nb), 0:M]
        ni = lr * xi + li * xr + xs_ref[pl.ds(r0, nb), M:2 * M]
        xs_ref[pl.ds(r0, nb), 0:M] = nr
        xs_ref[pl.ds(r0, nb), M:2 * M] = ni
        return nr, ni

    xr, xi = lax.fori_loop(0, steps, step, (st_ref[:, 0:M], st_ref[:, M:2 * M]))
    st_ref[:, 0:M] = xr
    st_ref[:, M:2 * M] = xi

    y = _dot(xs_ref[:, 0:M].astype(BF16), cr_ref[...]) - _dot(xs_ref[:, M:2 * M].astype(BF16), ci_ref[...])
    y = jax.nn.gelu(y + d_ref[...] * u)
    ab = _dot(y.astype(BF16), wglu_ref[...])
    o_ref[...] = ab[:, :S5_WIDTH] * jax.nn.sigmoid(ab[:, S5_WIDTH:])


def _s5(u_tb, bbig, lr_b, li_b, cre, cim, d, wglu, *, nb, steps=64):
    R = u_tb.shape[0]
    rows = steps * nb
    return pl.pallas_call(
        functools.partial(_s5_kernel, steps=steps, nb=nb),
        out_shape=jax.ShapeDtypeStruct((R, S5_WIDTH), F32),
        grid=(R // rows,),
        in_specs=[pl.BlockSpec((rows, S5_WIDTH), lambda i: (i, 0)),
                  _resident((S5_WIDTH, 2 * S5_MODES)),
                  _resident((nb, S5_MODES)),
                  _resident((nb, S5_MODES)),
                  _resident((S5_MODES, S5_WIDTH)),
                  _resident((S5_MODES, S5_WIDTH)),
                  _resident((1, S5_WIDTH)),
                  _resident((S5_WIDTH, 2 * S5_WIDTH))],
        out_specs=pl.BlockSpec((rows, S5_WIDTH), lambda i: (i, 0)),
        scratch_shapes=[pltpu.VMEM((rows, 2 * S5_MODES), F32),
                        pltpu.VMEM((nb, 2 * S5_MODES), F32)],
        compiler_params=_cparams(1),
        name="s5",
    )(u_tb, bbig, lr_b, li_b, cre, cim, d, wglu)


def _rope_tables(S):
    inv = ROPE_THETA ** (-jnp.arange(HALF, dtype=F32) / HALF)
    ang = jnp.arange(S, dtype=F32)[:, None] * inv[None, :]
    cos, sin = jnp.cos(ang), jnp.sin(ang)
    cos_t = jnp.tile(cos, (1, LANES // HALF))
    sin_t = jnp.tile(jnp.concatenate([-sin, sin], axis=1), (1, LANES // HEAD_DIM))
    end_pos = (jnp.arange(S // CMP_STRIDE, dtype=F32) * CMP_STRIDE + (CMP_LEN - 1))[:, None] * inv[None, :]
    cosc = jnp.tile(jnp.cos(end_pos), (1, 2))
    sinc = jnp.tile(jnp.sin(end_pos), (1, 2))
    return cos_t, sin_t, cosc, sinc


def _selection_constants(S):
    ncmp = S // CMP_STRIDE
    j = jnp.arange(LANES, dtype=jnp.int32)[:, None]
    n = jnp.arange(ncmp, dtype=jnp.int32)[None, :]
    ovt = ((n * CMP_STRIDE < (j + 1) * SEL_BLOCK) & (n * CMP_STRIDE + CMP_LEN > j * SEL_BLOCK)
           & (j < S // SEL_BLOCK) & (n < ncmp - 1))
    key = jnp.arange(S, dtype=jnp.int32)[None, :]
    ex = (key // SEL_BLOCK) == j
    return ovt.astype(BF16), ex.astype(BF16)


def _even_mixer(x, g, w_in, w_out, gm_w_s, gm_b, pe, w1, w2, consts, *, B, S):
    cos_t, sin_t, cosc, sinc, ovt, ex = consts
    w_pad = jnp.pad(w_in, ((0, 0), (0, EVEN_IN_PAD - EVEN_IN))).astype(BF16)
    z = _even_proj(x, g, w_pad, cos_t, sin_t, S=S)
    out_a = _gmlp(z, gm_w_s, gm_b.T)
    nblk = S // CMP_STRIDE
    flat = CMP_STRIDE * HEAD_DIM
    to_blocks = lambda c0: (z[:, c0:c0 + LANES].reshape(B, S, NSA_KV, HEAD_DIM)
                            .transpose(0, 2, 1, 3).reshape(B, NSA_KV, nblk, flat))
    kcmp, vcmp = _compress(to_blocks(COL_KC), to_blocks(COL_VC),
                           pe.reshape(2, 1, CMP_LEN * HEAD_DIM), w1, w2, cosc, sinc)
    out_b = _nsa(z, kcmp, vcmp, ovt, ex, B=B, S=S)
    return _outproj(out_a, out_b, w_out.astype(BF16), x)


def _odd_mixer(x, g, w_in, w_out, pool_w, pool_scale, lam_re, lam_im, b_re, b_im, c_re, c_im,
               d_skip, log_dt, w_glu, *, B, S):
    z = _odd_proj(x, g, w_in.astype(BF16))
    y_c = _pool(z, pool_w, pool_scale.reshape(1, HALF_W), B=B, S=S)

    lbr, lbi, bbr, bbi = _s5_params(lam_re, lam_im, log_dt, b_re, b_im)
    eye = jnp.eye(S5_GROUPS, dtype=F32)
    blockdiag_in = lambda bb: jnp.einsum(
        'gpc,gh->gchp', bb.reshape(S5_GROUPS, S5_STATE, S5_GROUP_CH), eye).reshape(S5_WIDTH, S5_MODES)
    blockdiag_out = lambda cc: jnp.einsum('gcp,gh->gphc', cc, eye).reshape(S5_MODES, S5_WIDTH)
    bbig = jnp.concatenate([blockdiag_in(bbr), blockdiag_in(bbi)], axis=1).astype(BF16)
    bcast = lambda a: jnp.broadcast_to(a.reshape(1, S5_MODES), (B, S5_MODES))
    u_tb = z[:, HALF_W:].reshape(B, S, S5_WIDTH).transpose(1, 0, 2).reshape(S * B, S5_WIDTH)
    y_tb = _s5(u_tb, bbig, bcast(lbr), bcast(lbi),
               blockdiag_out(c_re).astype(BF16), blockdiag_out(c_im).astype(BF16),
               d_skip.reshape(1, S5_WIDTH), w_glu.astype(BF16), nb=B)
    y_d = y_tb.reshape(S, B, S5_WIDTH).transpose(1, 0, 2).reshape(B * S, S5_WIDTH)
    return _outproj(y_c, y_d, w_out.astype(BF16), x)


def kernel(x, norm_w, ffn_w_gate, ffn_w_up, ffn_w_down, final_norm_w, ev_w_in, ev_w_out, gm_w_s, gm_b,
           nsa_cmp_pe, nsa_cmp_w1, nsa_cmp_w2, od_w_in, od_w_out, pool_w, pool_scale, s5_lam_re,
           s5_lam_im, s5_b_re, s5_b_im, s5_c_re, s5_c_im, s5_d, s5_log_dt, s5_w_glu):
    B, S, _ = x.shape
    consts = _rope_tables(S) + _selection_constants(S)
    xt = x.reshape(B * S, D_MODEL)
    ffn = lambda xt, l, k: _ffn(xt, norm_w[l, 2 * k].reshape(1, D_MODEL), ffn_w_gate[l, k].astype(BF16),
                                ffn_w_up[l, k].astype(BF16), ffn_w_down[l, k].astype(BF16))
    for l in range(DEPTH):
        xt = ffn(xt, l, 0)
        g = norm_w[l, 1].reshape(1, D_MODEL)
        i = l // 2
        if l % 2 == 0:
            xt = _even_mixer(xt, g, ev_w_in[i], ev_w_out[i], gm_w_s[i], gm_b[i], nsa_cmp_pe[i],
                             nsa_cmp_w1[i], nsa_cmp_w2[i], consts, B=B, S=S)
        else:
            xt = _odd_mixer(xt, g, od_w_in[i], od_w_out[i], pool_w[i], pool_scale[i], s5_lam_re[i],
                            s5_lam_im[i], s5_b_re[i], s5_b_im[i], s5_c_re[i], s5_c_im[i], s5_d[i],
                            s5_log_dt[i], s5_w_glu[i], B=B, S=S)
        xt = ffn(xt, l, 1)
    return _final_norm(xt, final_norm_w.reshape(1, D_MODEL)).reshape(B, S, D_MODEL)
```

```python
import functools

import jax
import jax.numpy as jnp
from jax import lax
from jax.experimental import pallas as pl
from jax.experimental.pallas import tpu as pltpu

F32 = jnp.float32
BF16 = jnp.bfloat16

D_MODEL = 1024
DEPTH = 4
HEAD_DIM = 64
HALF = HEAD_DIM // 2
ROPE_THETA = 10000.0
N_HEADS = 8
GM_CHUNK = 128
NSA_KV = 2
NSA_HPG = 4
CMP_LEN = 32
CMP_STRIDE = 16
SEL_BLOCK = 64
SEL_TOPN = 8
WINDOW = 512
Q_BLOCK = 128
POOL_WINDOWS = (2, 4, 8, 16)
POOL_GROUP = 128
S5_GROUPS = 32
S5_GROUP_CH = 16
S5_STATE = 64
S5_WIDTH = 512
S5_MODES = S5_GROUPS * S5_STATE
FFN_DIM = 2816
RMS_EPS = 1e-6
LN_EPS = 1e-5
NEG_INF = -1e30
SEL_FORCE = 1e4
LOG2E = 1.4426950408889634

LANES = 128
HALF_W = 512
EVEN_IN = 2328
EVEN_IN_PAD = 2432
COL_U, COL_V, COL_Q = 0, 512, 1024
COL_KC, COL_VC, COL_KS, COL_VS, COL_KW, COL_VW, COL_GL = 1536, 1664, 1792, 1920, 2048, 2176, 2304
ROPE_SLABS = (1024, 1152, 1280, 1408, COL_KS, COL_KW)
SEL_STEP = 512
ONES_LANE = HEAD_DIM

VMEM_LIMIT = 56 * 1024 * 1024


def _cparams(n_axes):
    return pltpu.CompilerParams(dimension_semantics=("arbitrary",) * n_axes,
                                vmem_limit_bytes=VMEM_LIMIT)


def _resident(shape):
    nd = len(shape)
    return pl.BlockSpec(shape, lambda *_: (0,) * nd, pipeline_mode=pl.Buffered(1))


def _rms(x, g):
    ms = jnp.mean(x * x, axis=-1, keepdims=True)
    return x * lax.rsqrt(ms + RMS_EPS) * g


def _dot(a, b):
    return jnp.dot(a, b, preferred_element_type=F32)


def _dot_nt(a, b):
    return lax.dot_general(a, b, (((1,), (1,)), ((), ())), preferred_element_type=F32)


def _split_bf16(x):
    hi = x.astype(BF16)
    return hi, (x - hi.astype(F32)).astype(BF16)


def _ffn_kernel(x_ref, g_ref, wg_ref, wu_ref, wd_ref, o_ref, *, fc):
    x = x_ref[...]
    h = _rms(x, g_ref[...]).astype(BF16)
    acc = jnp.zeros_like(x)
    for c in range(FFN_DIM // fc):
        sl = slice(c * fc, (c + 1) * fc)
        a = _dot(h, wg_ref[:, sl])
        b = _dot(h, wu_ref[:, sl])
        t = (a * jax.nn.sigmoid(a) * b).astype(BF16)
        acc = acc + _dot(t, wd_ref[sl, :])
    o_ref[...] = x + 0.5 * acc


def _ffn(x, g, wg, wu, wd, *, tm=512, fc=256):
    T = x.shape[0]
    return pl.pallas_call(
        functools.partial(_ffn_kernel, fc=fc),
        out_shape=jax.ShapeDtypeStruct((T, D_MODEL), F32),
        grid=(T // tm,),
        in_specs=[pl.BlockSpec((tm, D_MODEL), lambda i: (i, 0)),
                  _resident((1, D_MODEL)),
                  _resident((D_MODEL, FFN_DIM)),
                  _resident((D_MODEL, FFN_DIM)),
                  _resident((FFN_DIM, D_MODEL))],
        out_specs=pl.BlockSpec((tm, D_MODEL), lambda i: (i, 0)),
        compiler_params=_cparams(1),
        name="ffn",
    )(x, g, wg, wu, wd)


def _proj_kernel(x_ref, g_ref, w_ref, o_ref):
    h = _rms(x_ref[...], g_ref[...]).astype(BF16)
    o_ref[...] = _dot(h, w_ref[...])


def _odd_proj(x, g, w, *, tm=512):
    T = x.shape[0]
    N = w.shape[1]
    return pl.pallas_call(
        _proj_kernel,
        out_shape=jax.ShapeDtypeStruct((T, N), F32),
        grid=(T // tm,),
        in_specs=[pl.BlockSpec((tm, D_MODEL), lambda i: (i, 0)),
                  _resident((1, D_MODEL)),
                  _resident((D_MODEL, N))],
        out_specs=pl.BlockSpec((tm, N), lambda i: (i, 0)),
        compiler_params=_cparams(1),
        name="odd_proj",
    )(x, g, w)


def _even_proj_kernel(x_ref, g_ref, w_ref, cos_ref, sin_ref, o_ref):
    h = _rms(x_ref[...], g_ref[...]).astype(BF16)
    o_ref[...] = _dot(h, w_ref[...])
    cos = cos_ref[...]
    sin = sin_ref[...]
    lane = lax.broadcasted_iota(jnp.int32, cos.shape, 1)
    first_half = (lane % HEAD_DIM) < HALF
    for c0 in ROPE_SLABS:
        xs = o_ref[:, c0:c0 + LANES]
        rot = jnp.where(first_half, pltpu.roll(xs, LANES - HALF, 1), pltpu.roll(xs, HALF, 1))
        o_ref[:, c0:c0 + LANES] = xs * cos + rot * sin


def _even_proj(x, g, w, cos, sin, *, S, tm=512):
    T = x.shape[0]
    spt = S // tm
    return pl.pallas_call(
        _even_proj_kernel,
        out_shape=jax.ShapeDtypeStruct((T, EVEN_IN_PAD), F32),
        grid=(T // tm,),
        in_specs=[pl.BlockSpec((tm, D_MODEL), lambda i: (i, 0)),
                  _resident((1, D_MODEL)),
                  _resident((D_MODEL, EVEN_IN_PAD)),
                  pl.BlockSpec((tm, LANES), lambda i: (i % spt, 0)),
                  pl.BlockSpec((tm, LANES), lambda i: (i % spt, 0))],
        out_specs=pl.BlockSpec((tm, EVEN_IN_PAD), lambda i: (i, 0)),
        compiler_params=_cparams(1),
        name="even_proj",
    )(x, g, w, cos, sin)


def _outproj_kernel(a_ref, b_ref, w_ref, x_ref, o_ref):
    acc = _dot(a_ref[...].astype(BF16), w_ref[0:HALF_W, :])
    acc = acc + _dot(b_ref[...].astype(BF16), w_ref[HALF_W:2 * HALF_W, :])
    o_ref[...] = x_ref[...] + acc


def _outproj(a, b, w, x, *, tm=512):
    T = x.shape[0]
    return pl.pallas_call(
        _outproj_kernel,
        out_shape=jax.ShapeDtypeStruct((T, D_MODEL), F32),
        grid=(T // tm,),
        in_specs=[pl.BlockSpec((tm, HALF_W), lambda i: (i, 0)),
                  pl.BlockSpec((tm, HALF_W), lambda i: (i, 0)),
                  _resident((D_MODEL, D_MODEL)),
                  pl.BlockSpec((tm, D_MODEL), lambda i: (i, 0))],
        out_specs=pl.BlockSpec((tm, D_MODEL), lambda i: (i, 0)),
        compiler_params=_cparams(1),
        name="outproj",
    )(a, b, w, x)


def _final_norm_kernel(x_ref, g_ref, o_ref):
    o_ref[...] = _rms(x_ref[...], g_ref[...])


def _final_norm(x, g, *, tm=512):
    T = x.shape[0]
    return pl.pallas_call(
        _final_norm_kernel,
        out_shape=jax.ShapeDtypeStruct((T, D_MODEL), F32),
        grid=(T // tm,),
        in_specs=[pl.BlockSpec((tm, D_MODEL), lambda i: (i, 0)), _resident((1, D_MODEL))],
        out_specs=pl.BlockSpec((tm, D_MODEL), lambda i: (i, 0)),
        compiler_params=_cparams(1),
        name="final_norm",
    )(x, g)


def _gmlp_kernel(u_ref, v_ref, ws_ref, avg_ref, bias_ref, o_ref, *, chunks):
    rows_all = N_HEADS * GM_CHUNK
    r = lax.broadcasted_iota(jnp.int32, (rows_all, GM_CHUNK), 0) & (GM_CHUNK - 1)
    c = lax.broadcasted_iota(jnp.int32, (rows_all, GM_CHUNK), 1)
    w_all = jnp.where(c <= r, ws_ref[...], 0.0).astype(BF16)
    avg = avg_ref[...]

    def head_mean(x):
        hi, lo = _split_bf16(x)
        return _dot(hi, avg) + _dot(lo, avg)

    v = jax.nn.gelu(v_ref[...])
    d = v - head_mean(v)
    vn = (d * lax.rsqrt(head_mean(d * d) + LN_EPS)).astype(BF16)
    lane_head = lax.broadcasted_iota(jnp.int32, (GM_CHUNK, HALF_W), 1) // HEAD_DIM
    for ci in range(chunks):
        rows = slice(ci * GM_CHUNK, (ci + 1) * GM_CHUNK)
        res = _dot(w_all, vn[rows, :])
        s = res[0:GM_CHUNK, :]
        for h in range(1, N_HEADS):
            s = jnp.where(lane_head == h, res[h * GM_CHUNK:(h + 1) * GM_CHUNK, :], s)
        o_ref[rows, :] = jax.nn.gelu(u_ref[rows, :]) * (s + bias_ref[...])


def _gmlp(z, ws, avg, bias, *, chunks=4):
    T = z.shape[0]
    tm = chunks * GM_CHUNK
    return pl.pallas_call(
        functools.partial(_gmlp_kernel, chunks=chunks),
        out_shape=jax.ShapeDtypeStruct((T, HALF_W), F32),
        grid=(T // tm,),
        in_specs=[pl.BlockSpec((tm, HALF_W), lambda i: (i, COL_U // HALF_W)),
                  pl.BlockSpec((tm, HALF_W), lambda i: (i, COL_V // HALF_W)),
                  _resident((N_HEADS * GM_CHUNK, GM_CHUNK)),
                  _resident((HALF_W, HALF_W)),
                  _resident((GM_CHUNK, HALF_W))],
        out_specs=pl.BlockSpec((tm, HALF_W), lambda i: (i, 0)),
        compiler_params=_cparams(1),
        name="gmlp",
    )(z, z, ws, avg, bias)


def _cmp_kernel(k2_ref, v2_ref, pe_ref, w1_ref, w2_ref, cos_ref, sin_ref, ko_ref, vo_ref):
    half_flat = CMP_STRIDE * HEAD_DIM
    for which, (src, dst) in enumerate(((k2_ref, ko_ref), (v2_ref, vo_ref))):
        pe = pe_ref[which]
        w1 = w1_ref[which]
        w2 = w2_ref[which].astype(BF16)
        for g in range(NSA_KV):
            a = src[0, g]
            a_next = pltpu.roll(a, a.shape[0] - 1, 0)
            pre = _dot((a + pe[:, :half_flat]).astype(BF16), w1[:half_flat, :].astype(BF16))
            pre = pre + _dot((a_next + pe[:, half_flat:]).astype(BF16), w1[half_flat:, :].astype(BF16))
            cmp = _dot(jax.nn.gelu(pre).astype(BF16), w2)
            if which == 0:
                rot = jnp.concatenate([-cmp[:, HALF:], cmp[:, :HALF]], axis=1)
                cmp = cmp * cos_ref[...] + rot * sin_ref[...]
            dst[0, g] = jnp.concatenate([cmp, jnp.zeros_like(cmp)], axis=1).astype(BF16)


def _compress(k2, v2, pe, w1, w2, cosc, sinc):
    B, _, nblk, flat = k2.shape
    out = jax.ShapeDtypeStruct((B, NSA_KV, nblk, LANES), BF16)
    out_spec = pl.BlockSpec((1, NSA_KV, nblk, LANES), lambda b: (b, 0, 0, 0))
    return pl.pallas_call(
        _cmp_kernel,
        out_shape=(out, out),
        grid=(B,),
        in_specs=[pl.BlockSpec((1, NSA_KV, nblk, flat), lambda b: (b, 0, 0, 0)),
                  pl.BlockSpec((1, NSA_KV, nblk, flat), lambda b: (b, 0, 0, 0)),
                  _resident((2, 1, 2 * flat)),
                  _resident((2, 2 * flat, HEAD_DIM)),
                  _resident((2, HEAD_DIM, HEAD_DIM)),
                  _resident((nblk, HEAD_DIM)),
                  _resident((nblk, HEAD_DIM))],
        out_specs=(out_spec, out_spec),
        compiler_params=_cparams(1),
        name="nsa_compress",
    )(k2, v2, pe, w1, w2, cosc, sinc)


def _softmax_pv(pieces):
    m = None
    for s, _ in pieces:
        pm = jnp.max(s, axis=-1, keepdims=True)
        m = pm if m is None else jnp.maximum(m, pm)
    acc = None
    for s, vv in pieces:
        part = _dot(jnp.exp2(s - m).astype(BF16), vv)
        acc = part if acc is None else acc + part
    return acc / acc[:, ONES_LANE:ONES_LANE + 1]


def _nsa_kernel(q_ref, gl_ref, ksa_ref, vsa_ref, kwa_ref, vwa_ref, kc_ref, vc_ref, ovt_ref, o_ref,
                osel_ref, owin_ref, *, S):
    TQ = Q_BLOCK
    R = NSA_HPG * TQ
    ncmp = S // CMP_STRIDE
    nsel = S // SEL_BLOCK
    i = pl.program_id(1)
    t0 = i * TQ

    q = q_ref[...] * (HEAD_DIM ** -0.5 * LOG2E)
    gate = jax.nn.sigmoid(gl_ref[...])
    lane = lax.broadcasted_iota(jnp.int32, (TQ, LANES), 1)
    lane4 = lax.broadcasted_iota(jnp.int32, (R, LANES), 1)
    trow4 = t0 + (lax.broadcasted_iota(jnp.int32, (R, 1), 0) & (TQ - 1))

    ncol = lax.broadcasted_iota(jnp.int32, (R, ncmp), 1)
    valid_c = (ncol * CMP_STRIDE + (CMP_LEN - 1) <= trow4) & (ncol < ncmp - 1)
    any_c = (trow4 >= CMP_LEN - 1).astype(F32)

    jrow = lax.broadcasted_iota(jnp.int32, (nsel, TQ), 0)
    tl = t0 + lax.broadcasted_iota(jnp.int32, (nsel, TQ), 1)
    cur = jnp.right_shift(tl, 6)
    forced = (jrow == 0) | (jrow == cur) | (jrow == cur - 1)
    causal_blk = jrow * SEL_BLOCK <= tl
    ovt = ovt_ref[...]

    q4a, o_cmp = [], []
    for g in range(NSA_KV):
        heads = []
        for hq in range(NSA_HPG):
            h = g * NSA_HPG + hq
            slab = q[:, (h // 2) * LANES:(h // 2 + 1) * LANES]
            if h % 2 == 1:
                slab = pltpu.roll(slab, HEAD_DIM, 1)
            heads.append(jnp.where(lane < HEAD_DIM, slab, 0.0))
        q4 = jnp.concatenate(heads, axis=0)

        s_c = jnp.where(valid_c, _dot_nt(q4.astype(BF16), kc_ref[0, g]), NEG_INF)
        e_c = jnp.exp2(s_c - jnp.max(s_c, axis=-1, keepdims=True))
        p_c = e_c / jnp.sum(e_c, axis=-1, keepdims=True) * any_c
        o_cmp.append(_dot(p_c.astype(BF16), vc_ref[0, g]))
        p_sum = (p_c[0:TQ] + p_c[TQ:2 * TQ]) + (p_c[2 * TQ:3 * TQ] + p_c[3 * TQ:4 * TQ])

        p_hi, p_lo = _split_bf16(p_sum)
        imp_t = (_dot_nt(ovt, p_hi) + _dot_nt(ovt, p_lo))[0:nsel, :]
        imp_t = jnp.where(forced, SEL_FORCE, jnp.where(causal_blk, imp_t, -SEL_FORCE))
        rank = jnp.zeros((nsel, TQ), jnp.int32)
        for jp in range(nsel):
            row = imp_t[jp:jp + 1, :]
            beats = (row > imp_t) | ((row == imp_t) & (jrow > jp))
            rank = rank + beats.astype(jnp.int32)
        sel_t = (rank < SEL_TOPN).astype(F32)
        if nsel < LANES:
            sel_t = jnp.concatenate([sel_t, jnp.zeros((LANES - nsel, TQ), F32)], axis=0)
        bias = (pltpu.roll(sel_t.T, HEAD_DIM, 1) - 1.0) * (-NEG_INF)
        q4a.append(jnp.where(lane4 < HEAD_DIM, q4, jnp.concatenate([bias] * NSA_HPG, axis=0)).astype(BF16))

    r4 = lax.broadcasted_iota(jnp.int32, (R, 1), 0) & (TQ - 1)
    col_step = lax.broadcasted_iota(jnp.int32, (R, SEL_STEP), 1)
    for n in range(S // SEL_STEP):
        @pl.when(i // (SEL_STEP // TQ) == n)
        def _(n=n):
            head = n * SEL_STEP
            for g in range(NSA_KV):
                s = _dot_nt(q4a[g], ksa_ref[0, g, 0:head + SEL_STEP, :])
                tail = jnp.where(head + col_step <= trow4, s[:, head:], NEG_INF)
                pieces = [(tail, vsa_ref[0, g, head:head + SEL_STEP, :])]
                if n > 0:
                    pieces.append((s[:, :head], vsa_ref[0, g, 0:head, :]))
                osel_ref[g] = _softmax_pv(pieces)

    n_back = WINDOW // TQ
    col = lax.broadcasted_iota(jnp.int32, (R, TQ), 1)

    @pl.when(i >= n_back)
    def _():
        k0 = pl.multiple_of((i - n_back) * TQ, TQ)
        for g in range(NSA_KV):
            s = _dot_nt(q4a[g], kwa_ref[0, g, pl.ds(k0, WINDOW + TQ), :])
            vv = vwa_ref[0, g, pl.ds(k0, WINDOW + TQ), :]
            owin_ref[g] = _softmax_pv([(jnp.where(col > r4, s[:, :TQ], NEG_INF), vv[:TQ]),
                                       (s[:, TQ:WINDOW], vv[TQ:WINDOW]),
                                       (jnp.where(col <= r4, s[:, WINDOW:], NEG_INF), vv[WINDOW:])])

    @pl.when(i < n_back)
    def _():
        colw = lax.broadcasted_iota(jnp.int32, (R, WINDOW), 1)
        for g in range(NSA_KV):
            s = _dot_nt(q4a[g], kwa_ref[0, g, 0:WINDOW, :])
            owin_ref[g] = _softmax_pv([(jnp.where(colw <= trow4, s, NEG_INF), vwa_ref[0, g, 0:WINDOW, :])])

    for g in range(NSA_KV):
        o_s = osel_ref[g]
        o_w = owin_ref[g]
        outs = []
        for hq in range(NSA_HPG):
            h = g * NSA_HPG + hq
            rows = slice(hq * TQ, (hq + 1) * TQ)
            outs.append(gate[:, 3 * h:3 * h + 1] * o_cmp[g][rows] + gate[:, 3 * h + 1:3 * h + 2] * o_s[rows]
                        + gate[:, 3 * h + 2:3 * h + 3] * o_w[rows])
        for pair in range(NSA_HPG // 2):
            c0 = (g * NSA_HPG + 2 * pair) * HEAD_DIM
            o_ref[:, c0:c0 + LANES] = jnp.where(lane < HEAD_DIM, outs[2 * pair],
                                                pltpu.roll(outs[2 * pair + 1], HEAD_DIM, 1))


def _nsa(z, ksa, vsa, kwa, vwa, kcmp, vcmp, ovt, *, B, S):
    T = z.shape[0]
    nq = S // Q_BLOCK
    ncmp = S // CMP_STRIDE
    per_batch = lambda rows: pl.BlockSpec((1, NSA_KV, rows, LANES), lambda b, i: (b, 0, 0, 0))
    return pl.pallas_call(
        functools.partial(_nsa_kernel, S=S),
        out_shape=jax.ShapeDtypeStruct((T, HALF_W), F32),
        grid=(B, nq),
        in_specs=[pl.BlockSpec((Q_BLOCK, HALF_W), lambda b, i: (b * nq + i, COL_Q // HALF_W)),
                  pl.BlockSpec((Q_BLOCK, LANES), lambda b, i: (b * nq + i, COL_GL // LANES)),
                  per_batch(S), per_batch(S), per_batch(S), per_batch(S),
                  per_batch(ncmp), per_batch(ncmp),
                  _resident((LANES, ncmp))],
        out_specs=pl.BlockSpec((Q_BLOCK, HALF_W), lambda b, i: (b * nq + i, 0)),
        scratch_shapes=[pltpu.VMEM((NSA_KV, NSA_HPG * Q_BLOCK, LANES), F32),
                        pltpu.VMEM((NSA_KV, NSA_HPG * Q_BLOCK, LANES), F32)],
        compiler_params=_cparams(2),
        name="nsa_attn",
    )(z, z, ksa, vsa, kwa, vwa, kcmp, vcmp, ovt)


def _pool_kernel(z_ref, pw_ref, ps_ref, o_ref):
    S = z_ref.shape[0]
    row = lax.broadcasted_iota(jnp.int32, (S, POOL_GROUP), 0)
    for gi, w in enumerate(POOL_WINDOWS):
        cols = slice(gi * POOL_GROUP, (gi + 1) * POOL_GROUP)
        zg = z_ref[:, cols]
        acc = zg
        k = 1
        while k < w:
            acc = acc + jnp.where(row >= k, pltpu.roll(acc, k, 0), 0.0)
            k *= 2
        cnt = jnp.minimum(row + 1, w).astype(F32)
        pooled = acc / cnt - zg
        y = _dot(pooled.astype(BF16), pw_ref[gi].astype(BF16))
        o_ref[:, cols] = y * ps_ref[:, cols]


def _pool(z, pw, ps, *, B, S):
    T = z.shape[0]
    return pl.pallas_call(
        _pool_kernel,
        out_shape=jax.ShapeDtypeStruct((T, HALF_W), F32),
        grid=(B,),
        in_specs=[pl.BlockSpec((S, HALF_W), lambda b: (b, 0)),
                  _resident((len(POOL_WINDOWS), POOL_GROUP, POOL_GROUP)),
                  _resident((1, HALF_W))],
        out_specs=pl.BlockSpec((S, HALF_W), lambda b: (b, 0)),
        compiler_params=_cparams(1),
        name="pool",
    )(z, pw, ps)


def _s5_disc_kernel(lr_ref, li_ref, ldt_ref, lbr_ref, lbi_ref, cr_ref, ci_ref):
    lr = lr_ref[...]
    li = li_ref[...]
    dt = jnp.exp(ldt_ref[...])
    mag = jnp.exp(lr * dt)
    ang = li * dt
    lbr = mag * jnp.cos(ang)
    lbi = mag * jnp.sin(ang)
    nr = lbr - 1.0
    den = lr * lr + li * li
    lbr_ref[...] = lbr
    lbi_ref[...] = lbi
    cr_ref[...] = (nr * lr + lbi * li) / den
    ci_ref[...] = (lbi * lr - nr * li) / den


def _s5_bbar_kernel(cr_ref, ci_ref, br_ref, bi_ref, or_ref, oi_ref):
    cr = cr_ref[...]
    ci = ci_ref[...]
    br = br_ref[...]
    bi = bi_ref[...]
    or_ref[...] = cr * br - ci * bi
    oi_ref[...] = cr * bi + ci * br


def _s5_params(lam_re, lam_im, log_dt, b_re, b_im):
    gp = jax.ShapeDtypeStruct((S5_GROUPS, S5_STATE), F32)
    lbr, lbi, cr, ci = pl.pallas_call(_s5_disc_kernel, out_shape=(gp, gp, gp, gp), name="s5_disc")(
        lam_re, lam_im, log_dt.reshape(S5_GROUPS, 1))
    flat = jax.ShapeDtypeStruct((S5_MODES, S5_GROUP_CH), F32)
    bbr, bbi = pl.pallas_call(_s5_bbar_kernel, out_shape=(flat, flat), name="s5_bbar")(
        cr.reshape(S5_MODES, 1), ci.reshape(S5_MODES, 1),
        b_re.reshape(S5_MODES, S5_GROUP_CH), b_im.reshape(S5_MODES, S5_GROUP_CH))
    return lbr, lbi, bbr, bbi


def _s5_kernel(u_ref, bb_ref, lr_ref, li_ref, cr_ref, ci_ref, d_ref, wglu_ref, o_ref,
               xs_ref, st_ref, *, steps, nb):
    M = S5_MODES

    @pl.when(pl.program_id(0) == 0)
    def _():
        st_ref[...] = jnp.zeros_like(st_ref)

    u = u_ref[...]
    xs_ref[...] = _dot(u.astype(BF16), bb_ref[...])

    def step(t, carry):
        xr, xi = carry
        r0 = pl.multiple_of(t * nb, nb)
        lr = lr_ref[...]
        li = li_ref[...]
        nr = lr * xr - li * xi + xs_ref[pl.ds(r0, nb), 0:M]
        ni = lr * xi + li * xr + xs_ref[pl.ds(r0, nb), M:2 * M]
        xs_ref[pl.ds(r0, nb), 0:M] = nr
        xs_ref[pl.ds(r0, nb), M:2 * M] = ni
        return nr, ni

    xr, xi = lax.fori_loop(0, steps, step, (st_ref[:, 0:M], st_ref[:, M:2 * M]))
    st_ref[:, 0:M] = xr
    st_ref[:, M:2 * M] = xi

    y = _dot(xs_ref[:, 0:M].astype(BF16), cr_ref[...]) - _dot(xs_ref[:, M:2 * M].astype(BF16), ci_ref[...])
    y = jax.nn.gelu(y + d_ref[...] * u)
    ab = _dot(y.astype(BF16), wglu_ref[...])
    o_ref[...] = ab[:, :S5_WIDTH] * jax.nn.sigmoid(ab[:, S5_WIDTH:])


def _s5(u_tb, bbig, lr_b, li_b, cre, cim, d, wglu, *, nb, steps=64):
    R = u_tb.shape[0]
    rows = steps * nb
    return pl.pallas_call(
        functools.partial(_s5_kernel, steps=steps, nb=nb),
        out_shape=jax.ShapeDtypeStruct((R, S5_WIDTH), F32),
        grid=(R // rows,),
        in_specs=[pl.BlockSpec((rows, S5_WIDTH), lambda i: (i, 0)),
                  _resident((S5_WIDTH, 2 * S5_MODES)),
                  _resident((nb, S5_MODES)),
                  _resident((nb, S5_MODES)),
                  _resident((S5_MODES, S5_WIDTH)),
                  _resident((S5_MODES, S5_WIDTH)),
                  _resident((1, S5_WIDTH)),
                  _resident((S5_WIDTH, 2 * S5_WIDTH))],
        out_specs=pl.BlockSpec((rows, S5_WIDTH), lambda i: (i, 0)),
        scratch_shapes=[pltpu.VMEM((rows, 2 * S5_MODES), F32),
                        pltpu.VMEM((nb, 2 * S5_MODES), F32)],
        compiler_params=_cparams(1),
        name="s5",
    )(u_tb, bbig, lr_b, li_b, cre, cim, d, wglu)


def _rope_tables(S):
    inv = ROPE_THETA ** (-jnp.arange(HALF, dtype=F32) / HALF)
    ang = jnp.arange(S, dtype=F32)[:, None] * inv[None, :]
    cos, sin = jnp.cos(ang), jnp.sin(ang)
    cos_t = jnp.tile(cos, (1, LANES // HALF))
    sin_t = jnp.tile(jnp.concatenate([-sin, sin], axis=1), (1, LANES // HEAD_DIM))
    end_pos = (jnp.arange(S // CMP_STRIDE, dtype=F32) * CMP_STRIDE + (CMP_LEN - 1))[:, None] * inv[None, :]
    cosc = jnp.tile(jnp.cos(end_pos), (1, 2))
    sinc = jnp.tile(jnp.sin(end_pos), (1, 2))
    return cos_t, sin_t, cosc, sinc


def _structure_constants(S):
    ncmp = S // CMP_STRIDE
    nsel = S // SEL_BLOCK
    assert nsel <= LANES - HEAD_DIM
    j = jnp.arange(LANES, dtype=jnp.int32)[:, None]
    n = jnp.arange(ncmp, dtype=jnp.int32)[None, :]
    ovt = ((n * CMP_STRIDE < (j + 1) * SEL_BLOCK) & (n * CMP_STRIDE + CMP_LEN > j * SEL_BLOCK)
           & (j < nsel) & (n < ncmp - 1)).astype(BF16)
    extra_lane = jnp.arange(LANES - HEAD_DIM, dtype=jnp.int32)[None, :]
    key_blk = (jnp.arange(S, dtype=jnp.int32) // SEL_BLOCK)[:, None]
    blk_onehot = (key_blk == extra_lane).astype(BF16)
    ones_lane = jnp.broadcast_to(extra_lane == 0, (S, LANES - HEAD_DIM)).astype(BF16)
    no_extra = jnp.zeros((S, LANES - HEAD_DIM), BF16)
    c = jnp.arange(HALF_W, dtype=jnp.int32)
    avg = ((c[:, None] // HEAD_DIM == c[None, :] // HEAD_DIM).astype(F32) / HEAD_DIM).astype(BF16)
    return ovt, blk_onehot, ones_lane, no_extra, avg


def _even_mixer(x, g, w_in, w_out, gm_w_s, gm_b, pe, w1, w2, consts, *, B, S):
    cos_t, sin_t, cosc, sinc, ovt, blk_onehot, ones_lane, no_extra, avg = consts
    w_pad = jnp.pad(w_in, ((0, 0), (0, EVEN_IN_PAD - EVEN_IN))).astype(BF16)
    z = _even_proj(x, g, w_pad, cos_t, sin_t, S=S)
    out_a = _gmlp(z, gm_w_s.reshape(N_HEADS * GM_CHUNK, GM_CHUNK), avg,
                  jnp.repeat(gm_b.T, HEAD_DIM, axis=1))
    nblk = S // CMP_STRIDE
    flat = CMP_STRIDE * HEAD_DIM
    per_group = lambda c0: (z[:, c0:c0 + LANES].reshape(B, S, NSA_KV, HEAD_DIM).transpose(0, 2, 1, 3))
    to_blocks = lambda c0: per_group(c0).reshape(B, NSA_KV, nblk, flat)
    augment = lambda c0, extra: jnp.concatenate(
        [per_group(c0).astype(BF16), jnp.broadcast_to(extra, (B, NSA_KV) + extra.shape)], axis=-1)
    kcmp, vcmp = _compress(to_blocks(COL_KC), to_blocks(COL_VC),
                           pe.reshape(2, 1, CMP_LEN * HEAD_DIM), w1, w2, cosc, sinc)
    out_b = _nsa(z, augment(COL_KS, blk_onehot), augment(COL_VS, ones_lane), augment(COL_KW, no_extra),
                 augment(COL_VW, ones_lane), kcmp, vcmp, ovt, B=B, S=S)
    return _outproj(out_a, out_b, w_out.astype(BF16), x)


def _odd_mixer(x, g, w_in, w_out, pool_w, pool_scale, lam_re, lam_im, b_re, b_im, c_re, c_im,
               d_skip, log_dt, w_glu, *, B, S):
    z = _odd_proj(x, g, w_in.astype(BF16))
    y_c = _pool(z, pool_w, pool_scale.reshape(1, HALF_W), B=B, S=S)

    lbr, lbi, bbr, bbi = _s5_params(lam_re, lam_im, log_dt, b_re, b_im)
    eye = jnp.eye(S5_GROUPS, dtype=F32)
    blockdiag_in = lambda bb: jnp.einsum(
        'gpc,gh->gchp', bb.reshape(S5_GROUPS, S5_STATE, S5_GROUP_CH), eye).reshape(S5_WIDTH, S5_MODES)
    blockdiag_out = lambda cc: jnp.einsum('gcp,gh->gphc', cc, eye).reshape(S5_MODES, S5_WIDTH)
    bbig = jnp.concatenate([blockdiag_in(bbr), blockdiag_in(bbi)], axis=1).astype(BF16)
    bcast = lambda a: jnp.broadcast_to(a.reshape(1, S5_MODES), (B, S5_MODES))
    u_tb = z[:, HALF_W:].reshape(B, S, S5_WIDTH).transpose(1, 0, 2).reshape(S * B, S5_WIDTH)
    y_tb = _s5(u_tb, bbig, bcast(lbr), bcast(lbi),
               blockdiag_out(c_re).astype(BF16), blockdiag_out(c_im).astype(BF16),
               d_skip.reshape(1, S5_WIDTH), w_glu.astype(BF16), nb=B)
    y_d = y_tb.reshape(S, B, S5_WIDTH).transpose(1, 0, 2).reshape(B * S, S5_WIDTH)
    return _outproj(y_c, y_d, w_out.astype(BF16), x)


def kernel(x, norm_w, ffn_w_gate, ffn_w_up, ffn_w_down, final_norm_w, ev_w_in, ev_w_out, gm_w_s, gm_b,
           nsa_cmp_pe, nsa_cmp_w1, nsa_cmp_w2, od_w_in, od_w_out, pool_w, pool_scale, s5_lam_re,
           s5_lam_im, s5_b_re, s5_b_im, s5_c_re, s5_c_im, s5_d, s5_log_dt, s5_w_glu):
    B, S, _ = x.shape
    consts = _rope_tables(S) + _structure_constants(S)
    xt = x.reshape(B * S, D_MODEL)
    ffn = lambda xt, l, k: _ffn(xt, norm_w[l, 2 * k].reshape(1, D_MODEL), ffn_w_gate[l, k].astype(BF16),
                                ffn_w_up[l, k].astype(BF16), ffn_w_down[l, k].astype(BF16))
    for l in range(DEPTH):
        xt = ffn(xt, l, 0)
        g = norm_w[l, 1].reshape(1, D_MODEL)
        i = l // 2
        if l % 2 == 0:
            xt = _even_mixer(xt, g, ev_w_in[i], ev_w_out[i], gm_w_s[i], gm_b[i], nsa_cmp_pe[i],
                             nsa_cmp_w1[i], nsa_cmp_w2[i], consts, B=B, S=S)
        else:
            xt = _odd_mixer(xt, g, od_w_in[i], od_w_out[i], pool_w[i], pool_scale[i], s5_lam_re[i],
                            s5_lam_im[i], s5_b_re[i], s5_b_im[i], s5_c_re[i], s5_c_im[i], s5_d[i],
                            s5_log_dt[i], s5_w_glu[i], B=B, S=S)
        xt = ffn(xt, l, 1)
    return _final_norm(xt, final_norm_w.reshape(1, D_MODEL)).reshape(B, S, D_MODEL)
```

```python
import functools

import jax
import jax.numpy as jnp
from jax import lax
from jax.experimental import pallas as pl
from jax.experimental.pallas import tpu as pltpu

F32 = jnp.float32
BF16 = jnp.bfloat16

D_MODEL = 1024
DEPTH = 4
HEAD_DIM = 64
HALF = HEAD_DIM // 2
ROPE_THETA = 10000.0
N_HEADS = 8
GM_CHUNK = 128
NSA_KV = 2
NSA_HPG = 4
CMP_LEN = 32
CMP_STRIDE = 16
SEL_BLOCK = 64
SEL_TOPN = 8
WINDOW = 512
Q_BLOCK = 128
POOL_WINDOWS = (2, 4, 8, 16)
POOL_GROUP = 128
S5_GROUPS = 32
S5_GROUP_CH = 16
S5_STATE = 64
S5_WIDTH = 512
S5_MODES = S5_GROUPS * S5_STATE
S5_SUPER = 4
S5_SUPER_CH = S5_WIDTH // S5_SUPER
S5_SUPER_MODES = S5_MODES // S5_SUPER
FFN_DIM = 2816
RMS_EPS = 1e-6
LN_EPS = 1e-5
NEG_INF = -1e30
SEL_FORCE = 1e4
LOG2E = 1.4426950408889634

LANES = 128
HALF_W = 512
EVEN_IN = 2328
EVEN_IN_PAD = 2432
COL_U, COL_V, COL_Q = 0, 512, 1024
COL_KC, COL_VC, COL_KS, COL_VS, COL_KW, COL_VW, COL_GL = 1536, 1664, 1792, 1920, 2048, 2176, 2304
SEL_STEP = 512
ONES_LANE = HEAD_DIM

VMEM_LIMIT = 56 * 1024 * 1024


def _cparams(n_axes):
    return pltpu.CompilerParams(dimension_semantics=("arbitrary",) * n_axes,
                                vmem_limit_bytes=VMEM_LIMIT)


def _resident(shape):
    nd = len(shape)
    return pl.BlockSpec(shape, lambda *_: (0,) * nd, pipeline_mode=pl.Buffered(1))


def _rms(x, g):
    ms = jnp.mean(x * x, axis=-1, keepdims=True)
    return x * lax.rsqrt(ms + RMS_EPS) * g


def _dot(a, b):
    return jnp.dot(a, b, preferred_element_type=F32)


def _dot_nt(a, b):
    return lax.dot_general(a, b, (((1,), (1,)), ((), ())), preferred_element_type=F32)


def _split_bf16(x):
    hi = x.astype(BF16)
    return hi, (x - hi.astype(F32)).astype(BF16)


def _swiglu_half_step(x, g, wg_ref, wu_ref, wd_ref, fc):
    h = _rms(x, g).astype(BF16)
    acc = jnp.zeros_like(x)
    for c in range(FFN_DIM // fc):
        sl = slice(c * fc, (c + 1) * fc)
        a = _dot(h, wg_ref[:, sl])
        b = _dot(h, wu_ref[:, sl])
        t = (a * jax.nn.sigmoid(a) * b).astype(BF16)
        acc = acc + _dot(t, wd_ref[sl, :])
    return x + 0.5 * acc


def _ffn_kernel(x_ref, g_ref, wg_ref, wu_ref, wd_ref, o_ref, *, fc):
    o_ref[...] = _swiglu_half_step(x_ref[...], g_ref[...], wg_ref, wu_ref, wd_ref, fc)


def _mix_ffn_kernel(a_ref, b_ref, wo_ref, x_ref, g_ref, wg_ref, wu_ref, wd_ref, gf_ref, o_ref, *, fc, final):
    x = x_ref[...] + _dot(a_ref[...].astype(BF16), wo_ref[0:HALF_W, :])
    x = x + _dot(b_ref[...].astype(BF16), wo_ref[HALF_W:2 * HALF_W, :])
    x = _swiglu_half_step(x, g_ref[...], wg_ref, wu_ref, wd_ref, fc)
    o_ref[...] = _rms(x, gf_ref[...]) if final else x


def _ffn_weight_specs():
    return [_resident((1, D_MODEL)), _resident((D_MODEL, FFN_DIM)), _resident((D_MODEL, FFN_DIM)),
            _resident((FFN_DIM, D_MODEL))]


def _ffn(x, g, wg, wu, wd, *, tm=512, fc=256):
    T = x.shape[0]
    return pl.pallas_call(
        functools.partial(_ffn_kernel, fc=fc),
        out_shape=jax.ShapeDtypeStruct((T, D_MODEL), F32),
        grid=(T // tm,),
        in_specs=[pl.BlockSpec((tm, D_MODEL), lambda i: (i, 0))] + _ffn_weight_specs(),
        out_specs=pl.BlockSpec((tm, D_MODEL), lambda i: (i, 0)),
        compiler_params=_cparams(1),
        name="ffn",
    )(x, g, wg, wu, wd)


def _mix_ffn(a, b, b_spec, wo, x, g, wg, wu, wd, gf, *, final, tm=512, fc=256):
    T = x.shape[0]
    return pl.pallas_call(
        functools.partial(_mix_ffn_kernel, fc=fc, final=final),
        out_shape=jax.ShapeDtypeStruct((T, D_MODEL), F32),
        grid=(T // tm,),
        in_specs=[pl.BlockSpec((tm, HALF_W), lambda i: (i, 0)), b_spec,
                  _resident((D_MODEL, D_MODEL)),
                  pl.BlockSpec((tm, D_MODEL), lambda i: (i, 0))] + _ffn_weight_specs()
                 + [_resident((1, D_MODEL))],
        out_specs=pl.BlockSpec((tm, D_MODEL), lambda i: (i, 0)),
        compiler_params=_cparams(1),
        name="mix_ffn",
    )(a, b, wo, x, g, wg, wu, wd, gf)


def _odd_proj_kernel(x_ref, g_ref, w_ref, zc_ref, u_ref):
    h = _rms(x_ref[...], g_ref[...]).astype(BF16)
    z = _dot(h, w_ref[...])
    zc_ref[...] = z[:, :HALF_W]
    u_ref[...] = z[:, HALF_W:]


def _odd_proj(x, g, w, *, B, S, tm=512):
    T = x.shape[0]
    spt = S // tm
    return pl.pallas_call(
        _odd_proj_kernel,
        out_shape=(jax.ShapeDtypeStruct((T, HALF_W), F32), jax.ShapeDtypeStruct((S, B * HALF_W), F32)),
        grid=(T // tm,),
        in_specs=[pl.BlockSpec((tm, D_MODEL), lambda i: (i, 0)),
                  _resident((1, D_MODEL)),
                  _resident((D_MODEL, 2 * HALF_W))],
        out_specs=(pl.BlockSpec((tm, HALF_W), lambda i: (i, 0)),
                   pl.BlockSpec((tm, HALF_W), lambda i: (i % spt, i // spt))),
        compiler_params=_cparams(1),
        name="odd_proj",
    )(x, g, w)


def _even_proj_kernel(x_ref, g_ref, w_ref, cos_ref, sin_ref, blk_ref,
                      uv_ref, q_ref, gl_ref, kc_ref, vc_ref, ksa_ref, vsa_ref, kwa_ref, vwa_ref):
    h = _rms(x_ref[...], g_ref[...]).astype(BF16)
    z = _dot(h, w_ref[...])
    cos = cos_ref[...]
    sin = sin_ref[...]
    lane = lax.broadcasted_iota(jnp.int32, cos.shape, 1)
    first_half = (lane % HEAD_DIM) < HALF
    low = lane < HEAD_DIM

    def slab(c0, rotary):
        xs = z[:, c0:c0 + LANES]
        if not rotary:
            return xs
        rot = jnp.where(first_half, pltpu.roll(xs, LANES - HALF, 1), pltpu.roll(xs, HALF, 1))
        return xs * cos + rot * sin

    uv_ref[...] = z[:, COL_U:COL_Q]
    for c in range(HALF_W // LANES):
        q_ref[:, c * LANES:(c + 1) * LANES] = slab(COL_Q + c * LANES, True)
    gl_ref[...] = z[:, COL_GL:COL_GL + LANES]
    for c0, dst in ((COL_KC, kc_ref), (COL_VC, vc_ref)):
        xs = slab(c0, False)
        for g in range(NSA_KV):
            dst[0, g] = xs[:, g * HEAD_DIM:(g + 1) * HEAD_DIM]
    ones = (lane == ONES_LANE).astype(F32)
    for c0, rotary, extra, dst in ((COL_KS, True, blk_ref[...], ksa_ref), (COL_VS, False, ones, vsa_ref),
                                   (COL_KW, True, 0.0, kwa_ref), (COL_VW, False, ones, vwa_ref)):
        xs = slab(c0, rotary)
        dst[0, 0] = jnp.where(low, xs, extra).astype(BF16)
        dst[0, 1] = jnp.where(low, pltpu.roll(xs, HEAD_DIM, 1), extra).astype(BF16)


def _even_proj(x, g, w, cos, sin, blk, *, B, S, tm=512):
    T = x.shape[0]
    spt = S // tm
    tok = lambda width: pl.BlockSpec((tm, width), lambda i: (i, 0))
    pos = pl.BlockSpec((tm, LANES), lambda i: (i % spt, 0))
    grp = lambda width: pl.BlockSpec((1, NSA_KV, tm, width), lambda i: (i // spt, 0, i % spt, 0))
    aug = jax.ShapeDtypeStruct((B, NSA_KV, S, LANES), BF16)
    raw = jax.ShapeDtypeStruct((B, NSA_KV, S, HEAD_DIM), F32)
    return pl.pallas_call(
        _even_proj_kernel,
        out_shape=(jax.ShapeDtypeStruct((T, 2 * HALF_W), F32), jax.ShapeDtypeStruct((T, HALF_W), F32),
                   jax.ShapeDtypeStruct((T, LANES), F32), raw, raw, aug, aug, aug, aug),
        grid=(T // tm,),
        in_specs=[tok(D_MODEL), _resident((1, D_MODEL)), _resident((D_MODEL, EVEN_IN_PAD)), pos, pos, pos],
        out_specs=(tok(2 * HALF_W), tok(HALF_W), tok(LANES), grp(HEAD_DIM), grp(HEAD_DIM),
                   grp(LANES), grp(LANES), grp(LANES), grp(LANES)),
        compiler_params=_cparams(1),
        name="even_proj",
    )(x, g, w, cos, sin, blk)


def _gmlp_kernel(u_ref, v_ref, ws_ref, avg_ref, bias_ref, o_ref, *, chunks):
    rows_all = N_HEADS * GM_CHUNK
    r = lax.broadcasted_iota(jnp.int32, (rows_all, GM_CHUNK), 0) & (GM_CHUNK - 1)
    c = lax.broadcasted_iota(jnp.int32, (rows_all, GM_CHUNK), 1)
    w_all = jnp.where(c <= r, ws_ref[...], 0.0).astype(BF16)
    avg = avg_ref[...]

    def head_mean(x):
        hi, lo = _split_bf16(x)
        return _dot(hi, avg) + _dot(lo, avg)

    v = jax.nn.gelu(v_ref[...])
    d = v - head_mean(v)
    vn = (d * lax.rsqrt(head_mean(d * d) + LN_EPS)).astype(BF16)
    lane_head = lax.broadcasted_iota(jnp.int32, (GM_CHUNK, HALF_W), 1) // HEAD_DIM
    for ci in range(chunks):
        rows = slice(ci * GM_CHUNK, (ci + 1) * GM_CHUNK)
        res = _dot(w_all, vn[rows, :])
        s = res[0:GM_CHUNK, :]
        for h in range(1, N_HEADS):
            s = jnp.where(lane_head == h, res[h * GM_CHUNK:(h + 1) * GM_CHUNK, :], s)
        o_ref[rows, :] = jax.nn.gelu(u_ref[rows, :]) * (s + bias_ref[...])


def _gmlp(uv, ws, avg, bias, *, chunks=4):
    T = uv.shape[0]
    tm = chunks * GM_CHUNK
    return pl.pallas_call(
        functools.partial(_gmlp_kernel, chunks=chunks),
        out_shape=jax.ShapeDtypeStruct((T, HALF_W), F32),
        grid=(T // tm,),
        in_specs=[pl.BlockSpec((tm, HALF_W), lambda i: (i, 0)),
                  pl.BlockSpec((tm, HALF_W), lambda i: (i, 1)),
                  _resident((N_HEADS * GM_CHUNK, GM_CHUNK)),
                  _resident((HALF_W, HALF_W)),
                  _resident((GM_CHUNK, HALF_W))],
        out_specs=pl.BlockSpec((tm, HALF_W), lambda i: (i, 0)),
        compiler_params=_cparams(1),
        name="gmlp",
    )(uv, uv, ws, avg, bias)


def _cmp_kernel(k2_ref, v2_ref, pe_ref, w1_ref, w2_ref, cos_ref, sin_ref, ko_ref, vo_ref):
    half_flat = CMP_STRIDE * HEAD_DIM
    for which, (src, dst) in enumerate(((k2_ref, ko_ref), (v2_ref, vo_ref))):
        pe = pe_ref[which]
        w1 = w1_ref[which]
        w2 = w2_ref[which].astype(BF16)
        for g in range(NSA_KV):
            a = src[0, g]
            a_next = pltpu.roll(a, a.shape[0] - 1, 0)
            pre = _dot((a + pe[:, :half_flat]).astype(BF16), w1[:half_flat, :].astype(BF16))
            pre = pre + _dot((a_next + pe[:, half_flat:]).astype(BF16), w1[half_flat:, :].astype(BF16))
            cmp = _dot(jax.nn.gelu(pre).astype(BF16), w2)
            if which == 0:
                rot = jnp.concatenate([-cmp[:, HALF:], cmp[:, :HALF]], axis=1)
                cmp = cmp * cos_ref[...] + rot * sin_ref[...]
            dst[0, g] = jnp.concatenate([cmp, jnp.zeros_like(cmp)], axis=1).astype(BF16)


def _compress(k2, v2, pe, w1, w2, cosc, sinc):
    B, _, nblk, flat = k2.shape
    out = jax.ShapeDtypeStruct((B, NSA_KV, nblk, LANES), BF16)
    out_spec = pl.BlockSpec((1, NSA_KV, nblk, LANES), lambda b: (b, 0, 0, 0))
    return pl.pallas_call(
        _cmp_kernel,
        out_shape=(out, out),
        grid=(B,),
        in_specs=[pl.BlockSpec((1, NSA_KV, nblk, flat), lambda b: (b, 0, 0, 0)),
                  pl.BlockSpec((1, NSA_KV, nblk, flat), lambda b: (b, 0, 0, 0)),
                  _resident((2, 1, 2 * flat)),
                  _resident((2, 2 * flat, HEAD_DIM)),
                  _resident((2, HEAD_DIM, HEAD_DIM)),
                  _resident((nblk, HEAD_DIM)),
                  _resident((nblk, HEAD_DIM))],
        out_specs=(out_spec, out_spec),
        compiler_params=_cparams(1),
        name="nsa_compress",
    )(k2, v2, pe, w1, w2, cosc, sinc)


def _softmax_pv(pieces):
    m = None
    for s, _ in pieces:
        pm = jnp.max(s, axis=-1, keepdims=True)
        m = pm if m is None else jnp.maximum(m, pm)
    acc = None
    for s, vv in pieces:
        part = _dot(jnp.exp2(s - m).astype(BF16), vv)
        acc = part if acc is None else acc + part
    return acc / acc[:, ONES_LANE:ONES_LANE + 1]


def _nsa_kernel(q_ref, gl_ref, ksa_ref, vsa_ref, kwa_ref, vwa_ref, kc_ref, vc_ref, ovt_ref, o_ref,
                osel_ref, owin_ref, *, S):
    TQ = Q_BLOCK
    R = NSA_HPG * TQ
    ncmp = S // CMP_STRIDE
    nsel = S // SEL_BLOCK
    i = pl.program_id(1)
    t0 = i * TQ

    q = q_ref[...] * (HEAD_DIM ** -0.5 * LOG2E)
    gate = jax.nn.sigmoid(gl_ref[...])
    lane = lax.broadcasted_iota(jnp.int32, (TQ, LANES), 1)
    lane4 = lax.broadcasted_iota(jnp.int32, (R, LANES), 1)
    trow4 = t0 + (lax.broadcasted_iota(jnp.int32, (R, 1), 0) & (TQ - 1))

    ncol = lax.broadcasted_iota(jnp.int32, (R, ncmp), 1)
    valid_c = (ncol * CMP_STRIDE + (CMP_LEN - 1) <= trow4) & (ncol < ncmp - 1)
    any_c = (trow4 >= CMP_LEN - 1).astype(F32)

    jrow = lax.broadcasted_iota(jnp.int32, (nsel, TQ), 0)
    tl = t0 + lax.broadcasted_iota(jnp.int32, (nsel, TQ), 1)
    cur = jnp.right_shift(tl, 6)
    forced = (jrow == 0) | (jrow == cur) | (jrow == cur - 1)
    causal_blk = jrow * SEL_BLOCK <= tl
    ovt = ovt_ref[...]

    q4a, o_cmp = [], []
    for g in range(NSA_KV):
        heads = []
        for hq in range(NSA_HPG):
            h = g * NSA_HPG + hq
            slab = q[:, (h // 2) * LANES:(h // 2 + 1) * LANES]
            if h % 2 == 1:
                slab = pltpu.roll(slab, HEAD_DIM, 1)
            heads.append(jnp.where(lane < HEAD_DIM, slab, 0.0))
        q4 = jnp.concatenate(heads, axis=0)

        s_c = jnp.where(valid_c, _dot_nt(q4.astype(BF16), kc_ref[0, g]), NEG_INF)
        e_c = jnp.exp2(s_c - jnp.max(s_c, axis=-1, keepdims=True))
        p_c = e_c / jnp.sum(e_c, axis=-1, keepdims=True) * any_c
        o_cmp.append(_dot(p_c.astype(BF16), vc_ref[0, g]))
        p_sum = (p_c[0:TQ] + p_c[TQ:2 * TQ]) + (p_c[2 * TQ:3 * TQ] + p_c[3 * TQ:4 * TQ])

        p_hi, p_lo = _split_bf16(p_sum)
        imp_t = (_dot_nt(ovt, p_hi) + _dot_nt(ovt, p_lo))[0:nsel, :]
        imp_t = jnp.where(forced, SEL_FORCE, jnp.where(causal_blk, imp_t, -SEL_FORCE))
        rank = jnp.zeros((nsel, TQ), jnp.int32)
        for jp in range(nsel):
            row = imp_t[jp:jp + 1, :]
            beats = (row > imp_t) | ((row == imp_t) & (jrow > jp))
            rank = rank + beats.astype(jnp.int32)
        sel_t = (rank < SEL_TOPN).astype(F32)
        if nsel < LANES:
            sel_t = jnp.concatenate([sel_t, jnp.zeros((LANES - nsel, TQ), F32)], axis=0)
        bias = (pltpu.roll(sel_t.T, HEAD_DIM, 1) - 1.0) * (-NEG_INF)
        q4a.append(jnp.where(lane4 < HEAD_DIM, q4, jnp.concatenate([bias] * NSA_HPG, axis=0)).astype(BF16))

    r4 = lax.broadcasted_iota(jnp.int32, (R, 1), 0) & (TQ - 1)
    col_step = lax.broadcasted_iota(jnp.int32, (R, SEL_STEP), 1)
    for n in range(S // SEL_STEP):
        @pl.when(i // (SEL_STEP // TQ) == n)
        def _(n=n):
            head = n * SEL_STEP
            for g in range(NSA_KV):
                s = _dot_nt(q4a[g], ksa_ref[0, g, 0:head + SEL_STEP, :])
                tail = jnp.where(head + col_step <= trow4, s[:, head:], NEG_INF)
                pieces = [(tail, vsa_ref[0, g, head:head + SEL_STEP, :])]
                if n > 0:
                    pieces.append((s[:, :head], vsa_ref[0, g, 0:head, :]))
                osel_ref[g] = _softmax_pv(pieces)

    n_back = WINDOW // TQ
    col = lax.broadcasted_iota(jnp.int32, (R, TQ), 1)

    @pl.when(i >= n_back)
    def _():
        k0 = pl.multiple_of((i - n_back) * TQ, TQ)
        for g in range(NSA_KV):
            s = _dot_nt(q4a[g], kwa_ref[0, g, pl.ds(k0, WINDOW + TQ), :])
            vv = vwa_ref[0, g, pl.ds(k0, WINDOW + TQ), :]
            owin_ref[g] = _softmax_pv([(jnp.where(col > r4, s[:, :TQ], NEG_INF), vv[:TQ]),
                                       (s[:, TQ:WINDOW], vv[TQ:WINDOW]),
                                       (jnp.where(col <= r4, s[:, WINDOW:], NEG_INF), vv[WINDOW:])])

    @pl.when(i < n_back)
    def _():
        colw = lax.broadcasted_iota(jnp.int32, (R, WINDOW), 1)
        for g in range(NSA_KV):
            s = _dot_nt(q4a[g], kwa_ref[0, g, 0:WINDOW, :])
            owin_ref[g] = _softmax_pv([(jnp.where(colw <= trow4, s, NEG_INF), vwa_ref[0, g, 0:WINDOW, :])])

    for g in range(NSA_KV):
        o_s = osel_ref[g]
        o_w = owin_ref[g]
        outs = []
        for hq in range(NSA_HPG):
            h = g * NSA_HPG + hq
            rows = slice(hq * TQ, (hq + 1) * TQ)
            outs.append(gate[:, 3 * h:3 * h + 1] * o_cmp[g][rows] + gate[:, 3 * h + 1:3 * h + 2] * o_s[rows]
                        + gate[:, 3 * h + 2:3 * h + 3] * o_w[rows])
        for pair in range(NSA_HPG // 2):
            c0 = (g * NSA_HPG + 2 * pair) * HEAD_DIM
            o_ref[:, c0:c0 + LANES] = jnp.where(lane < HEAD_DIM, outs[2 * pair],
                                                pltpu.roll(outs[2 * pair + 1], HEAD_DIM, 1))


def _nsa(q, gl, ksa, vsa, kwa, vwa, kcmp, vcmp, ovt, *, B, S):
    T = q.shape[0]
    nq = S // Q_BLOCK
    ncmp = S // CMP_STRIDE
    per_batch = lambda rows: pl.BlockSpec((1, NSA_KV, rows, LANES), lambda b, i: (b, 0, 0, 0))
    return pl.pallas_call(
        functools.partial(_nsa_kernel, S=S),
        out_shape=jax.ShapeDtypeStruct((T, HALF_W), F32),
        grid=(B, nq),
        in_specs=[pl.BlockSpec((Q_BLOCK, HALF_W), lambda b, i: (b * nq + i, 0)),
                  pl.BlockSpec((Q_BLOCK, LANES), lambda b, i: (b * nq + i, 0)),
                  per_batch(S), per_batch(S), per_batch(S), per_batch(S),
                  per_batch(ncmp), per_batch(ncmp),
                  _resident((LANES, ncmp))],
        out_specs=pl.BlockSpec((Q_BLOCK, HALF_W), lambda b, i: (b * nq + i, 0)),
        scratch_shapes=[pltpu.VMEM((NSA_KV, NSA_HPG * Q_BLOCK, LANES), F32),
                        pltpu.VMEM((NSA_KV, NSA_HPG * Q_BLOCK, LANES), F32)],
        compiler_params=_cparams(2),
        name="nsa_attn",
    )(q, gl, ksa, vsa, kwa, vwa, kcmp, vcmp, ovt)


def _pool_kernel(z_ref, pw_ref, ps_ref, o_ref):
    S = z_ref.shape[0]
    row = lax.broadcasted_iota(jnp.int32, (S, POOL_GROUP), 0)
    for gi, w in enumerate(POOL_WINDOWS):
        cols = slice(gi * POOL_GROUP, (gi + 1) * POOL_GROUP)
        zg = z_ref[:, cols]
        acc = zg
        k = 1
        while k < w:
            acc = acc + jnp.where(row >= k, pltpu.roll(acc, k, 0), 0.0)
            k *= 2
        cnt = jnp.minimum(row + 1, w).astype(F32)
        pooled = acc / cnt - zg
        y = _dot(pooled.astype(BF16), pw_ref[gi].astype(BF16))
        o_ref[:, cols] = y * ps_ref[:, cols]


def _pool(z, pw, ps, *, B, S):
    T = z.shape[0]
    return pl.pallas_call(
        _pool_kernel,
        out_shape=jax.ShapeDtypeStruct((T, HALF_W), F32),
        grid=(B,),
        in_specs=[pl.BlockSpec((S, HALF_W), lambda b: (b, 0)),
                  _resident((len(POOL_WINDOWS), POOL_GROUP, POOL_GROUP)),
                  _resident((1, HALF_W))],
        out_specs=pl.BlockSpec((S, HALF_W), lambda b: (b, 0)),
        compiler_params=_cparams(1),
        name="pool",
    )(z, pw, ps)


def _s5_disc_kernel(lr_ref, li_ref, ldt_ref, lbr_ref, lbi_ref, cr_ref, ci_ref):
    lr = lr_ref[...]
    li = li_ref[...]
    dt = jnp.exp(ldt_ref[...])
    mag = jnp.exp(lr * dt)
    ang = li * dt
    lbr = mag * jnp.cos(ang)
    lbi = mag * jnp.sin(ang)
    nr = lbr - 1.0
    den = lr * lr + li * li
    lbr_ref[...] = lbr
    lbi_ref[...] = lbi
    cr_ref[...] = (nr * lr + lbi * li) / den
    ci_ref[...] = (lbi * lr - nr * li) / den


def _s5_bbar_kernel(cr_ref, ci_ref, br_ref, bi_ref, or_ref, oi_ref):
    cr = cr_ref[...]
    ci = ci_ref[...]
    br = br_ref[...]
    bi = bi_ref[...]
    or_ref[...] = cr * br - ci * bi
    oi_ref[...] = cr * bi + ci * br


def _s5_params(lam_re, lam_im, log_dt, b_re, b_im):
    gp = jax.ShapeDtypeStruct((S5_GROUPS, S5_STATE), F32)
    lbr, lbi, cr, ci = pl.pallas_call(_s5_disc_kernel, out_shape=(gp, gp, gp, gp), name="s5_disc")(
        lam_re, lam_im, log_dt.reshape(S5_GROUPS, 1))
    flat = jax.ShapeDtypeStruct((S5_MODES, S5_GROUP_CH), F32)
    bbr, bbi = pl.pallas_call(_s5_bbar_kernel, out_shape=(flat, flat), name="s5_bbar")(
        cr.reshape(S5_MODES, 1), ci.reshape(S5_MODES, 1),
        b_re.reshape(S5_MODES, S5_GROUP_CH), b_im.reshape(S5_MODES, S5_GROUP_CH))
    return lbr, lbi, bbr, bbi


def _s5_kernel(u_ref, bsb_ref, lr_ref, li_ref, cre_ref, cim_ref, d_ref, wglu_ref, o_ref,
               xs_ref, st_ref, *, steps, nb):
    M = S5_MODES
    W = S5_SUPER_MODES

    @pl.when(pl.program_id(0) == 0)
    def _():
        st_ref[...] = jnp.zeros_like(st_ref)

    u = u_ref[...]
    ub = u.astype(BF16)
    for sb in range(S5_SUPER):
        bu = _dot(ub[:, sb * S5_SUPER_CH:(sb + 1) * S5_SUPER_CH], bsb_ref[sb])
        xs_ref[:, sb * W:(sb + 1) * W] = bu[:, :W]
        xs_ref[:, M + sb * W:M + (sb + 1) * W] = bu[:, W:]

    def step(t, carry):
        xr, xi = carry
        r0 = pl.multiple_of(t * nb, nb)
        lr = lr_ref[...]
        li = li_ref[...]
        nr = lr * xr - li * xi + xs_ref[pl.ds(r0, nb), 0:M]
        ni = lr * xi + li * xr + xs_ref[pl.ds(r0, nb), M:2 * M]
        xs_ref[pl.ds(r0, nb), 0:M] = nr
        xs_ref[pl.ds(r0, nb), M:2 * M] = ni
        return nr, ni

    xr, xi = lax.fori_loop(0, steps, step, (st_ref[:, 0:M], st_ref[:, M:2 * M]))
    st_ref[:, 0:M] = xr
    st_ref[:, M:2 * M] = xi

    ys = []
    for sb in range(S5_SUPER):
        ys.append(_dot(xs_ref[:, sb * W:(sb + 1) * W].astype(BF16), cre_ref[sb])
                  - _dot(xs_ref[:, M + sb * W:M + (sb + 1) * W].astype(BF16), cim_ref[sb]))
    y = jax.nn.gelu(jnp.concatenate(ys, axis=1) + d_ref[...] * u)
    ab = _dot(y.astype(BF16), wglu_ref[...])
    o_ref[...] = ab[:, :S5_WIDTH] * jax.nn.sigmoid(ab[:, S5_WIDTH:])


def _s5(u_tb, bsb, lr_b, li_b, cre, cim, d, wglu, *, nb, steps=64):
    R = u_tb.shape[0]
    rows = steps * nb
    return pl.pallas_call(
        functools.partial(_s5_kernel, steps=steps, nb=nb),
        out_shape=jax.ShapeDtypeStruct((R, S5_WIDTH), F32),
        grid=(R // rows,),
        in_specs=[pl.BlockSpec((rows, S5_WIDTH), lambda i: (i, 0)),
                  _resident((S5_SUPER, S5_SUPER_CH, 2 * S5_SUPER_MODES)),
                  _resident((nb, S5_MODES)),
                  _resident((nb, S5_MODES)),
                  _resident((S5_SUPER, S5_SUPER_MODES, S5_SUPER_CH)),
                  _resident((S5_SUPER, S5_SUPER_MODES, S5_SUPER_CH)),
                  _resident((1, S5_WIDTH)),
                  _resident((S5_WIDTH, 2 * S5_WIDTH))],
        out_specs=pl.BlockSpec((rows, S5_WIDTH), lambda i: (i, 0)),
        scratch_shapes=[pltpu.VMEM((rows, 2 * S5_MODES), F32),
                        pltpu.VMEM((nb, 2 * S5_MODES), F32)],
        compiler_params=_cparams(1),
        name="s5",
    )(u_tb, bsb, lr_b, li_b, cre, cim, d, wglu)


def _rope_tables(S):
    inv = ROPE_THETA ** (-jnp.arange(HALF, dtype=F32) / HALF)
    ang = jnp.arange(S, dtype=F32)[:, None] * inv[None, :]
    cos, sin = jnp.cos(ang), jnp.sin(ang)
    cos_t = jnp.tile(cos, (1, LANES // HALF))
    sin_t = jnp.tile(jnp.concatenate([-sin, sin], axis=1), (1, LANES // HEAD_DIM))
    end_pos = (jnp.arange(S // CMP_STRIDE, dtype=F32) * CMP_STRIDE + (CMP_LEN - 1))[:, None] * inv[None, :]
    cosc = jnp.tile(jnp.cos(end_pos), (1, 2))
    sinc = jnp.tile(jnp.sin(end_pos), (1, 2))
    return cos_t, sin_t, cosc, sinc


def _structure_constants(S):
    ncmp = S // CMP_STRIDE
    nsel = S // SEL_BLOCK
    assert nsel <= LANES - HEAD_DIM
    j = jnp.arange(LANES, dtype=jnp.int32)[:, None]
    n = jnp.arange(ncmp, dtype=jnp.int32)[None, :]
    ovt = ((n * CMP_STRIDE < (j + 1) * SEL_BLOCK) & (n * CMP_STRIDE + CMP_LEN > j * SEL_BLOCK)
           & (j < nsel) & (n < ncmp - 1)).astype(BF16)
    key_blk = (jnp.arange(S, dtype=jnp.int32) // SEL_BLOCK)[:, None]
    blk_tab = (key_blk + HEAD_DIM == jnp.arange(LANES, dtype=jnp.int32)[None, :]).astype(F32)
    c = jnp.arange(HALF_W, dtype=jnp.int32)
    avg = ((c[:, None] // HEAD_DIM == c[None, :] // HEAD_DIM).astype(F32) / HEAD_DIM).astype(BF16)
    return ovt, blk_tab, avg


def _even_mixer(x, g, w_in, gm_w_s, gm_b, pe, w1, w2, consts, *, B, S):
    cos_t, sin_t, cosc, sinc, ovt, blk_tab, avg = consts
    w_pad = jnp.pad(w_in, ((0, 0), (0, EVEN_IN_PAD - EVEN_IN))).astype(BF16)
    uv, q, gl, kc, vc, ksa, vsa, kwa, vwa = _even_proj(x, g, w_pad, cos_t, sin_t, blk_tab, B=B, S=S)
    out_a = _gmlp(uv, gm_w_s.reshape(N_HEADS * GM_CHUNK, GM_CHUNK), avg,
                  jnp.repeat(gm_b.T, HEAD_DIM, axis=1))
    blocks = (B, NSA_KV, S // CMP_STRIDE, CMP_STRIDE * HEAD_DIM)
    kcmp, vcmp = _compress(kc.reshape(blocks), vc.reshape(blocks),
                           pe.reshape(2, 1, CMP_LEN * HEAD_DIM), w1, w2, cosc, sinc)
    out_b = _nsa(q, gl, ksa, vsa, kwa, vwa, kcmp, vcmp, ovt, B=B, S=S)
    return out_a, out_b


def _odd_mixer(x, g, w_in, pool_w, pool_scale, lam_re, lam_im, b_re, b_im, c_re, c_im,
               d_skip, log_dt, w_glu, *, B, S):
    zc, u_sb = _odd_proj(x, g, w_in.astype(BF16), B=B, S=S)
    y_c = _pool(zc, pool_w, pool_scale.reshape(1, HALF_W), B=B, S=S)

    lbr, lbi, bbr, bbi = _s5_params(lam_re, lam_im, log_dt, b_re, b_im)
    gps = S5_GROUPS // S5_SUPER
    eye = jnp.eye(gps, dtype=F32)
    blockdiag_in = lambda bb: jnp.einsum(
        'sgpc,gh->sgchp', bb.reshape(S5_SUPER, gps, S5_STATE, S5_GROUP_CH), eye
    ).reshape(S5_SUPER, S5_SUPER_CH, S5_SUPER_MODES)
    blockdiag_out = lambda cc: jnp.einsum(
        'sgcp,gh->sgphc', cc.reshape(S5_SUPER, gps, S5_GROUP_CH, S5_STATE), eye
    ).reshape(S5_SUPER, S5_SUPER_MODES, S5_SUPER_CH)
    bsb = jnp.concatenate([blockdiag_in(bbr), blockdiag_in(bbi)], axis=2).astype(BF16)
    bcast = lambda a: jnp.broadcast_to(a.reshape(1, S5_MODES), (B, S5_MODES))
    y_tb = _s5(u_sb.reshape(S * B, S5_WIDTH), bsb, bcast(lbr), bcast(lbi),
               blockdiag_out(c_re).astype(BF16), blockdiag_out(c_im).astype(BF16),
               d_skip.reshape(1, S5_WIDTH), w_glu.astype(BF16), nb=B)
    return y_c, y_tb.reshape(S, B * S5_WIDTH)


def kernel(x, norm_w, ffn_w_gate, ffn_w_up, ffn_w_down, final_norm_w, ev_w_in, ev_w_out, gm_w_s, gm_b,
           nsa_cmp_pe, nsa_cmp_w1, nsa_cmp_w2, od_w_in, od_w_out, pool_w, pool_scale, s5_lam_re,
           s5_lam_im, s5_b_re, s5_b_im, s5_c_re, s5_c_im, s5_d, s5_log_dt, s5_w_glu):
    B, S, _ = x.shape
    consts = _rope_tables(S) + _structure_constants(S)
    tm = 512
    spt = S // tm
    batch_major = pl.BlockSpec((tm, HALF_W), lambda i: (i, 0))
    time_major = pl.BlockSpec((tm, HALF_W), lambda i: (i % spt, i // spt))
    ffn_w = lambda l, k: (norm_w[l, 2 * k].reshape(1, D_MODEL), ffn_w_gate[l, k].astype(BF16),
                          ffn_w_up[l, k].astype(BF16), ffn_w_down[l, k].astype(BF16))
    gf = final_norm_w.reshape(1, D_MODEL)
    xt = x.reshape(B * S, D_MODEL)
    for l in range(DEPTH):
        xt = _ffn(xt, *ffn_w(l, 0), tm=tm)
        g = norm_w[l, 1].reshape(1, D_MODEL)
        i = l // 2
        if l % 2 == 0:
            a, b = _even_mixer(xt, g, ev_w_in[i], gm_w_s[i], gm_b[i], nsa_cmp_pe[i], nsa_cmp_w1[i],
                               nsa_cmp_w2[i], consts, B=B, S=S)
            w_out, b_spec = ev_w_out[i], batch_major
        else:
            a, b = _odd_mixer(xt, g, od_w_in[i], pool_w[i], pool_scale[i], s5_lam_re[i], s5_lam_im[i],
                              s5_b_re[i], s5_b_im[i], s5_c_re[i], s5_c_im[i], s5_d[i], s5_log_dt[i],
                              s5_w_glu[i], B=B, S=S)
            w_out, b_spec = od_w_out[i], time_major
        xt = _mix_ffn(a, b, b_spec, w_out.astype(BF16), xt, *ffn_w(l, 1), gf, final=(l == DEPTH - 1), tm=tm)
    return xt.reshape(B, S, D_MODEL)
```

```python
import functools

import jax
import jax.numpy as jnp
from jax import lax
from jax.experimental import pallas as pl
from jax.experimental.pallas import tpu as pltpu

F32 = jnp.float32
BF16 = jnp.bfloat16

D_MODEL = 1024
DEPTH = 4
HEAD_DIM = 64
HALF = HEAD_DIM // 2
ROPE_THETA = 10000.0
N_HEADS = 8
GM_CHUNK = 128
NSA_KV = 2
NSA_HPG = 4
CMP_LEN = 32
CMP_STRIDE = 16
SEL_BLOCK = 64
SEL_TOPN = 8
WINDOW = 512
Q_BLOCK = 128
POOL_WINDOWS = (2, 4, 8, 16)
POOL_GROUP = 128
S5_GROUPS = 32
S5_GROUP_CH = 16
S5_STATE = 64
S5_WIDTH = 512
S5_MODES = S5_GROUPS * S5_STATE
S5_SUPER = 4
S5_SUPER_CH = S5_WIDTH // S5_SUPER
S5_SUPER_MODES = S5_MODES // S5_SUPER
FFN_DIM = 2816
RMS_EPS = 1e-6
LN_EPS = 1e-5
NEG_INF = -1e30
SEL_FORCE = 1e4
LOG2E = 1.4426950408889634

LANES = 128
HALF_W = 512
EVEN_IN = 2328
EVEN_IN_PAD = 2432
COL_U, COL_V, COL_Q = 0, 512, 1024
COL_KC, COL_VC, COL_KS, COL_VS, COL_KW, COL_VW, COL_GL = 1536, 1664, 1792, 1920, 2048, 2176, 2304
SEL_STEP = 512
ONES_LANE = HEAD_DIM

VMEM_LIMIT = 56 * 1024 * 1024


def _cparams(n_axes):
    return pltpu.CompilerParams(dimension_semantics=("arbitrary",) * n_axes,
                                vmem_limit_bytes=VMEM_LIMIT)


def _resident(shape):
    nd = len(shape)
    return pl.BlockSpec(shape, lambda *_: (0,) * nd, pipeline_mode=pl.Buffered(1))


def _rms(x, g):
    ms = jnp.mean(x * x, axis=-1, keepdims=True)
    return x * lax.rsqrt(ms + RMS_EPS) * g


def _dot(a, b):
    return jnp.dot(a, b, preferred_element_type=F32)


def _dot_nt(a, b):
    return lax.dot_general(a, b, (((1,), (1,)), ((), ())), preferred_element_type=F32)


def _split_bf16(x):
    hi = x.astype(BF16)
    return hi, (x - hi.astype(F32)).astype(BF16)


def _swiglu_half_step(x, g, wg_ref, wu_ref, wd_ref, fc):
    h = _rms(x, g).astype(BF16)
    acc = jnp.zeros_like(x)
    for c in range(FFN_DIM // fc):
        sl = slice(c * fc, (c + 1) * fc)
        a = _dot(h, wg_ref[:, sl])
        b = _dot(h, wu_ref[:, sl])
        t = (a * jax.nn.sigmoid(a) * b).astype(BF16)
        acc = acc + _dot(t, wd_ref[sl, :])
    return x + 0.5 * acc


def _ffn_kernel(x_ref, g_ref, wg_ref, wu_ref, wd_ref, o_ref, *, fc):
    o_ref[...] = _swiglu_half_step(x_ref[...], g_ref[...], wg_ref, wu_ref, wd_ref, fc)


def _mix_ffn_kernel(a_ref, b_ref, wo_ref, x_ref, g_ref, wg_ref, wu_ref, wd_ref, gf_ref, o_ref, *, fc, final):
    x = x_ref[...] + _dot(a_ref[...].astype(BF16), wo_ref[0:HALF_W, :])
    x = x + _dot(b_ref[...].astype(BF16), wo_ref[HALF_W:2 * HALF_W, :])
    x = _swiglu_half_step(x, g_ref[...], wg_ref, wu_ref, wd_ref, fc)
    o_ref[...] = _rms(x, gf_ref[...]) if final else x


def _ffn_weight_specs(l, k):
    pick = lambda rows, cols: pl.BlockSpec((None, None, rows, cols), lambda *_: (l, k, 0, 0),
                                           pipeline_mode=pl.Buffered(1))
    return [_resident((1, D_MODEL)), pick(D_MODEL, FFN_DIM), pick(D_MODEL, FFN_DIM), pick(FFN_DIM, D_MODEL)]


def _ffn(x, g, wg, wu, wd, l, k, *, tm=512, fc=256):
    T = x.shape[0]
    return pl.pallas_call(
        functools.partial(_ffn_kernel, fc=fc),
        out_shape=jax.ShapeDtypeStruct((T, D_MODEL), F32),
        grid=(T // tm,),
        in_specs=[pl.BlockSpec((tm, D_MODEL), lambda i: (i, 0))] + _ffn_weight_specs(l, k),
        out_specs=pl.BlockSpec((tm, D_MODEL), lambda i: (i, 0)),
        compiler_params=_cparams(1),
        name="ffn",
    )(x, g, wg, wu, wd)


def _mix_ffn(a, b, b_spec, wo, x, g, wg, wu, wd, l, k, gf, *, final, tm=512, fc=256):
    T = x.shape[0]
    return pl.pallas_call(
        functools.partial(_mix_ffn_kernel, fc=fc, final=final),
        out_shape=jax.ShapeDtypeStruct((T, D_MODEL), F32),
        grid=(T // tm,),
        in_specs=[pl.BlockSpec((tm, HALF_W), lambda i: (i, 0)), b_spec,
                  _resident((D_MODEL, D_MODEL)),
                  pl.BlockSpec((tm, D_MODEL), lambda i: (i, 0))] + _ffn_weight_specs(l, k)
                 + [_resident((1, D_MODEL))],
        out_specs=pl.BlockSpec((tm, D_MODEL), lambda i: (i, 0)),
        compiler_params=_cparams(1),
        name="mix_ffn",
    )(a, b, wo, x, g, wg, wu, wd, gf)


def _odd_proj_kernel(x_ref, g_ref, w_ref, zc_ref, u_ref):
    h = _rms(x_ref[...], g_ref[...]).astype(BF16)
    z = _dot(h, w_ref[...])
    zc_ref[...] = z[:, :HALF_W]
    u_ref[...] = z[:, HALF_W:]


def _odd_proj(x, g, w, *, B, S, tm=512):
    T = x.shape[0]
    spt = S // tm
    return pl.pallas_call(
        _odd_proj_kernel,
        out_shape=(jax.ShapeDtypeStruct((T, HALF_W), F32), jax.ShapeDtypeStruct((S, B * HALF_W), F32)),
        grid=(T // tm,),
        in_specs=[pl.BlockSpec((tm, D_MODEL), lambda i: (i, 0)),
                  _resident((1, D_MODEL)),
                  _resident((D_MODEL, 2 * HALF_W))],
        out_specs=(pl.BlockSpec((tm, HALF_W), lambda i: (i, 0)),
                   pl.BlockSpec((tm, HALF_W), lambda i: (i % spt, i // spt))),
        compiler_params=_cparams(1),
        name="odd_proj",
    )(x, g, w)


def _even_proj_kernel(x_ref, g_ref, w_ref, cos_ref, sin_ref, blk_ref,
                      uv_ref, q_ref, gl_ref, kc_ref, vc_ref, ksa_ref, vsa_ref, kwa_ref, vwa_ref):
    h = _rms(x_ref[...], g_ref[...]).astype(BF16)
    z = _dot(h, w_ref[...])
    cos = cos_ref[...]
    sin = sin_ref[...]
    lane = lax.broadcasted_iota(jnp.int32, cos.shape, 1)
    first_half = (lane % HEAD_DIM) < HALF
    low = lane < HEAD_DIM

    def slab(c0, rotary):
        xs = z[:, c0:c0 + LANES]
        if not rotary:
            return xs
        rot = jnp.where(first_half, pltpu.roll(xs, LANES - HALF, 1), pltpu.roll(xs, HALF, 1))
        return xs * cos + rot * sin

    uv_ref[...] = z[:, COL_U:COL_Q]
    for c in range(HALF_W // LANES):
        q_ref[:, c * LANES:(c + 1) * LANES] = slab(COL_Q + c * LANES, True)
    gl_ref[...] = z[:, COL_GL:COL_GL + LANES]
    kc_ref[...] = slab(COL_KC, False)
    vc_ref[...] = slab(COL_VC, False)
    ones = (lane == ONES_LANE).astype(F32)
    for c0, rotary, extra, dst in ((COL_KS, True, blk_ref[...], ksa_ref), (COL_VS, False, ones, vsa_ref),
                                   (COL_KW, True, 0.0, kwa_ref), (COL_VW, False, ones, vwa_ref)):
        xs = slab(c0, rotary)
        dst[0, 0] = jnp.where(low, xs, extra).astype(BF16)
        dst[0, 1] = jnp.where(low, pltpu.roll(xs, HEAD_DIM, 1), extra).astype(BF16)


def _even_proj(x, g, w, cos, sin, blk, *, B, S, tm=512):
    T = x.shape[0]
    spt = S // tm
    tok = lambda width: pl.BlockSpec((tm, width), lambda i: (i, 0))
    pos = pl.BlockSpec((tm, LANES), lambda i: (i % spt, 0))
    grp = lambda width: pl.BlockSpec((1, NSA_KV, tm, width), lambda i: (i // spt, 0, i % spt, 0))
    aug = jax.ShapeDtypeStruct((B, NSA_KV, S, LANES), BF16)
    raw = jax.ShapeDtypeStruct((T, LANES), F32)
    return pl.pallas_call(
        _even_proj_kernel,
        out_shape=(jax.ShapeDtypeStruct((T, 2 * HALF_W), F32), jax.ShapeDtypeStruct((T, HALF_W), F32),
                   jax.ShapeDtypeStruct((T, LANES), F32), raw, raw, aug, aug, aug, aug),
        grid=(T // tm,),
        in_specs=[tok(D_MODEL), _resident((1, D_MODEL)), _resident((D_MODEL, EVEN_IN_PAD)), pos, pos, pos],
        out_specs=(tok(2 * HALF_W), tok(HALF_W), tok(LANES), tok(LANES), tok(LANES),
                   grp(LANES), grp(LANES), grp(LANES), grp(LANES)),
        compiler_params=_cparams(1),
        name="even_proj",
    )(x, g, w, cos, sin, blk)


def _gmlp_kernel(u_ref, v_ref, ws_ref, avg_ref, bias_ref, o_ref, *, chunks):
    rows_all = N_HEADS * GM_CHUNK
    r = lax.broadcasted_iota(jnp.int32, (rows_all, GM_CHUNK), 0) & (GM_CHUNK - 1)
    c = lax.broadcasted_iota(jnp.int32, (rows_all, GM_CHUNK), 1)
    w_all = jnp.where(c <= r, ws_ref[...], 0.0).astype(BF16)
    avg = avg_ref[...]

    def head_mean(x):
        hi, lo = _split_bf16(x)
        return _dot(hi, avg) + _dot(lo, avg)

    v = jax.nn.gelu(v_ref[...])
    d = v - head_mean(v)
    vn = (d * lax.rsqrt(head_mean(d * d) + LN_EPS)).astype(BF16)
    lane_head = lax.broadcasted_iota(jnp.int32, (GM_CHUNK, HALF_W), 1) // HEAD_DIM
    for ci in range(chunks):
        rows = slice(ci * GM_CHUNK, (ci + 1) * GM_CHUNK)
        res = _dot(w_all, vn[rows, :])
        s = res[0:GM_CHUNK, :]
        for h in range(1, N_HEADS):
            s = jnp.where(lane_head == h, res[h * GM_CHUNK:(h + 1) * GM_CHUNK, :], s)
        o_ref[rows, :] = jax.nn.gelu(u_ref[rows, :]) * (s + bias_ref[...])


def _gmlp(uv, ws, avg, bias, *, chunks=4):
    T = uv.shape[0]
    tm = chunks * GM_CHUNK
    return pl.pallas_call(
        functools.partial(_gmlp_kernel, chunks=chunks),
        out_shape=jax.ShapeDtypeStruct((T, HALF_W), F32),
        grid=(T // tm,),
        in_specs=[pl.BlockSpec((tm, HALF_W), lambda i: (i, 0)),
                  pl.BlockSpec((tm, HALF_W), lambda i: (i, 1)),
                  _resident((N_HEADS * GM_CHUNK, GM_CHUNK)),
                  _resident((HALF_W, HALF_W)),
                  _resident((GM_CHUNK, HALF_W))],
        out_specs=pl.BlockSpec((tm, HALF_W), lambda i: (i, 0)),
        compiler_params=_cparams(1),
        name="gmlp",
    )(uv, uv, ws, avg, bias)


def _cmp_kernel(k_ref, v_ref, pe_ref, w1_ref, w2_ref, cos_ref, sin_ref, ko_ref, vo_ref, *, nblk):
    lane = lax.broadcasted_iota(jnp.int32, (nblk, LANES), 1)
    low = lane < HEAD_DIM
    first_half = (lane % HEAD_DIM) < HALF
    for which, (src, dst) in enumerate(((k_ref, ko_ref), (v_ref, vo_ref))):
        pre = jnp.zeros((nblk, LANES), F32)
        for r in range(CMP_STRIDE):
            rows = src[pl.ds(r, nblk, stride=CMP_STRIDE), :]
            nxt = pltpu.roll(rows, nblk - 1, 0)
            pre = pre + _dot((rows + pe_ref[which, r]).astype(BF16), w1_ref[which, r])
            pre = pre + _dot((nxt + pe_ref[which, CMP_STRIDE + r]).astype(BF16), w1_ref[which, CMP_STRIDE + r])
        cmp = _dot(jax.nn.gelu(pre).astype(BF16), w2_ref[which])
        if which == 0:
            rot = jnp.where(first_half, pltpu.roll(cmp, LANES - HALF, 1), pltpu.roll(cmp, HALF, 1))
            cmp = cmp * cos_ref[...] + rot * sin_ref[...]
        dst[0, 0] = jnp.where(low, cmp, 0.0).astype(BF16)
        dst[0, 1] = jnp.where(low, pltpu.roll(cmp, HEAD_DIM, 1), 0.0).astype(BF16)


def _compress(kc, vc, pe, w1, w2, cosc, sinc, *, B, S):
    nblk = S // CMP_STRIDE
    out = jax.ShapeDtypeStruct((B, NSA_KV, nblk, LANES), BF16)
    out_spec = pl.BlockSpec((1, NSA_KV, nblk, LANES), lambda b: (b, 0, 0, 0))
    return pl.pallas_call(
        functools.partial(_cmp_kernel, nblk=nblk),
        out_shape=(out, out),
        grid=(B,),
        in_specs=[pl.BlockSpec((S, LANES), lambda b: (b, 0)),
                  pl.BlockSpec((S, LANES), lambda b: (b, 0)),
                  _resident((2, CMP_LEN, 1, LANES)),
                  _resident((2, CMP_LEN, LANES, LANES)),
                  _resident((2, LANES, LANES)),
                  _resident((nblk, LANES)),
                  _resident((nblk, LANES))],
        out_specs=(out_spec, out_spec),
        compiler_params=_cparams(1),
        name="nsa_compress",
    )(kc, vc, pe, w1, w2, cosc, sinc)


def _softmax_pv(pieces):
    ms = []
    for group in pieces:
        m = None
        for s, _ in group:
            pm = jnp.max(s, axis=-1, keepdims=True)
            m = pm if m is None else jnp.maximum(m, pm)
        ms.append(m)
    accs = [None] * len(pieces)
    for j in range(len(pieces[0])):
        for g, group in enumerate(pieces):
            s, vv = group[j]
            part = _dot(jnp.exp2(s - ms[g]).astype(BF16), vv)
            accs[g] = part if accs[g] is None else accs[g] + part
    return [acc / acc[:, ONES_LANE:ONES_LANE + 1] for acc in accs]


def _nsa_kernel(q_ref, gl_ref, ksa_ref, vsa_ref, kwa_ref, vwa_ref, kc_ref, vc_ref, ovt_ref, o_ref,
                osel_ref, *, S):
    TQ = Q_BLOCK
    R = NSA_HPG * TQ
    G = range(NSA_KV)
    ncmp = S // CMP_STRIDE
    nsel = S // SEL_BLOCK
    i = pl.program_id(1)
    t0 = i * TQ

    q = q_ref[...] * (HEAD_DIM ** -0.5 * LOG2E)
    gate = jax.nn.sigmoid(gl_ref[...])
    lane = lax.broadcasted_iota(jnp.int32, (TQ, LANES), 1)
    lane4 = lax.broadcasted_iota(jnp.int32, (R, LANES), 1)
    trow4 = t0 + (lax.broadcasted_iota(jnp.int32, (R, 1), 0) & (TQ - 1))

    def stack_heads(g):
        heads = []
        for hq in range(NSA_HPG):
            h = g * NSA_HPG + hq
            slab = q[:, (h // 2) * LANES:(h // 2 + 1) * LANES]
            if h % 2 == 1:
                slab = pltpu.roll(slab, HEAD_DIM, 1)
            heads.append(jnp.where(lane < HEAD_DIM, slab, 0.0))
        return jnp.concatenate(heads, axis=0)

    q4 = [stack_heads(g) for g in G]
    q4b = [x.astype(BF16) for x in q4]

    s_c = [_dot_nt(q4b[g], kc_ref[0, g]) for g in G]
    k0 = pl.multiple_of(jnp.maximum(i - WINDOW // TQ, 0) * TQ, TQ)
    win = [[] for _ in G]
    for lo, hi in ((0, 2 * TQ), (2 * TQ, 4 * TQ), (4 * TQ, 5 * TQ)):
        kpos = k0 + lo + lax.broadcasted_iota(jnp.int32, (R, hi - lo), 1)
        allowed = (kpos <= trow4) & (trow4 - kpos < WINDOW)
        for g in G:
            s = _dot_nt(q4b[g], kwa_ref[0, g, pl.ds(k0 + lo, hi - lo), :])
            win[g].append((jnp.where(allowed, s, NEG_INF), vwa_ref[0, g, pl.ds(k0 + lo, hi - lo), :]))

    ncol = lax.broadcasted_iota(jnp.int32, (R, ncmp), 1)
    valid_c = (ncol * CMP_STRIDE + (CMP_LEN - 1) <= trow4) & (ncol < ncmp - 1)
    any_c = (trow4 >= CMP_LEN - 1).astype(F32)
    p_c = []
    for g in G:
        s = jnp.where(valid_c, s_c[g], NEG_INF)
        e = jnp.exp2(s - jnp.max(s, axis=-1, keepdims=True))
        p_c.append(e / jnp.sum(e, axis=-1, keepdims=True) * any_c)
    o_cmp = [_dot(p_c[g].astype(BF16), vc_ref[0, g]) for g in G]

    ovt = ovt_ref[...]
    jrow = lax.broadcasted_iota(jnp.int32, (nsel, TQ), 0)
    tl = t0 + lax.broadcasted_iota(jnp.int32, (nsel, TQ), 1)
    cur = jnp.right_shift(tl, 6)
    forced = (jrow == 0) | (jrow == cur) | (jrow == cur - 1)
    causal_blk = jrow * SEL_BLOCK <= tl
    imp_t = []
    for g in G:
        p = p_c[g]
        p_hi, p_lo = _split_bf16((p[0:TQ] + p[TQ:2 * TQ]) + (p[2 * TQ:3 * TQ] + p[3 * TQ:4 * TQ]))
        imp = (_dot_nt(ovt, p_hi) + _dot_nt(ovt, p_lo))[0:nsel, :]
        imp_t.append(jnp.where(forced, SEL_FORCE, jnp.where(causal_blk, imp, -SEL_FORCE)))

    rank = [jnp.zeros((nsel, TQ), jnp.int32) for _ in G]
    for jp in range(nsel):
        for g in G:
            row = imp_t[g][jp:jp + 1, :]
            beats = (row > imp_t[g]) | ((row == imp_t[g]) & (jrow > jp))
            rank[g] = rank[g] + beats.astype(jnp.int32)

    q4a = []
    for g in G:
        sel_t = (rank[g] < SEL_TOPN).astype(F32)
        if nsel < LANES:
            sel_t = jnp.concatenate([sel_t, jnp.zeros((LANES - nsel, TQ), F32)], axis=0)
        bias = (pltpu.roll(sel_t.T, HEAD_DIM, 1) - 1.0) * (-NEG_INF)
        q4a.append(jnp.where(lane4 < HEAD_DIM, q4[g], jnp.concatenate([bias] * NSA_HPG, axis=0)).astype(BF16))

    o_win = _softmax_pv(win)

    col_step = lax.broadcasted_iota(jnp.int32, (R, SEL_STEP), 1)
    for n in range(S // SEL_STEP):
        @pl.when(i // (SEL_STEP // TQ) == n)
        def _(n=n):
            pieces = [[] for _ in G]
            for j in range(n + 1):
                lo, hi = j * SEL_STEP, (j + 1) * SEL_STEP
                for g in G:
                    s = _dot_nt(q4a[g], ksa_ref[0, g, lo:hi, :])
                    if j == n:
                        s = jnp.where(lo + col_step <= trow4, s, NEG_INF)
                    pieces[g].append((s, vsa_ref[0, g, lo:hi, :]))
            for g, o in enumerate(_softmax_pv(pieces)):
                osel_ref[g] = o

    for g in G:
        o_s = osel_ref[g]
        outs = []
        for hq in range(NSA_HPG):
            h = g * NSA_HPG + hq
            rows = slice(hq * TQ, (hq + 1) * TQ)
            outs.append(gate[:, 3 * h:3 * h + 1] * o_cmp[g][rows] + gate[:, 3 * h + 1:3 * h + 2] * o_s[rows]
                        + gate[:, 3 * h + 2:3 * h + 3] * o_win[g][rows])
        for pair in range(NSA_HPG // 2):
            c0 = (g * NSA_HPG + 2 * pair) * HEAD_DIM
            o_ref[:, c0:c0 + LANES] = jnp.where(lane < HEAD_DIM, outs[2 * pair],
                                                pltpu.roll(outs[2 * pair + 1], HEAD_DIM, 1))


def _nsa(q, gl, ksa, vsa, kwa, vwa, kcmp, vcmp, ovt, *, B, S):
    T = q.shape[0]
    nq = S // Q_BLOCK
    ncmp = S // CMP_STRIDE
    per_batch = lambda rows: pl.BlockSpec((1, NSA_KV, rows, LANES), lambda b, i: (b, 0, 0, 0))
    return pl.pallas_call(
        functools.partial(_nsa_kernel, S=S),
        out_shape=jax.ShapeDtypeStruct((T, HALF_W), F32),
        grid=(B, nq),
        in_specs=[pl.BlockSpec((Q_BLOCK, HALF_W), lambda b, i: (b * nq + i, 0)),
                  pl.BlockSpec((Q_BLOCK, LANES), lambda b, i: (b * nq + i, 0)),
                  per_batch(S), per_batch(S), per_batch(S), per_batch(S),
                  per_batch(ncmp), per_batch(ncmp),
                  _resident((LANES, ncmp))],
        out_specs=pl.BlockSpec((Q_BLOCK, HALF_W), lambda b, i: (b * nq + i, 0)),
        scratch_shapes=[pltpu.VMEM((NSA_KV, NSA_HPG * Q_BLOCK, LANES), F32)],
        compiler_params=_cparams(2),
        name="nsa_attn",
    )(q, gl, ksa, vsa, kwa, vwa, kcmp, vcmp, ovt)


def _pool_kernel(z_ref, pw_ref, ps_ref, o_ref):
    S = z_ref.shape[0]
    row = lax.broadcasted_iota(jnp.int32, (S, POOL_GROUP), 0)
    for gi, w in enumerate(POOL_WINDOWS):
        cols = slice(gi * POOL_GROUP, (gi + 1) * POOL_GROUP)
        zg = z_ref[:, cols]
        acc = zg
        k = 1
        while k < w:
            acc = acc + jnp.where(row >= k, pltpu.roll(acc, k, 0), 0.0)
            k *= 2
        cnt = jnp.minimum(row + 1, w).astype(F32)
        pooled = acc / cnt - zg
        y = _dot(pooled.astype(BF16), pw_ref[gi].astype(BF16))
        o_ref[:, cols] = y * ps_ref[:, cols]


def _pool(z, pw, ps, *, B, S):
    T = z.shape[0]
    return pl.pallas_call(
        _pool_kernel,
        out_shape=jax.ShapeDtypeStruct((T, HALF_W), F32),
        grid=(B,),
        in_specs=[pl.BlockSpec((S, HALF_W), lambda b: (b, 0)),
                  _resident((len(POOL_WINDOWS), POOL_GROUP, POOL_GROUP)),
                  _resident((1, HALF_W))],
        out_specs=pl.BlockSpec((S, HALF_W), lambda b: (b, 0)),
        compiler_params=_cparams(1),
        name="pool",
    )(z, pw, ps)


def _s5_disc_kernel(lr_ref, li_ref, ldt_ref, lbr_ref, lbi_ref, cr_ref, ci_ref):
    lr = lr_ref[...]
    li = li_ref[...]
    dt = jnp.exp(ldt_ref[...])
    mag = jnp.exp(lr * dt)
    ang = li * dt
    lbr = mag * jnp.cos(ang)
    lbi = mag * jnp.sin(ang)
    nr = lbr - 1.0
    den = lr * lr + li * li
    lbr_ref[...] = lbr
    lbi_ref[...] = lbi
    cr_ref[...] = (nr * lr + lbi * li) / den
    ci_ref[...] = (lbi * lr - nr * li) / den


def _s5_bbar_kernel(cr_ref, ci_ref, br_ref, bi_ref, or_ref, oi_ref):
    cr = cr_ref[...]
    ci = ci_ref[...]
    br = br_ref[...]
    bi = bi_ref[...]
    or_ref[...] = cr * br - ci * bi
    oi_ref[...] = cr * bi + ci * br


def _s5_params(lam_re, lam_im, log_dt, b_re, b_im):
    gp = jax.ShapeDtypeStruct((S5_GROUPS, S5_STATE), F32)
    lbr, lbi, cr, ci = pl.pallas_call(_s5_disc_kernel, out_shape=(gp, gp, gp, gp), name="s5_disc")(
        lam_re, lam_im, log_dt.reshape(S5_GROUPS, 1))
    flat = jax.ShapeDtypeStruct((S5_MODES, S5_GROUP_CH), F32)
    bbr, bbi = pl.pallas_call(_s5_bbar_kernel, out_shape=(flat, flat), name="s5_bbar")(
        cr.reshape(S5_MODES, 1), ci.reshape(S5_MODES, 1),
        b_re.reshape(S5_MODES, S5_GROUP_CH), b_im.reshape(S5_MODES, S5_GROUP_CH))
    return lbr, lbi, bbr, bbi


def _s5_kernel(u_ref, bsb_ref, lr_ref, li_ref, cre_ref, cim_ref, d_ref, wglu_ref, o_ref,
               xs_ref, st_ref, *, steps, nb):
    M = S5_MODES
    W = S5_SUPER_MODES

    @pl.when(pl.program_id(0) == 0)
    def _():
        st_ref[...] = jnp.zeros_like(st_ref)

    half = steps // 2 * nb
    halves = (slice(0, half), slice(half, 2 * half))

    def input_drive(rows, sb):
        bu = _dot(u_ref[rows, sb * S5_SUPER_CH:(sb + 1) * S5_SUPER_CH].astype(BF16), bsb_ref[sb])
        xs_ref[rows, sb * W:(sb + 1) * W] = bu[:, :W]
        xs_ref[rows, M + sb * W:M + (sb + 1) * W] = bu[:, W:]

    def readout(rows, sb):
        return (_dot(xs_ref[rows, sb * W:(sb + 1) * W].astype(BF16), cre_ref[sb])
                - _dot(xs_ref[rows, M + sb * W:M + (sb + 1) * W].astype(BF16), cim_ref[sb]))

    def scan(first, carry, side_work):
        xr, xi = carry
        lr = lr_ref[...]
        li = li_ref[...]
        every = (steps // 2) // len(side_work)
        done = []
        for t in range(steps // 2):
            if t % every == 0:
                done.append(side_work[t // every]())
            r0 = (first + t) * nb
            nr = lr * xr - li * xi + xs_ref[r0:r0 + nb, 0:M]
            ni = lr * xi + li * xr + xs_ref[r0:r0 + nb, M:2 * M]
            xs_ref[r0:r0 + nb, 0:M] = nr
            xs_ref[r0:r0 + nb, M:2 * M] = ni
            xr, xi = nr, ni
        return (xr, xi), done

    for sb in range(S5_SUPER):
        input_drive(halves[0], sb)
    carry = (st_ref[:, 0:M], st_ref[:, M:2 * M])
    carry, _ = scan(0, carry, [functools.partial(input_drive, halves[1], sb) for sb in range(S5_SUPER)])
    carry, ys0 = scan(steps // 2, carry, [functools.partial(readout, halves[0], sb) for sb in range(S5_SUPER)])
    st_ref[:, 0:M] = carry[0]
    st_ref[:, M:2 * M] = carry[1]
    ys1 = [readout(halves[1], sb) for sb in range(S5_SUPER)]

    for rows, ys in zip(halves, (ys0, ys1)):
        y = jax.nn.gelu(jnp.concatenate(ys, axis=1) + d_ref[...] * u_ref[rows, :])
        ab = _dot(y.astype(BF16), wglu_ref[...])
        o_ref[rows, :] = ab[:, :S5_WIDTH] * jax.nn.sigmoid(ab[:, S5_WIDTH:])


def _s5(u_tb, bsb, lr_b, li_b, cre, cim, d, wglu, *, nb, steps=64):
    R = u_tb.shape[0]
    rows = steps * nb
    return pl.pallas_call(
        functools.partial(_s5_kernel, steps=steps, nb=nb),
        out_shape=jax.ShapeDtypeStruct((R, S5_WIDTH), F32),
        grid=(R // rows,),
        in_specs=[pl.BlockSpec((rows, S5_WIDTH), lambda i: (i, 0)),
                  _resident((S5_SUPER, S5_SUPER_CH, 2 * S5_SUPER_MODES)),
                  _resident((nb, S5_MODES)),
                  _resident((nb, S5_MODES)),
                  _resident((S5_SUPER, S5_SUPER_MODES, S5_SUPER_CH)),
                  _resident((S5_SUPER, S5_SUPER_MODES, S5_SUPER_CH)),
                  _resident((1, S5_WIDTH)),
                  _resident((S5_WIDTH, 2 * S5_WIDTH))],
        out_specs=pl.BlockSpec((rows, S5_WIDTH), lambda i: (i, 0)),
        scratch_shapes=[pltpu.VMEM((rows, 2 * S5_MODES), F32),
                        pltpu.VMEM((nb, 2 * S5_MODES), F32)],
        compiler_params=_cparams(1),
        name="s5",
    )(u_tb, bsb, lr_b, li_b, cre, cim, d, wglu)


def _rope_tables(S):
    inv = ROPE_THETA ** (-jnp.arange(HALF, dtype=F32) / HALF)
    ang = jnp.arange(S, dtype=F32)[:, None] * inv[None, :]
    cos, sin = jnp.cos(ang), jnp.sin(ang)
    cos_t = jnp.tile(cos, (1, LANES // HALF))
    sin_t = jnp.tile(jnp.concatenate([-sin, sin], axis=1), (1, LANES // HEAD_DIM))
    end_pos = (jnp.arange(S // CMP_STRIDE, dtype=F32) * CMP_STRIDE + (CMP_LEN - 1))[:, None] * inv[None, :]
    cosc = jnp.tile(jnp.cos(end_pos), (1, LANES // HALF))
    sinc = jnp.tile(jnp.concatenate([-jnp.sin(end_pos), jnp.sin(end_pos)], axis=1), (1, LANES // HEAD_DIM))
    return cos_t, sin_t, cosc, sinc


def _structure_constants(S):
    ncmp = S // CMP_STRIDE
    nsel = S // SEL_BLOCK
    assert nsel <= LANES - HEAD_DIM
    j = jnp.arange(LANES, dtype=jnp.int32)[:, None]
    n = jnp.arange(ncmp, dtype=jnp.int32)[None, :]
    ovt = ((n * CMP_STRIDE < (j + 1) * SEL_BLOCK) & (n * CMP_STRIDE + CMP_LEN > j * SEL_BLOCK)
           & (j < nsel) & (n < ncmp - 1)).astype(BF16)
    key_blk = (jnp.arange(S, dtype=jnp.int32) // SEL_BLOCK)[:, None]
    blk_tab = (key_blk + HEAD_DIM == jnp.arange(LANES, dtype=jnp.int32)[None, :]).astype(F32)
    c = jnp.arange(HALF_W, dtype=jnp.int32)
    avg = ((c[:, None] // HEAD_DIM == c[None, :] // HEAD_DIM).astype(F32) / HEAD_DIM).astype(BF16)
    return ovt, blk_tab, avg


def _even_mixer(x, g, w_in, gm_w_s, gm_b, pe, w1, w2, consts, *, B, S):
    cos_t, sin_t, cosc, sinc, ovt, blk_tab, avg = consts
    w_pad = jnp.pad(w_in, ((0, 0), (0, EVEN_IN_PAD - EVEN_IN))).astype(BF16)
    uv, q, gl, kc, vc, ksa, vsa, kwa, vwa = _even_proj(x, g, w_pad, cos_t, sin_t, blk_tab, B=B, S=S)
    out_a = _gmlp(uv, gm_w_s.reshape(N_HEADS * GM_CHUNK, GM_CHUNK), avg,
                  jnp.repeat(gm_b.T, HEAD_DIM, axis=1))
    eye = jnp.eye(NSA_KV, dtype=F32)
    w1_bd = jnp.einsum('wlde,gh->wlgdhe', w1.reshape(2, CMP_LEN, HEAD_DIM, HEAD_DIM), eye)
    w2_bd = jnp.einsum('wde,gh->wgdhe', w2, eye)
    kcmp, vcmp = _compress(kc, vc, jnp.tile(pe.reshape(2, CMP_LEN, 1, HEAD_DIM), (1, 1, 1, NSA_KV)),
                           w1_bd.reshape(2, CMP_LEN, LANES, LANES).astype(BF16),
                           w2_bd.reshape(2, LANES, LANES).astype(BF16), cosc, sinc, B=B, S=S)
    out_b = _nsa(q, gl, ksa, vsa, kwa, vwa, kcmp, vcmp, ovt, B=B, S=S)
    return out_a, out_b


def _odd_mixer(x, g, w_in, pool_w, pool_scale, lam_re, lam_im, b_re, b_im, c_re, c_im,
               d_skip, log_dt, w_glu, *, B, S):
    zc, u_sb = _odd_proj(x, g, w_in.astype(BF16), B=B, S=S)
    y_c = _pool(zc, pool_w, pool_scale.reshape(1, HALF_W), B=B, S=S)

    lbr, lbi, bbr, bbi = _s5_params(lam_re, lam_im, log_dt, b_re, b_im)
    gps = S5_GROUPS // S5_SUPER
    eye = jnp.eye(gps, dtype=F32)
    blockdiag_in = lambda bb: jnp.einsum(
        'sgpc,gh->sgchp', bb.reshape(S5_SUPER, gps, S5_STATE, S5_GROUP_CH), eye
    ).reshape(S5_SUPER, S5_SUPER_CH, S5_SUPER_MODES)
    blockdiag_out = lambda cc: jnp.einsum(
        'sgcp,gh->sgphc', cc.reshape(S5_SUPER, gps, S5_GROUP_CH, S5_STATE), eye
    ).reshape(S5_SUPER, S5_SUPER_MODES, S5_SUPER_CH)
    bsb = jnp.concatenate([blockdiag_in(bbr), blockdiag_in(bbi)], axis=2).astype(BF16)
    bcast = lambda a: jnp.broadcast_to(a.reshape(1, S5_MODES), (B, S5_MODES))
    y_tb = _s5(u_sb.reshape(S * B, S5_WIDTH), bsb, bcast(lbr), bcast(lbi),
               blockdiag_out(c_re).astype(BF16), blockdiag_out(c_im).astype(BF16),
               d_skip.reshape(1, S5_WIDTH), w_glu.astype(BF16), nb=B)
    return y_c, y_tb.reshape(S, B * S5_WIDTH)


def kernel(x, norm_w, ffn_w_gate, ffn_w_up, ffn_w_down, final_norm_w, ev_w_in, ev_w_out, gm_w_s, gm_b,
           nsa_cmp_pe, nsa_cmp_w1, nsa_cmp_w2, od_w_in, od_w_out, pool_w, pool_scale, s5_lam_re,
           s5_lam_im, s5_b_re, s5_b_im, s5_c_re, s5_c_im, s5_d, s5_log_dt, s5_w_glu):
    B, S, _ = x.shape
    consts = _rope_tables(S) + _structure_constants(S)
    tm = 512
    spt = S // tm
    batch_major = pl.BlockSpec((tm, HALF_W), lambda i: (i, 0))
    time_major = pl.BlockSpec((tm, HALF_W), lambda i: (i % spt, i // spt))
    wg, wu, wd = ffn_w_gate.astype(BF16), ffn_w_up.astype(BF16), ffn_w_down.astype(BF16)
    ffn_w = lambda l, k: (norm_w[l, 2 * k].reshape(1, D_MODEL), wg, wu, wd, l, k)
    gf = final_norm_w.reshape(1, D_MODEL)
    xt = x.reshape(B * S, D_MODEL)
    for l in range(DEPTH):
        xt = _ffn(xt, *ffn_w(l, 0), tm=tm)
        g = norm_w[l, 1].reshape(1, D_MODEL)
        i = l // 2
        if l % 2 == 0:
            a, b = _even_mixer(xt, g, ev_w_in[i], gm_w_s[i], gm_b[i], nsa_cmp_pe[i], nsa_cmp_w1[i],
                               nsa_cmp_w2[i], consts, B=B, S=S)
            w_out, b_spec = ev_w_out[i], batch_major
        else:
            a, b = _odd_mixer(xt, g, od_w_in[i], pool_w[i], pool_scale[i], s5_lam_re[i], s5_lam_im[i],
                              s5_b_re[i], s5_b_im[i], s5_c_re[i], s5_c_im[i], s5_d[i], s5_log_dt[i],
                              s5_w_glu[i], B=B, S=S)
            w_out, b_spec = od_w_out[i], time_major
        xt = _mix_ffn(a, b, b_spec, w_out.astype(BF16), xt, *ffn_w(l, 1), gf, final=(l == DEPTH - 1), tm=tm)
    return xt.reshape(B, S, D_MODEL)
```

```python
import functools

import jax
import jax.numpy as jnp
from jax import lax
from jax.experimental import pallas as pl
from jax.experimental.pallas import tpu as pltpu

F32 = jnp.float32
BF16 = jnp.bfloat16

D_MODEL = 1024
DEPTH = 4
HEAD_DIM = 64
HALF = HEAD_DIM // 2
ROPE_THETA = 10000.0
N_HEADS = 8
GM_CHUNK = 128
NSA_KV = 2
NSA_HPG = 4
CMP_LEN = 32
CMP_STRIDE = 16
SEL_BLOCK = 64
SEL_TOPN = 8
WINDOW = 512
Q_BLOCK = 128
POOL_WINDOWS = (2, 4, 8, 16)
POOL_GROUP = 128
S5_GROUPS = 32
S5_GROUP_CH = 16
S5_STATE = 64
S5_WIDTH = 512
S5_MODES = S5_GROUPS * S5_STATE
S5_SUPER = 4
S5_SUPER_CH = S5_WIDTH // S5_SUPER
S5_SUPER_MODES = S5_MODES // S5_SUPER
FFN_DIM = 2816
RMS_EPS = 1e-6
LN_EPS = 1e-5
NEG_INF = -1e30
SEL_FORCE = 1e4
LOG2E = 1.4426950408889634

LANES = 128
HALF_W = 512
EVEN_IN = 2328
EVEN_IN_PAD = 2432
COL_U, COL_V, COL_Q = 0, 512, 1024
COL_KC, COL_VC, COL_KS, COL_VS, COL_KW, COL_VW, COL_GL = 1536, 1664, 1792, 1920, 2048, 2176, 2304
SEL_STEP = 512
ONES_LANE = HEAD_DIM

VMEM_LIMIT = 56 * 1024 * 1024


def _cparams(n_axes):
    return pltpu.CompilerParams(dimension_semantics=("arbitrary",) * n_axes,
                                vmem_limit_bytes=VMEM_LIMIT)


def _resident(shape):
    nd = len(shape)
    return pl.BlockSpec(shape, lambda *_: (0,) * nd, pipeline_mode=pl.Buffered(1))


def _rms(x, g):
    ms = jnp.mean(x * x, axis=-1, keepdims=True)
    return x * lax.rsqrt(ms + RMS_EPS) * g


def _dot(a, b):
    return jnp.dot(a, b, preferred_element_type=F32)


def _dot_nt(a, b):
    return lax.dot_general(a, b, (((1,), (1,)), ((), ())), preferred_element_type=F32)


def _split_bf16(x):
    hi = x.astype(BF16)
    return hi, (x - hi.astype(F32)).astype(BF16)


def _swiglu_half_step(x, g, wg_ref, wu_ref, wd_ref, fc):
    h = _rms(x, g).astype(BF16)
    acc = jnp.zeros_like(x)
    for c in range(FFN_DIM // fc):
        sl = slice(c * fc, (c + 1) * fc)
        a = _dot(h, wg_ref[:, sl])
        b = _dot(h, wu_ref[:, sl])
        t = (a * jax.nn.sigmoid(a) * b).astype(BF16)
        acc = acc + _dot(t, wd_ref[sl, :])
    return x + 0.5 * acc


def _mix_ffn_kernel(a_ref, b_ref, wo_ref, x_ref, g_ref, wg_ref, wu_ref, wd_ref, gf_ref, o_ref, *, fc, final):
    x = x_ref[...] + _dot(a_ref[...].astype(BF16), wo_ref[0:HALF_W, :])
    x = x + _dot(b_ref[...].astype(BF16), wo_ref[HALF_W:2 * HALF_W, :])
    x = _swiglu_half_step(x, g_ref[...], wg_ref, wu_ref, wd_ref, fc)
    o_ref[...] = _rms(x, gf_ref[...]) if final else x


def _ffn_weight_specs(l, k):
    pick = lambda rows, cols: pl.BlockSpec((None, None, rows, cols), lambda *_: (l, k, 0, 0),
                                           pipeline_mode=pl.Buffered(1))
    return [_resident((1, D_MODEL)), pick(D_MODEL, FFN_DIM), pick(D_MODEL, FFN_DIM), pick(FFN_DIM, D_MODEL)]


def _mix_ffn(a, b, wo, x, g, wg, wu, wd, l, k, gf, *, final, tm=512, fc=256):
    T = x.shape[0]
    return pl.pallas_call(
        functools.partial(_mix_ffn_kernel, fc=fc, final=final),
        out_shape=jax.ShapeDtypeStruct((T, D_MODEL), F32),
        grid=(T // tm,),
        in_specs=[pl.BlockSpec((tm, HALF_W), lambda i: (i, 0)), pl.BlockSpec((tm, HALF_W), lambda i: (i, 0)),
                  _resident((D_MODEL, D_MODEL)),
                  pl.BlockSpec((tm, D_MODEL), lambda i: (i, 0))] + _ffn_weight_specs(l, k)
                 + [_resident((1, D_MODEL))],
        out_specs=pl.BlockSpec((tm, D_MODEL), lambda i: (i, 0)),
        compiler_params=_cparams(1),
        name="mix_ffn",
    )(a, b, wo, x, g, wg, wu, wd, gf)


def _ffn_odd_proj_kernel(x_ref, g_ref, wg_ref, wu_ref, wd_ref, gp_ref, wp_ref, xo_ref, zc_ref, u_ref,
                         *, fc, nb, ts):
    x = _swiglu_half_step(x_ref[...].reshape(nb * ts, D_MODEL), g_ref[...], wg_ref, wu_ref, wd_ref, fc)
    xo_ref[...] = x.reshape(nb, ts, D_MODEL)
    z = _dot(_rms(x, gp_ref[...]).astype(BF16), wp_ref[...])
    zc_ref[...] = z[:, :HALF_W].reshape(nb, ts, HALF_W)
    for sb in range(S5_SUPER):
        c0 = HALF_W + sb * S5_SUPER_CH
        for b in range(nb):
            u_ref[sb, pl.ds(b, ts, stride=nb), :] = z[b * ts:(b + 1) * ts, c0:c0 + S5_SUPER_CH]


def _ffn_odd_proj(x, g, wg, wu, wd, l, k, gp, wp, *, B, S, ts=64, fc=256):
    whole = lambda width: pl.BlockSpec((B, ts, width), lambda i: (0, i, 0))
    return pl.pallas_call(
        functools.partial(_ffn_odd_proj_kernel, fc=fc, nb=B, ts=ts),
        out_shape=(jax.ShapeDtypeStruct((B, S, D_MODEL), F32), jax.ShapeDtypeStruct((B, S, HALF_W), F32),
                   jax.ShapeDtypeStruct((S5_SUPER, S * B, S5_SUPER_CH), F32)),
        grid=(S // ts,),
        in_specs=[whole(D_MODEL)] + _ffn_weight_specs(l, k)
                 + [_resident((1, D_MODEL)), _resident((D_MODEL, 2 * HALF_W))],
        out_specs=(whole(D_MODEL), whole(HALF_W),
                   pl.BlockSpec((S5_SUPER, ts * B, S5_SUPER_CH), lambda i: (0, i, 0))),
        compiler_params=_cparams(1),
        name="ffn_odd_proj",
    )(x, g, wg, wu, wd, gp, wp)


def _ffn_even_proj_kernel(x_ref, g_ref, wg_ref, wu_ref, wd_ref, gp_ref, w_ref, cos_ref, sin_ref, blk_ref,
                          xo_ref, uv_ref, q_ref, gl_ref, kc_ref, vc_ref, ksa_ref, vsa_ref, kwa_ref, vwa_ref,
                          *, fc):
    x = _swiglu_half_step(x_ref[...], g_ref[...], wg_ref, wu_ref, wd_ref, fc)
    xo_ref[...] = x
    z = _dot(_rms(x, gp_ref[...]).astype(BF16), w_ref[...])
    cos = cos_ref[...]
    sin = sin_ref[...]
    lane = lax.broadcasted_iota(jnp.int32, cos.shape, 1)
    first_half = (lane % HEAD_DIM) < HALF
    low = lane < HEAD_DIM

    def slab(c0, rotary):
        xs = z[:, c0:c0 + LANES]
        if not rotary:
            return xs
        rot = jnp.where(first_half, pltpu.roll(xs, LANES - HALF, 1), pltpu.roll(xs, HALF, 1))
        return xs * cos + rot * sin

    uv_ref[...] = z[:, COL_U:COL_Q]
    for c in range(HALF_W // LANES):
        q_ref[:, c * LANES:(c + 1) * LANES] = slab(COL_Q + c * LANES, True)
    gl_ref[...] = z[:, COL_GL:COL_GL + LANES]
    kc_ref[...] = slab(COL_KC, False)
    vc_ref[...] = slab(COL_VC, False)
    ones = (lane == ONES_LANE).astype(F32)
    for c0, rotary, extra, dst in ((COL_KS, True, blk_ref[...], ksa_ref), (COL_VS, False, ones, vsa_ref),
                                   (COL_KW, True, 0.0, kwa_ref), (COL_VW, False, ones, vwa_ref)):
        xs = slab(c0, rotary)
        dst[0, 0] = jnp.where(low, xs, extra).astype(BF16)
        dst[0, 1] = jnp.where(low, pltpu.roll(xs, HEAD_DIM, 1), extra).astype(BF16)


def _ffn_even_proj(x, g, wg, wu, wd, l, k, gp, w, cos, sin, blk, *, B, S, tm=512, fc=256):
    T = x.shape[0]
    spt = S // tm
    tok = lambda width: pl.BlockSpec((tm, width), lambda i: (i, 0))
    pos = pl.BlockSpec((tm, LANES), lambda i: (i % spt, 0))
    grp = pl.BlockSpec((1, NSA_KV, tm, LANES), lambda i: (i // spt, 0, i % spt, 0))
    aug = jax.ShapeDtypeStruct((B, NSA_KV, S, LANES), BF16)
    tokens = lambda width: jax.ShapeDtypeStruct((T, width), F32)
    return pl.pallas_call(
        functools.partial(_ffn_even_proj_kernel, fc=fc),
        out_shape=(tokens(D_MODEL), tokens(2 * HALF_W), tokens(HALF_W), tokens(LANES), tokens(LANES),
                   tokens(LANES), aug, aug, aug, aug),
        grid=(T // tm,),
        in_specs=[tok(D_MODEL)] + _ffn_weight_specs(l, k)
                 + [_resident((1, D_MODEL)), _resident((D_MODEL, EVEN_IN_PAD)), pos, pos, pos],
        out_specs=(tok(D_MODEL), tok(2 * HALF_W), tok(HALF_W), tok(LANES), tok(LANES), tok(LANES),
                   grp, grp, grp, grp),
        compiler_params=_cparams(1),
        name="ffn_even_proj",
    )(x, g, wg, wu, wd, gp, w, cos, sin, blk)


def _gmlp_kernel(u_ref, v_ref, ws_ref, avg_ref, bias_ref, o_ref, *, chunks):
    rows_all = N_HEADS * GM_CHUNK
    r = lax.broadcasted_iota(jnp.int32, (rows_all, GM_CHUNK), 0) & (GM_CHUNK - 1)
    c = lax.broadcasted_iota(jnp.int32, (rows_all, GM_CHUNK), 1)
    w_all = jnp.where(c <= r, ws_ref[...], 0.0).astype(BF16)
    avg = avg_ref[...]

    def head_mean(x):
        return _dot(x.astype(BF16), avg)

    v = jax.nn.gelu(v_ref[...])
    d = v - head_mean(v)
    vn = (d * lax.rsqrt(head_mean(d * d) + LN_EPS)).astype(BF16)
    lane_head = lax.broadcasted_iota(jnp.int32, (GM_CHUNK, HALF_W), 1) // HEAD_DIM
    for ci in range(chunks):
        rows = slice(ci * GM_CHUNK, (ci + 1) * GM_CHUNK)
        res = _dot(w_all, vn[rows, :])
        s = res[0:GM_CHUNK, :]
        for h in range(1, N_HEADS):
            s = jnp.where(lane_head == h, res[h * GM_CHUNK:(h + 1) * GM_CHUNK, :], s)
        o_ref[rows, :] = jax.nn.gelu(u_ref[rows, :]) * (s + bias_ref[...])


def _gmlp(uv, ws, avg, bias, *, chunks=4):
    T = uv.shape[0]
    tm = chunks * GM_CHUNK
    return pl.pallas_call(
        functools.partial(_gmlp_kernel, chunks=chunks),
        out_shape=jax.ShapeDtypeStruct((T, HALF_W), F32),
        grid=(T // tm,),
        in_specs=[pl.BlockSpec((tm, HALF_W), lambda i: (i, 0)),
                  pl.BlockSpec((tm, HALF_W), lambda i: (i, 1)),
                  _resident((N_HEADS * GM_CHUNK, GM_CHUNK)),
                  _resident((HALF_W, HALF_W)),
                  _resident((GM_CHUNK, HALF_W))],
        out_specs=pl.BlockSpec((tm, HALF_W), lambda i: (i, 0)),
        compiler_params=_cparams(1),
        name="gmlp",
    )(uv, uv, ws, avg, bias)


def _cmp_kernel(k_ref, v_ref, pe_ref, w1_ref, w2_ref, cos_ref, sin_ref, ko_ref, vo_ref, *, nblk):
    lane = lax.broadcasted_iota(jnp.int32, (nblk, LANES), 1)
    low = lane < HEAD_DIM
    first_half = (lane % HEAD_DIM) < HALF
    for which, (src, dst) in enumerate(((k_ref, ko_ref), (v_ref, vo_ref))):
        pre = jnp.zeros((nblk, LANES), F32)
        for r in range(CMP_STRIDE):
            rows = src[pl.ds(r, nblk, stride=CMP_STRIDE), :]
            nxt = pltpu.roll(rows, nblk - 1, 0)
            pre = pre + _dot((rows + pe_ref[which, r]).astype(BF16), w1_ref[which, r])
            pre = pre + _dot((nxt + pe_ref[which, CMP_STRIDE + r]).astype(BF16), w1_ref[which, CMP_STRIDE + r])
        cmp = _dot(jax.nn.gelu(pre).astype(BF16), w2_ref[which])
        if which == 0:
            rot = jnp.where(first_half, pltpu.roll(cmp, LANES - HALF, 1), pltpu.roll(cmp, HALF, 1))
            cmp = cmp * cos_ref[...] + rot * sin_ref[...]
        dst[0, 0] = jnp.where(low, cmp, 0.0).astype(BF16)
        dst[0, 1] = jnp.where(low, pltpu.roll(cmp, HEAD_DIM, 1), 0.0).astype(BF16)


def _compress(kc, vc, pe, w1, w2, cosc, sinc, *, B, S):
    nblk = S // CMP_STRIDE
    out = jax.ShapeDtypeStruct((B, NSA_KV, nblk, LANES), BF16)
    out_spec = pl.BlockSpec((1, NSA_KV, nblk, LANES), lambda b: (b, 0, 0, 0))
    return pl.pallas_call(
        functools.partial(_cmp_kernel, nblk=nblk),
        out_shape=(out, out),
        grid=(B,),
        in_specs=[pl.BlockSpec((S, LANES), lambda b: (b, 0)),
                  pl.BlockSpec((S, LANES), lambda b: (b, 0)),
                  _resident((2, CMP_LEN, 1, LANES)),
                  _resident((2, CMP_LEN, LANES, LANES)),
                  _resident((2, LANES, LANES)),
                  _resident((nblk, LANES)),
                  _resident((nblk, LANES))],
        out_specs=(out_spec, out_spec),
        compiler_params=_cparams(1),
        name="nsa_compress",
    )(kc, vc, pe, w1, w2, cosc, sinc)


def _softmax_pv(pieces):
    ms = []
    for group in pieces:
        m = None
        for s, _ in group:
            pm = jnp.max(s, axis=-1, keepdims=True)
            m = pm if m is None else jnp.maximum(m, pm)
        ms.append(m)
    accs = [None] * len(pieces)
    for j in range(len(pieces[0])):
        for g, group in enumerate(pieces):
            s, vv = group[j]
            part = _dot(jnp.exp2(s - ms[g]).astype(BF16), vv)
            accs[g] = part if accs[g] is None else accs[g] + part
    return [acc / acc[:, ONES_LANE:ONES_LANE + 1] for acc in accs]


def _nsa_kernel(q_ref, gl_ref, ksa_ref, vsa_ref, kwa_ref, vwa_ref, kc_ref, vc_ref, ovt_ref, o_ref,
                osel_ref, *, S):
    TQ = Q_BLOCK
    R = NSA_HPG * TQ
    G = range(NSA_KV)
    ncmp = S // CMP_STRIDE
    nsel = S // SEL_BLOCK
    i = pl.program_id(1)
    t0 = i * TQ

    q = q_ref[...] * (HEAD_DIM ** -0.5 * LOG2E)
    gate = jax.nn.sigmoid(gl_ref[...])
    lane = lax.broadcasted_iota(jnp.int32, (TQ, LANES), 1)
    lane4 = lax.broadcasted_iota(jnp.int32, (R, LANES), 1)
    trow4 = t0 + (lax.broadcasted_iota(jnp.int32, (R, 1), 0) & (TQ - 1))

    def stack_heads(g):
        heads = []
        for hq in range(NSA_HPG):
            h = g * NSA_HPG + hq
            slab = q[:, (h // 2) * LANES:(h // 2 + 1) * LANES]
            if h % 2 == 1:
                slab = pltpu.roll(slab, HEAD_DIM, 1)
            heads.append(jnp.where(lane < HEAD_DIM, slab, 0.0))
        return jnp.concatenate(heads, axis=0)

    q4 = [stack_heads(g) for g in G]
    q4b = [x.astype(BF16) for x in q4]

    s_c = [_dot_nt(q4b[g], kc_ref[0, g]) for g in G]
    k0 = pl.multiple_of(jnp.maximum(i - WINDOW // TQ, 0) * TQ, TQ)
    win = [[] for _ in G]
    for lo, hi in ((0, 2 * TQ), (2 * TQ, 4 * TQ), (4 * TQ, 5 * TQ)):
        kpos = k0 + lo + lax.broadcasted_iota(jnp.int32, (R, hi - lo), 1)
        allowed = (kpos <= trow4) & (trow4 - kpos < WINDOW)
        for g in G:
            s = _dot_nt(q4b[g], kwa_ref[0, g, pl.ds(k0 + lo, hi - lo), :])
            win[g].append((jnp.where(allowed, s, NEG_INF), vwa_ref[0, g, pl.ds(k0 + lo, hi - lo), :]))

    ncol = lax.broadcasted_iota(jnp.int32, (R, ncmp), 1)
    valid_c = (ncol * CMP_STRIDE + (CMP_LEN - 1) <= trow4) & (ncol < ncmp - 1)
    any_c = (trow4 >= CMP_LEN - 1).astype(F32)
    p_c = []
    for g in G:
        s = jnp.where(valid_c, s_c[g], NEG_INF)
        e = jnp.exp2(s - jnp.max(s, axis=-1, keepdims=True))
        p_c.append(e / jnp.sum(e, axis=-1, keepdims=True) * any_c)
    o_cmp = [_dot(p_c[g].astype(BF16), vc_ref[0, g]) for g in G]

    ovt = ovt_ref[...]
    jrow = lax.broadcasted_iota(jnp.int32, (nsel, TQ), 0)
    tl = t0 + lax.broadcasted_iota(jnp.int32, (nsel, TQ), 1)
    cur = jnp.right_shift(tl, 6)
    forced = (jrow == 0) | (jrow == cur) | (jrow == cur - 1)
    causal_blk = jrow * SEL_BLOCK <= tl
    imp_t = []
    for g in G:
        p = p_c[g]
        p_hi, p_lo = _split_bf16((p[0:TQ] + p[TQ:2 * TQ]) + (p[2 * TQ:3 * TQ] + p[3 * TQ:4 * TQ]))
        imp = (_dot_nt(ovt, p_hi) + _dot_nt(ovt, p_lo))[0:nsel, :]
        imp_t.append(jnp.where(forced, SEL_FORCE, jnp.where(causal_blk, imp, -SEL_FORCE)))

    rank = [jnp.zeros((nsel, TQ), jnp.int32) for _ in G]
    for jp in range(nsel):
        for g in G:
            row = imp_t[g][jp:jp + 1, :]
            beats = (row > imp_t[g]) | ((row == imp_t[g]) & (jrow > jp))
            rank[g] = rank[g] + beats.astype(jnp.int32)

    q4a = []
    for g in G:
        sel_t = (rank[g] < SEL_TOPN).astype(F32)
        if nsel < LANES:
            sel_t = jnp.concatenate([sel_t, jnp.zeros((LANES - nsel, TQ), F32)], axis=0)
        bias = (pltpu.roll(sel_t.T, HEAD_DIM, 1) - 1.0) * (-NEG_INF)
        q4a.append(jnp.where(lane4 < HEAD_DIM, q4[g], jnp.concatenate([bias] * NSA_HPG, axis=0)).astype(BF16))

    o_win = _softmax_pv(win)

    col_step = lax.broadcasted_iota(jnp.int32, (R, SEL_STEP), 1)
    for n in range(S // SEL_STEP):
        @pl.when(i // (SEL_STEP // TQ) == n)
        def _(n=n):
            pieces = [[] for _ in G]
            for j in range(n + 1):
                lo, hi = j * SEL_STEP, (j + 1) * SEL_STEP
                for g in G:
                    s = _dot_nt(q4a[g], ksa_ref[0, g, lo:hi, :])
                    if j == n:
                        s = jnp.where(lo + col_step <= trow4, s, NEG_INF)
                    pieces[g].append((s, vsa_ref[0, g, lo:hi, :]))
            for g, o in enumerate(_softmax_pv(pieces)):
                osel_ref[g] = o

    for g in G:
        o_s = osel_ref[g]
        outs = []
        for hq in range(NSA_HPG):
            h = g * NSA_HPG + hq
            rows = slice(hq * TQ, (hq + 1) * TQ)
            outs.append(gate[:, 3 * h:3 * h + 1] * o_cmp[g][rows] + gate[:, 3 * h + 1:3 * h + 2] * o_s[rows]
                        + gate[:, 3 * h + 2:3 * h + 3] * o_win[g][rows])
        for pair in range(NSA_HPG // 2):
            c0 = (g * NSA_HPG + 2 * pair) * HEAD_DIM
            o_ref[:, c0:c0 + LANES] = jnp.where(lane < HEAD_DIM, outs[2 * pair],
                                                pltpu.roll(outs[2 * pair + 1], HEAD_DIM, 1))


def _nsa(q, gl, ksa, vsa, kwa, vwa, kcmp, vcmp, ovt, *, B, S):
    T = q.shape[0]
    nq = S // Q_BLOCK
    ncmp = S // CMP_STRIDE
    per_batch = lambda rows: pl.BlockSpec((1, NSA_KV, rows, LANES), lambda b, i: (b, 0, 0, 0))
    return pl.pallas_call(
        functools.partial(_nsa_kernel, S=S),
        out_shape=jax.ShapeDtypeStruct((T, HALF_W), F32),
        grid=(B, nq),
        in_specs=[pl.BlockSpec((Q_BLOCK, HALF_W), lambda b, i: (b * nq + i, 0)),
                  pl.BlockSpec((Q_BLOCK, LANES), lambda b, i: (b * nq + i, 0)),
                  per_batch(S), per_batch(S), per_batch(S), per_batch(S),
                  per_batch(ncmp), per_batch(ncmp),
                  _resident((LANES, ncmp))],
        out_specs=pl.BlockSpec((Q_BLOCK, HALF_W), lambda b, i: (b * nq + i, 0)),
        scratch_shapes=[pltpu.VMEM((NSA_KV, NSA_HPG * Q_BLOCK, LANES), F32)],
        compiler_params=_cparams(2),
        name="nsa_attn",
    )(q, gl, ksa, vsa, kwa, vwa, kcmp, vcmp, ovt)


def _pool_kernel(z_ref, pw_ref, ps_ref, o_ref):
    S = z_ref.shape[0]
    row = lax.broadcasted_iota(jnp.int32, (S, POOL_GROUP), 0)
    for gi, w in enumerate(POOL_WINDOWS):
        cols = slice(gi * POOL_GROUP, (gi + 1) * POOL_GROUP)
        zg = z_ref[:, cols]
        acc = zg
        k = 1
        while k < w:
            acc = acc + jnp.where(row >= k, pltpu.roll(acc, k, 0), 0.0)
            k *= 2
        cnt = jnp.minimum(row + 1, w).astype(F32)
        pooled = acc / cnt - zg
        y = _dot(pooled.astype(BF16), pw_ref[gi].astype(BF16))
        o_ref[:, cols] = y * ps_ref[:, cols]


def _pool(z, pw, ps, *, B, S):
    T = z.shape[0]
    return pl.pallas_call(
        _pool_kernel,
        out_shape=jax.ShapeDtypeStruct((T, HALF_W), F32),
        grid=(B,),
        in_specs=[pl.BlockSpec((S, HALF_W), lambda b: (b, 0)),
                  _resident((len(POOL_WINDOWS), POOL_GROUP, POOL_GROUP)),
                  _resident((1, HALF_W))],
        out_specs=pl.BlockSpec((S, HALF_W), lambda b: (b, 0)),
        compiler_params=_cparams(1),
        name="pool",
    )(z, pw, ps)


def _s5_disc_kernel(lr_ref, li_ref, ldt_ref, lbr_ref, lbi_ref, cr_ref, ci_ref):
    lr = lr_ref[...]
    li = li_ref[...]
    dt = jnp.exp(ldt_ref[...])
    mag = jnp.exp(lr * dt)
    ang = li * dt
    lbr = mag * jnp.cos(ang)
    lbi = mag * jnp.sin(ang)
    nr = lbr - 1.0
    den = lr * lr + li * li
    lbr_ref[...] = lbr
    lbi_ref[...] = lbi
    cr_ref[...] = (nr * lr + lbi * li) / den
    ci_ref[...] = (lbi * lr - nr * li) / den


def _s5_bbar_kernel(cr_ref, ci_ref, br_ref, bi_ref, or_ref, oi_ref):
    cr = cr_ref[...]
    ci = ci_ref[...]
    br = br_ref[...]
    bi = bi_ref[...]
    or_ref[...] = cr * br - ci * bi
    oi_ref[...] = cr * bi + ci * br


def _s5_params(lam_re, lam_im, log_dt, b_re, b_im):
    gp = jax.ShapeDtypeStruct((S5_GROUPS, S5_STATE), F32)
    lbr, lbi, cr, ci = pl.pallas_call(_s5_disc_kernel, out_shape=(gp, gp, gp, gp), name="s5_disc")(
        lam_re, lam_im, log_dt.reshape(S5_GROUPS, 1))
    flat = jax.ShapeDtypeStruct((S5_MODES, S5_GROUP_CH), F32)
    bbr, bbi = pl.pallas_call(_s5_bbar_kernel, out_shape=(flat, flat), name="s5_bbar")(
        cr.reshape(S5_MODES, 1), ci.reshape(S5_MODES, 1),
        b_re.reshape(S5_MODES, S5_GROUP_CH), b_im.reshape(S5_MODES, S5_GROUP_CH))
    return lbr, lbi, bbr, bbi


def _s5_kernel(u_ref, bsb_ref, lr_ref, li_ref, cre_ref, cim_ref, d_ref, wglu_ref, o_ref,
               xs_ref, st_ref, ys_ref, *, steps, nb):
    M = S5_MODES
    W = S5_SUPER_MODES

    @pl.when(pl.program_id(0) == 0)
    def _():
        st_ref[...] = jnp.zeros_like(st_ref)

    half = steps // 2 * nb
    halves = (slice(0, half), slice(half, 2 * half))

    def input_drive(rows, sb):
        bu = _dot(u_ref[sb, rows, :].astype(BF16), bsb_ref[sb])
        xs_ref[rows, sb * W:(sb + 1) * W] = bu[:, :W]
        xs_ref[rows, M + sb * W:M + (sb + 1) * W] = bu[:, W:]

    def readout(rows, sb):
        y = (_dot(xs_ref[rows, sb * W:(sb + 1) * W].astype(BF16), cre_ref[sb])
             - _dot(xs_ref[rows, M + sb * W:M + (sb + 1) * W].astype(BF16), cim_ref[sb]))
        return y + d_ref[:, sb * S5_SUPER_CH:(sb + 1) * S5_SUPER_CH] * u_ref[sb, rows, :]

    def scan(first, carry, side_work):
        xr, xi = carry
        lr = lr_ref[...]
        li = li_ref[...]
        every = (steps // 2) // len(side_work)
        done = []
        for t in range(steps // 2):
            if t % every == 0:
                done.append(side_work[t // every]())
            r0 = (first + t) * nb
            nr = lr * xr - li * xi + xs_ref[r0:r0 + nb, 0:M]
            ni = lr * xi + li * xr + xs_ref[r0:r0 + nb, M:2 * M]
            xs_ref[r0:r0 + nb, 0:M] = nr
            xs_ref[r0:r0 + nb, M:2 * M] = ni
            xr, xi = nr, ni
        return (xr, xi), done

    for sb in range(S5_SUPER):
        input_drive(halves[0], sb)
    carry = (st_ref[:, 0:M], st_ref[:, M:2 * M])
    carry, _ = scan(0, carry, [functools.partial(input_drive, halves[1], sb) for sb in range(S5_SUPER)])
    carry, ys0 = scan(steps // 2, carry, [functools.partial(readout, halves[0], sb) for sb in range(S5_SUPER)])
    st_ref[:, 0:M] = carry[0]
    st_ref[:, M:2 * M] = carry[1]
    ys1 = [readout(halves[1], sb) for sb in range(S5_SUPER)]

    for rows, ys in zip(halves, (ys0, ys1)):
        y = jax.nn.gelu(jnp.concatenate(ys, axis=1))
        ab = _dot(y.astype(BF16), wglu_ref[...])
        out = ab[:, :S5_WIDTH] * jax.nn.sigmoid(ab[:, S5_WIDTH:])
        for sb in range(S5_SUPER):
            ys_ref[sb, rows, :] = out[:, sb * S5_SUPER_CH:(sb + 1) * S5_SUPER_CH]
    for sb in range(S5_SUPER):
        for b in range(nb):
            o_ref[b, :, sb * S5_SUPER_CH:(sb + 1) * S5_SUPER_CH] = ys_ref[sb, pl.ds(b, steps, stride=nb), :]


def _s5(u_slabs, bsb, lr_b, li_b, cre, cim, d, wglu, *, B, S, steps=64):
    rows = steps * B
    return pl.pallas_call(
        functools.partial(_s5_kernel, steps=steps, nb=B),
        out_shape=jax.ShapeDtypeStruct((B, S, S5_WIDTH), F32),
        grid=(S // steps,),
        in_specs=[pl.BlockSpec((S5_SUPER, rows, S5_SUPER_CH), lambda i: (0, i, 0)),
                  _resident((S5_SUPER, S5_SUPER_CH, 2 * S5_SUPER_MODES)),
                  _resident((B, S5_MODES)),
                  _resident((B, S5_MODES)),
                  _resident((S5_SUPER, S5_SUPER_MODES, S5_SUPER_CH)),
                  _resident((S5_SUPER, S5_SUPER_MODES, S5_SUPER_CH)),
                  _resident((1, S5_WIDTH)),
                  _resident((S5_WIDTH, 2 * S5_WIDTH))],
        out_specs=pl.BlockSpec((B, steps, S5_WIDTH), lambda i: (0, i, 0)),
        scratch_shapes=[pltpu.VMEM((rows, 2 * S5_MODES), F32),
                        pltpu.VMEM((B, 2 * S5_MODES), F32),
                        pltpu.VMEM((S5_SUPER, rows, S5_SUPER_CH), F32)],
        compiler_params=_cparams(1),
        name="s5",
    )(u_slabs, bsb, lr_b, li_b, cre, cim, d, wglu)


def _rope_tables(S):
    inv = ROPE_THETA ** (-jnp.arange(HALF, dtype=F32) / HALF)
    ang = jnp.arange(S, dtype=F32)[:, None] * inv[None, :]
    cos, sin = jnp.cos(ang), jnp.sin(ang)
    cos_t = jnp.tile(cos, (1, LANES // HALF))
    sin_t = jnp.tile(jnp.concatenate([-sin, sin], axis=1), (1, LANES // HEAD_DIM))
    end_pos = (jnp.arange(S // CMP_STRIDE, dtype=F32) * CMP_STRIDE + (CMP_LEN - 1))[:, None] * inv[None, :]
    cosc = jnp.tile(jnp.cos(end_pos), (1, LANES // HALF))
    sinc = jnp.tile(jnp.concatenate([-jnp.sin(end_pos), jnp.sin(end_pos)], axis=1), (1, LANES // HEAD_DIM))
    return cos_t, sin_t, cosc, sinc


def _structure_constants(S):
    ncmp = S // CMP_STRIDE
    nsel = S // SEL_BLOCK
    assert nsel <= LANES - HEAD_DIM
    j = jnp.arange(LANES, dtype=jnp.int32)[:, None]
    n = jnp.arange(ncmp, dtype=jnp.int32)[None, :]
    ovt = ((n * CMP_STRIDE < (j + 1) * SEL_BLOCK) & (n * CMP_STRIDE + CMP_LEN > j * SEL_BLOCK)
           & (j < nsel) & (n < ncmp - 1)).astype(BF16)
    key_blk = (jnp.arange(S, dtype=jnp.int32) // SEL_BLOCK)[:, None]
    blk_tab = (key_blk + HEAD_DIM == jnp.arange(LANES, dtype=jnp.int32)[None, :]).astype(F32)
    c = jnp.arange(HALF_W, dtype=jnp.int32)
    avg = ((c[:, None] // HEAD_DIM == c[None, :] // HEAD_DIM).astype(F32) / HEAD_DIM).astype(BF16)
    return ovt, blk_tab, avg


def _even_mixer(x, ffn_args, g, w_in, gm_w_s, gm_b, pe, w1, w2, consts, *, B, S):
    cos_t, sin_t, cosc, sinc, ovt, blk_tab, avg = consts
    w_pad = jnp.pad(w_in, ((0, 0), (0, EVEN_IN_PAD - EVEN_IN))).astype(BF16)
    x, uv, q, gl, kc, vc, ksa, vsa, kwa, vwa = _ffn_even_proj(x, *ffn_args, g, w_pad, cos_t, sin_t, blk_tab,
                                                              B=B, S=S)
    out_a = _gmlp(uv, gm_w_s.reshape(N_HEADS * GM_CHUNK, GM_CHUNK), avg,
                  jnp.repeat(gm_b.T, HEAD_DIM, axis=1))
    eye = jnp.eye(NSA_KV, dtype=F32)
    w1_bd = jnp.einsum('wlde,gh->wlgdhe', w1.reshape(2, CMP_LEN, HEAD_DIM, HEAD_DIM), eye)
    w2_bd = jnp.einsum('wde,gh->wgdhe', w2, eye)
    kcmp, vcmp = _compress(kc, vc, jnp.tile(pe.reshape(2, CMP_LEN, 1, HEAD_DIM), (1, 1, 1, NSA_KV)),
                           w1_bd.reshape(2, CMP_LEN, LANES, LANES).astype(BF16),
                           w2_bd.reshape(2, LANES, LANES).astype(BF16), cosc, sinc, B=B, S=S)
    out_b = _nsa(q, gl, ksa, vsa, kwa, vwa, kcmp, vcmp, ovt, B=B, S=S)
    return x, out_a, out_b


def _odd_mixer(x, ffn_args, g, w_in, pool_w, pool_scale, lam_re, lam_im, b_re, b_im, c_re, c_im,
               d_skip, log_dt, w_glu, *, B, S):
    x, zc, u_slabs = _ffn_odd_proj(x.reshape(B, S, D_MODEL), *ffn_args, g, w_in.astype(BF16), B=B, S=S)
    y_c = _pool(zc.reshape(B * S, HALF_W), pool_w, pool_scale.reshape(1, HALF_W), B=B, S=S)

    lbr, lbi, bbr, bbi = _s5_params(lam_re, lam_im, log_dt, b_re, b_im)
    gps = S5_GROUPS // S5_SUPER
    eye = jnp.eye(gps, dtype=F32)
    blockdiag_in = lambda bb: jnp.einsum(
        'sgpc,gh->sgchp', bb.reshape(S5_SUPER, gps, S5_STATE, S5_GROUP_CH), eye
    ).reshape(S5_SUPER, S5_SUPER_CH, S5_SUPER_MODES)
    blockdiag_out = lambda cc: jnp.einsum(
        'sgcp,gh->sgphc', cc.reshape(S5_SUPER, gps, S5_GROUP_CH, S5_STATE), eye
    ).reshape(S5_SUPER, S5_SUPER_MODES, S5_SUPER_CH)
    bsb = jnp.concatenate([blockdiag_in(bbr), blockdiag_in(bbi)], axis=2).astype(BF16)
    bcast = lambda a: jnp.broadcast_to(a.reshape(1, S5_MODES), (B, S5_MODES))
    y_d = _s5(u_slabs, bsb, bcast(lbr), bcast(lbi),
              blockdiag_out(c_re).astype(BF16), blockdiag_out(c_im).astype(BF16),
              d_skip.reshape(1, S5_WIDTH), w_glu.astype(BF16), B=B, S=S)
    return x.reshape(B * S, D_MODEL), y_c, y_d.reshape(B * S, S5_WIDTH)


def kernel(x, norm_w, ffn_w_gate, ffn_w_up, ffn_w_down, final_norm_w, ev_w_in, ev_w_out, gm_w_s, gm_b,
           nsa_cmp_pe, nsa_cmp_w1, nsa_cmp_w2, od_w_in, od_w_out, pool_w, pool_scale, s5_lam_re,
           s5_lam_im, s5_b_re, s5_b_im, s5_c_re, s5_c_im, s5_d, s5_log_dt, s5_w_glu):
    B, S, _ = x.shape
    consts = _rope_tables(S) + _structure_constants(S)
    wg, wu, wd = ffn_w_gate.astype(BF16), ffn_w_up.astype(BF16), ffn_w_down.astype(BF16)
    ffn_w = lambda l, k: (norm_w[l, 2 * k].reshape(1, D_MODEL), wg, wu, wd, l, k)
    gf = final_norm_w.reshape(1, D_MODEL)
    xt = x.reshape(B * S, D_MODEL)
    for l in range(DEPTH):
        g = norm_w[l, 1].reshape(1, D_MODEL)
        i = l // 2
        if l % 2 == 0:
            xt, a, b = _even_mixer(xt, ffn_w(l, 0), g, ev_w_in[i], gm_w_s[i], gm_b[i], nsa_cmp_pe[i],
                                   nsa_cmp_w1[i], nsa_cmp_w2[i], consts, B=B, S=S)
            w_out = ev_w_out[i]
        else:
            xt, a, b = _odd_mixer(xt, ffn_w(l, 0), g, od_w_in[i], pool_w[i], pool_scale[i], s5_lam_re[i],
                                  s5_lam_im[i], s5_b_re[i], s5_b_im[i], s5_c_re[i], s5_c_im[i], s5_d[i],
                                  s5_log_dt[i], s5_w_glu[i], B=B, S=S)
            w_out = od_w_out[i]
        xt = _mix_ffn(a, b, w_out.astype(BF16), xt, *ffn_w(l, 1), gf, final=(l == DEPTH - 1))
    return xt.reshape(B, S, D_MODEL)
```

```python
import functools

import jax
import jax.numpy as jnp
from jax import lax
from jax.experimental import pallas as pl
from jax.experimental.pallas import tpu as pltpu

F32 = jnp.float32
BF16 = jnp.bfloat16

D_MODEL = 1024
DEPTH = 4
HEAD_DIM = 64
HALF = HEAD_DIM // 2
ROPE_THETA = 10000.0
N_HEADS = 8
GM_CHUNK = 128
NSA_KV = 2
NSA_HPG = 4
CMP_LEN = 32
CMP_STRIDE = 16
SEL_BLOCK = 64
SEL_TOPN = 8
WINDOW = 512
Q_BLOCK = 256
POOL_WINDOWS = (2, 4, 8, 16)
POOL_GROUP = 128
S5_GROUPS = 32
S5_GROUP_CH = 16
S5_STATE = 64
S5_WIDTH = 512
S5_MODES = S5_GROUPS * S5_STATE
S5_SUPER = 4
S5_SUPER_CH = S5_WIDTH // S5_SUPER
S5_SUPER_MODES = S5_MODES // S5_SUPER
FFN_DIM = 2816
RMS_EPS = 1e-6
LN_EPS = 1e-5
NEG_INF = -1e30
SEL_FORCE = 1e4
LOG2E = 1.4426950408889634

LANES = 128
HALF_W = 512
EVEN_IN = 2328
EVEN_IN_PAD = 2432
COL_U, COL_V, COL_Q = 0, 512, 1024
COL_KC, COL_VC, COL_KS, COL_VS, COL_KW, COL_VW, COL_GL = 1536, 1664, 1792, 1920, 2048, 2176, 2304
SEL_STEP = 512
WIN_PIECE = 256
ONES_LANE = HEAD_DIM

VMEM_LIMIT = 56 * 1024 * 1024


def _cparams(n_axes):
    return pltpu.CompilerParams(dimension_semantics=("arbitrary",) * n_axes,
                                vmem_limit_bytes=VMEM_LIMIT)


def _resident(shape):
    nd = len(shape)
    return pl.BlockSpec(shape, lambda *_: (0,) * nd, pipeline_mode=pl.Buffered(1))


def _rms(x, g):
    ms = jnp.mean(x * x, axis=-1, keepdims=True)
    return x * lax.rsqrt(ms + RMS_EPS) * g


def _dot(a, b):
    return jnp.dot(a, b, preferred_element_type=F32)


def _dot_nt(a, b):
    return lax.dot_general(a, b, (((1,), (1,)), ((), ())), preferred_element_type=F32)


def _split_bf16(x):
    hi = x.astype(BF16)
    return hi, (x - hi.astype(F32)).astype(BF16)


def _swiglu_half_step(x, g, wg_ref, wu_ref, wd_ref, fc):
    h = _rms(x, g).astype(BF16)
    acc = jnp.zeros_like(x)
    for c in range(FFN_DIM // fc):
        sl = slice(c * fc, (c + 1) * fc)
        a = _dot(h, wg_ref[:, sl])
        b = _dot(h, wu_ref[:, sl])
        t = (a * jax.nn.sigmoid(a) * b).astype(BF16)
        acc = acc + _dot(t, wd_ref[sl, :])
    return x + 0.5 * acc


def _mix_ffn_kernel(a_ref, b_ref, wo_ref, x_ref, g_ref, wg_ref, wu_ref, wd_ref, gf_ref, o_ref, *, fc, final):
    x = x_ref[...] + _dot(a_ref[...].astype(BF16), wo_ref[0:HALF_W, :])
    x = x + _dot(b_ref[...].astype(BF16), wo_ref[HALF_W:2 * HALF_W, :])
    x = _swiglu_half_step(x, g_ref[...], wg_ref, wu_ref, wd_ref, fc)
    o_ref[...] = _rms(x, gf_ref[...]) if final else x


def _ffn_weight_specs(l, k):
    pick = lambda rows, cols: pl.BlockSpec((None, None, rows, cols), lambda *_: (l, k, 0, 0),
                                           pipeline_mode=pl.Buffered(1))
    return [_resident((1, D_MODEL)), pick(D_MODEL, FFN_DIM), pick(D_MODEL, FFN_DIM), pick(FFN_DIM, D_MODEL)]


def _mix_ffn(a, b, wo, x, g, wg, wu, wd, l, k, gf, *, final, tm=1024, fc=256):
    T = x.shape[0]
    return pl.pallas_call(
        functools.partial(_mix_ffn_kernel, fc=fc, final=final),
        out_shape=jax.ShapeDtypeStruct((T, D_MODEL), F32),
        grid=(T // tm,),
        in_specs=[pl.BlockSpec((tm, HALF_W), lambda i: (i, 0)), pl.BlockSpec((tm, HALF_W), lambda i: (i, 0)),
                  _resident((D_MODEL, D_MODEL)),
                  pl.BlockSpec((tm, D_MODEL), lambda i: (i, 0))] + _ffn_weight_specs(l, k)
                 + [_resident((1, D_MODEL))],
        out_specs=pl.BlockSpec((tm, D_MODEL), lambda i: (i, 0)),
        compiler_params=_cparams(1),
        name="mix_ffn",
    )(a, b, wo, x, g, wg, wu, wd, gf)


def _ffn_odd_proj_kernel(x_ref, g_ref, wg_ref, wu_ref, wd_ref, gp_ref, wp_ref, xo_ref, zc_ref, u_ref,
                         *, fc, nb, ts):
    x = _swiglu_half_step(x_ref[...].reshape(nb * ts, D_MODEL), g_ref[...], wg_ref, wu_ref, wd_ref, fc)
    xo_ref[...] = x.reshape(nb, ts, D_MODEL)
    z = _dot(_rms(x, gp_ref[...]).astype(BF16), wp_ref[...])
    zc_ref[...] = z[:, :HALF_W].reshape(nb, ts, HALF_W)
    for sb in range(S5_SUPER):
        c0 = HALF_W + sb * S5_SUPER_CH
        for b in range(nb):
            u_ref[sb, pl.ds(b, ts, stride=nb), :] = z[b * ts:(b + 1) * ts, c0:c0 + S5_SUPER_CH]


def _ffn_odd_proj(x, g, wg, wu, wd, l, k, gp, wp, *, B, S, ts=128, fc=256):
    whole = lambda width: pl.BlockSpec((B, ts, width), lambda i: (0, i, 0))
    return pl.pallas_call(
        functools.partial(_ffn_odd_proj_kernel, fc=fc, nb=B, ts=ts),
        out_shape=(jax.ShapeDtypeStruct((B, S, D_MODEL), F32), jax.ShapeDtypeStruct((B, S, HALF_W), F32),
                   jax.ShapeDtypeStruct((S5_SUPER, S * B, S5_SUPER_CH), F32)),
        grid=(S // ts,),
        in_specs=[whole(D_MODEL)] + _ffn_weight_specs(l, k)
                 + [_resident((1, D_MODEL)), _resident((D_MODEL, 2 * HALF_W))],
        out_specs=(whole(D_MODEL), whole(HALF_W),
                   pl.BlockSpec((S5_SUPER, ts * B, S5_SUPER_CH), lambda i: (0, i, 0))),
        compiler_params=_cparams(1),
        name="ffn_odd_proj",
    )(x, g, wg, wu, wd, gp, wp)


def _ffn_even_proj_kernel(x_ref, g_ref, wg_ref, wu_ref, wd_ref, gp_ref, w_ref, cos_ref, sin_ref, blk_ref,
                          xo_ref, uv_ref, q_ref, gl_ref, kc_ref, vc_ref, ksa_ref, vsa_ref, kwa_ref, vwa_ref,
                          *, fc):
    x = _swiglu_half_step(x_ref[...], g_ref[...], wg_ref, wu_ref, wd_ref, fc)
    xo_ref[...] = x
    z = _dot(_rms(x, gp_ref[...]).astype(BF16), w_ref[...])
    cos = cos_ref[...]
    sin = sin_ref[...]
    lane = lax.broadcasted_iota(jnp.int32, cos.shape, 1)
    first_half = (lane % HEAD_DIM) < HALF
    low = lane < HEAD_DIM

    def slab(c0, rotary):
        xs = z[:, c0:c0 + LANES]
        if not rotary:
            return xs
        rot = jnp.where(first_half, pltpu.roll(xs, LANES - HALF, 1), pltpu.roll(xs, HALF, 1))
        return xs * cos + rot * sin

    uv_ref[...] = z[:, COL_U:COL_Q]
    for c in range(HALF_W // LANES):
        q_ref[:, c * LANES:(c + 1) * LANES] = slab(COL_Q + c * LANES, True)
    gl_ref[...] = z[:, COL_GL:COL_GL + LANES]
    kc_ref[...] = slab(COL_KC, False)
    vc_ref[...] = slab(COL_VC, False)
    ones = (lane == ONES_LANE).astype(F32)
    for c0, rotary, extra, dst in ((COL_KS, True, blk_ref[...], ksa_ref), (COL_VS, False, ones, vsa_ref),
                                   (COL_KW, True, 0.0, kwa_ref), (COL_VW, False, ones, vwa_ref)):
        xs = slab(c0, rotary)
        dst[0, 0] = jnp.where(low, xs, extra).astype(BF16)
        dst[0, 1] = jnp.where(low, pltpu.roll(xs, HEAD_DIM, 1), extra).astype(BF16)


def _ffn_even_proj(x, g, wg, wu, wd, l, k, gp, w, cos, sin, blk, *, B, S, tm=512, fc=256):
    T = x.shape[0]
    spt = S // tm
    tok = lambda width: pl.BlockSpec((tm, width), lambda i: (i, 0))
    pos = pl.BlockSpec((tm, LANES), lambda i: (i % spt, 0))
    grp = pl.BlockSpec((1, NSA_KV, tm, LANES), lambda i: (i // spt, 0, i % spt, 0))
    aug = jax.ShapeDtypeStruct((B, NSA_KV, S, LANES), BF16)
    tokens = lambda width: jax.ShapeDtypeStruct((T, width), F32)
    return pl.pallas_call(
        functools.partial(_ffn_even_proj_kernel, fc=fc),
        out_shape=(tokens(D_MODEL), tokens(2 * HALF_W), tokens(HALF_W), tokens(LANES), tokens(LANES),
                   tokens(LANES), aug, aug, aug, aug),
        grid=(T // tm,),
        in_specs=[tok(D_MODEL)] + _ffn_weight_specs(l, k)
                 + [_resident((1, D_MODEL)), _resident((D_MODEL, EVEN_IN_PAD)), pos, pos, pos],
        out_specs=(tok(D_MODEL), tok(2 * HALF_W), tok(HALF_W), tok(LANES), tok(LANES), tok(LANES),
                   grp, grp, grp, grp),
        compiler_params=_cparams(1),
        name="ffn_even_proj",
    )(x, g, wg, wu, wd, gp, w, cos, sin, blk)


def _gmlp_kernel(u_ref, v_ref, ws_ref, avg_ref, bias_ref, o_ref, *, chunks):
    rows_all = N_HEADS * GM_CHUNK
    r = lax.broadcasted_iota(jnp.int32, (rows_all, GM_CHUNK), 0) & (GM_CHUNK - 1)
    c = lax.broadcasted_iota(jnp.int32, (rows_all, GM_CHUNK), 1)
    w_all = jnp.where(c <= r, ws_ref[...], 0.0).astype(BF16)
    avg = avg_ref[...]

    def head_mean(x):
        return _dot(x.astype(BF16), avg)

    v = jax.nn.gelu(v_ref[...])
    d = v - head_mean(v)
    vn = (d * lax.rsqrt(head_mean(d * d) + LN_EPS)).astype(BF16)
    lane_head = lax.broadcasted_iota(jnp.int32, (GM_CHUNK, HALF_W), 1) // HEAD_DIM
    for ci in range(chunks):
        rows = slice(ci * GM_CHUNK, (ci + 1) * GM_CHUNK)
        res = _dot(w_all, vn[rows, :])
        s = res[0:GM_CHUNK, :]
        for h in range(1, N_HEADS):
            s = jnp.where(lane_head == h, res[h * GM_CHUNK:(h + 1) * GM_CHUNK, :], s)
        o_ref[rows, :] = jax.nn.gelu(u_ref[rows, :]) * (s + bias_ref[...])


def _gmlp(uv, ws, avg, bias, *, chunks=4):
    T = uv.shape[0]
    tm = chunks * GM_CHUNK
    return pl.pallas_call(
        functools.partial(_gmlp_kernel, chunks=chunks),
        out_shape=jax.ShapeDtypeStruct((T, HALF_W), F32),
        grid=(T // tm,),
        in_specs=[pl.BlockSpec((tm, HALF_W), lambda i: (i, 0)),
                  pl.BlockSpec((tm, HALF_W), lambda i: (i, 1)),
                  _resident((N_HEADS * GM_CHUNK, GM_CHUNK)),
                  _resident((HALF_W, HALF_W)),
                  _resident((GM_CHUNK, HALF_W))],
        out_specs=pl.BlockSpec((tm, HALF_W), lambda i: (i, 0)),
        compiler_params=_cparams(1),
        name="gmlp",
    )(uv, uv, ws, avg, bias)


def _cmp_kernel(k_ref, v_ref, pe_ref, w1_ref, w2_ref, cos_ref, sin_ref, ko_ref, vo_ref, *, nblk):
    lane = lax.broadcasted_iota(jnp.int32, (nblk, LANES), 1)
    low = lane < HEAD_DIM
    first_half = (lane % HEAD_DIM) < HALF
    for which, (src, dst) in enumerate(((k_ref, ko_ref), (v_ref, vo_ref))):
        pre = jnp.zeros((nblk, LANES), F32)
        for r in range(CMP_STRIDE):
            rows = src[pl.ds(r, nblk, stride=CMP_STRIDE), :]
            nxt = pltpu.roll(rows, nblk - 1, 0)
            pre = pre + _dot((rows + pe_ref[which, r]).astype(BF16), w1_ref[which, r])
            pre = pre + _dot((nxt + pe_ref[which, CMP_STRIDE + r]).astype(BF16), w1_ref[which, CMP_STRIDE + r])
        cmp = _dot(jax.nn.gelu(pre).astype(BF16), w2_ref[which])
        if which == 0:
            rot = jnp.where(first_half, pltpu.roll(cmp, LANES - HALF, 1), pltpu.roll(cmp, HALF, 1))
            cmp = cmp * cos_ref[...] + rot * sin_ref[...]
        dst[0, 0] = jnp.where(low, cmp, 0.0).astype(BF16)
        dst[0, 1] = jnp.where(low, pltpu.roll(cmp, HEAD_DIM, 1), 0.0).astype(BF16)


def _compress(kc, vc, pe, w1, w2, cosc, sinc, *, B, S):
    nblk = S // CMP_STRIDE
    out = jax.ShapeDtypeStruct((B, NSA_KV, nblk, LANES), BF16)
    out_spec = pl.BlockSpec((1, NSA_KV, nblk, LANES), lambda b: (b, 0, 0, 0))
    return pl.pallas_call(
        functools.partial(_cmp_kernel, nblk=nblk),
        out_shape=(out, out),
        grid=(B,),
        in_specs=[pl.BlockSpec((S, LANES), lambda b: (b, 0)),
                  pl.BlockSpec((S, LANES), lambda b: (b, 0)),
                  _resident((2, CMP_LEN, 1, LANES)),
                  _resident((2, CMP_LEN, LANES, LANES)),
                  _resident((2, LANES, LANES)),
                  _resident((nblk, LANES)),
                  _resident((nblk, LANES))],
        out_specs=(out_spec, out_spec),
        compiler_params=_cparams(1),
        name="nsa_compress",
    )(kc, vc, pe, w1, w2, cosc, sinc)


def _softmax_pv(pieces):
    ms = []
    for group in pieces:
        m = None
        for s, _ in group:
            pm = jnp.max(s, axis=-1, keepdims=True)
            m = pm if m is None else jnp.maximum(m, pm)
        ms.append(m)
    accs = [None] * len(pieces)
    for j in range(len(pieces[0])):
        for g, group in enumerate(pieces):
            s, vv = group[j]
            part = _dot(jnp.exp2(s - ms[g]).astype(BF16), vv)
            accs[g] = part if accs[g] is None else accs[g] + part
    return [acc / acc[:, ONES_LANE:ONES_LANE + 1] for acc in accs]


def _nsa_kernel(q_ref, gl_ref, ksa_ref, vsa_ref, kwa_ref, vwa_ref, kc_ref, vc_ref, ovt_ref, o_ref,
                osel_ref, *, S):
    TQ = Q_BLOCK
    R = NSA_HPG * TQ
    G = range(NSA_KV)
    ncmp = S // CMP_STRIDE
    nsel = S // SEL_BLOCK
    i = pl.program_id(1)
    t0 = i * TQ

    q = q_ref[...] * (HEAD_DIM ** -0.5 * LOG2E)
    gate = jax.nn.sigmoid(gl_ref[...])
    lane = lax.broadcasted_iota(jnp.int32, (TQ, LANES), 1)
    lane4 = lax.broadcasted_iota(jnp.int32, (R, LANES), 1)
    trow4 = t0 + (lax.broadcasted_iota(jnp.int32, (R, 1), 0) & (TQ - 1))

    def stack_heads(g):
        heads = []
        for hq in range(NSA_HPG):
            h = g * NSA_HPG + hq
            slab = q[:, (h // 2) * LANES:(h // 2 + 1) * LANES]
            if h % 2 == 1:
                slab = pltpu.roll(slab, HEAD_DIM, 1)
            heads.append(jnp.where(lane < HEAD_DIM, slab, 0.0))
        return jnp.concatenate(heads, axis=0)

    q4 = [stack_heads(g) for g in G]
    q4b = [x.astype(BF16) for x in q4]

    s_c = [_dot_nt(q4b[g], kc_ref[0, g]) for g in G]
    k0 = pl.multiple_of(jnp.maximum(i - WINDOW // TQ, 0) * TQ, TQ)
    win = [[] for _ in G]
    span = WINDOW + TQ
    for lo, hi in [(lo, min(lo + WIN_PIECE, span)) for lo in range(0, span, WIN_PIECE)]:
        kpos = k0 + lo + lax.broadcasted_iota(jnp.int32, (R, hi - lo), 1)
        allowed = (kpos <= trow4) & (trow4 - kpos < WINDOW)
        for g in G:
            s = _dot_nt(q4b[g], kwa_ref[0, g, pl.ds(k0 + lo, hi - lo), :])
            win[g].append((jnp.where(allowed, s, NEG_INF), vwa_ref[0, g, pl.ds(k0 + lo, hi - lo), :]))

    ncol = lax.broadcasted_iota(jnp.int32, (R, ncmp), 1)
    valid_c = (ncol * CMP_STRIDE + (CMP_LEN - 1) <= trow4) & (ncol < ncmp - 1)
    any_c = (trow4 >= CMP_LEN - 1).astype(F32)
    p_c = []
    for g in G:
        s = jnp.where(valid_c, s_c[g], NEG_INF)
        e = jnp.exp2(s - jnp.max(s, axis=-1, keepdims=True))
        p_c.append(e / jnp.sum(e, axis=-1, keepdims=True) * any_c)
    o_cmp = [_dot(p_c[g].astype(BF16), vc_ref[0, g]) for g in G]

    ovt = ovt_ref[...]
    jrow = lax.broadcasted_iota(jnp.int32, (nsel, TQ), 0)
    tl = t0 + lax.broadcasted_iota(jnp.int32, (nsel, TQ), 1)
    cur = jnp.right_shift(tl, 6)
    forced = (jrow == 0) | (jrow == cur) | (jrow == cur - 1)
    causal_blk = jrow * SEL_BLOCK <= tl
    imp_t = []
    for g in G:
        p = p_c[g]
        p_hi, p_lo = _split_bf16((p[0:TQ] + p[TQ:2 * TQ]) + (p[2 * TQ:3 * TQ] + p[3 * TQ:4 * TQ]))
        imp = (_dot_nt(ovt, p_hi) + _dot_nt(ovt, p_lo))[0:nsel, :]
        imp_t.append(jnp.where(forced, SEL_FORCE, jnp.where(causal_blk, imp, -SEL_FORCE)))

    SUB = 8
    sub = lax.broadcasted_iota(jnp.int32, (SUB, TQ), 0)
    chunks = [[imp_t[g][c * SUB:(c + 1) * SUB] for c in range(nsel // SUB)] for g in G]
    rank = [[jnp.zeros((SUB, TQ), F32) for _ in range(nsel // SUB)] for _ in G]
    for jp in range(nsel):
        for g in G:
            row = imp_t[g][jp:jp + 1, :]
            for c, x in enumerate(chunks[g]):
                ge = jnp.where(row >= x, 1.0, 0.0)
                gt = jnp.where(row > x, 1.0, 0.0)
                if jp < c * SUB:
                    inc = ge
                elif jp >= (c + 1) * SUB:
                    inc = gt
                else:
                    inc = jnp.where(sub > jp - c * SUB, ge, gt)
                rank[g][c] = rank[g][c] + inc

    q4a = []
    for g in G:
        sel_t = (jnp.concatenate(rank[g], axis=0) < SEL_TOPN).astype(F32)
        if nsel < LANES:
            sel_t = jnp.concatenate([sel_t, jnp.zeros((LANES - nsel, TQ), F32)], axis=0)
        bias = (pltpu.roll(sel_t.T, HEAD_DIM, 1) - 1.0) * (-NEG_INF)
        q4a.append(jnp.where(lane4 < HEAD_DIM, q4[g], jnp.concatenate([bias] * NSA_HPG, axis=0)).astype(BF16))

    o_win = _softmax_pv(win)

    col_step = lax.broadcasted_iota(jnp.int32, (R, SEL_STEP), 1)
    for n in range(S // SEL_STEP):
        @pl.when(i // (SEL_STEP // TQ) == n)
        def _(n=n):
            pieces = [[] for _ in G]
            for j in range(n + 1):
                lo, hi = j * SEL_STEP, (j + 1) * SEL_STEP
                for g in G:
                    s = _dot_nt(q4a[g], ksa_ref[0, g, lo:hi, :])
                    if j == n:
                        s = jnp.where(lo + col_step <= trow4, s, NEG_INF)
                    pieces[g].append((s, vsa_ref[0, g, lo:hi, :]))
            for g, o in enumerate(_softmax_pv(pieces)):
                osel_ref[g] = o

    for g in G:
        o_s = osel_ref[g]
        outs = []
        for hq in range(NSA_HPG):
            h = g * NSA_HPG + hq
            rows = slice(hq * TQ, (hq + 1) * TQ)
            outs.append(gate[:, 3 * h:3 * h + 1] * o_cmp[g][rows] + gate[:, 3 * h + 1:3 * h + 2] * o_s[rows]
                        + gate[:, 3 * h + 2:3 * h + 3] * o_win[g][rows])
        for pair in range(NSA_HPG // 2):
            c0 = (g * NSA_HPG + 2 * pair) * HEAD_DIM
            o_ref[:, c0:c0 + LANES] = jnp.where(lane < HEAD_DIM, outs[2 * pair],
                                                pltpu.roll(outs[2 * pair + 1], HEAD_DIM, 1))


def _nsa(q, gl, ksa, vsa, kwa, vwa, kcmp, vcmp, ovt, *, B, S):
    T = q.shape[0]
    nq = S // Q_BLOCK
    ncmp = S // CMP_STRIDE
    per_batch = lambda rows: pl.BlockSpec((1, NSA_KV, rows, LANES), lambda b, i: (b, 0, 0, 0))
    return pl.pallas_call(
        functools.partial(_nsa_kernel, S=S),
        out_shape=jax.ShapeDtypeStruct((T, HALF_W), F32),
        grid=(B, nq),
        in_specs=[pl.BlockSpec((Q_BLOCK, HALF_W), lambda b, i: (b * nq + i, 0)),
                  pl.BlockSpec((Q_BLOCK, LANES), lambda b, i: (b * nq + i, 0)),
                  per_batch(S), per_batch(S), per_batch(S), per_batch(S),
                  per_batch(ncmp), per_batch(ncmp),
                  _resident((LANES, ncmp))],
        out_specs=pl.BlockSpec((Q_BLOCK, HALF_W), lambda b, i: (b * nq + i, 0)),
        scratch_shapes=[pltpu.VMEM((NSA_KV, NSA_HPG * Q_BLOCK, LANES), F32)],
        compiler_params=_cparams(2),
        name="nsa_attn",
    )(q, gl, ksa, vsa, kwa, vwa, kcmp, vcmp, ovt)


def _pool_kernel(z_ref, pw_ref, ps_ref, o_ref):
    S = z_ref.shape[0]
    row = lax.broadcasted_iota(jnp.int32, (S, POOL_GROUP), 0)
    for gi, w in enumerate(POOL_WINDOWS):
        cols = slice(gi * POOL_GROUP, (gi + 1) * POOL_GROUP)
        zg = z_ref[:, cols]
        acc = zg
        k = 1
        while k < w:
            acc = acc + jnp.where(row >= k, pltpu.roll(acc, k, 0), 0.0)
            k *= 2
        cnt = jnp.minimum(row + 1, w).astype(F32)
        pooled = acc / cnt - zg
        y = _dot(pooled.astype(BF16), pw_ref[gi].astype(BF16))
        o_ref[:, cols] = y * ps_ref[:, cols]


def _pool(z, pw, ps, *, B, S):
    T = z.shape[0]
    return pl.pallas_call(
        _pool_kernel,
        out_shape=jax.ShapeDtypeStruct((T, HALF_W), F32),
        grid=(B,),
        in_specs=[pl.BlockSpec((S, HALF_W), lambda b: (b, 0)),
                  _resident((len(POOL_WINDOWS), POOL_GROUP, POOL_GROUP)),
                  _resident((1, HALF_W))],
        out_specs=pl.BlockSpec((S, HALF_W), lambda b: (b, 0)),
        compiler_params=_cparams(1),
        name="pool",
    )(z, pw, ps)


def _s5_disc_kernel(lr_ref, li_ref, ldt_ref, lbr_ref, lbi_ref, cr_ref, ci_ref):
    lr = lr_ref[...]
    li = li_ref[...]
    dt = jnp.exp(ldt_ref[...])
    mag = jnp.exp(lr * dt)
    ang = li * dt
    lbr = mag * jnp.cos(ang)
    lbi = mag * jnp.sin(ang)
    nr = lbr - 1.0
    den = lr * lr + li * li
    lbr_ref[...] = lbr
    lbi_ref[...] = lbi
    cr_ref[...] = (nr * lr + lbi * li) / den
    ci_ref[...] = (lbi * lr - nr * li) / den


def _s5_bbar_kernel(cr_ref, ci_ref, br_ref, bi_ref, or_ref, oi_ref):
    cr = cr_ref[...]
    ci = ci_ref[...]
    br = br_ref[...]
    bi = bi_ref[...]
    or_ref[...] = cr * br - ci * bi
    oi_ref[...] = cr * bi + ci * br


def _s5_params(lam_re, lam_im, log_dt, b_re, b_im):
    gp = jax.ShapeDtypeStruct((S5_GROUPS, S5_STATE), F32)
    lbr, lbi, cr, ci = pl.pallas_call(_s5_disc_kernel, out_shape=(gp, gp, gp, gp), name="s5_disc")(
        lam_re, lam_im, log_dt.reshape(S5_GROUPS, 1))
    flat = jax.ShapeDtypeStruct((S5_MODES, S5_GROUP_CH), F32)
    bbr, bbi = pl.pallas_call(_s5_bbar_kernel, out_shape=(flat, flat), name="s5_bbar")(
        cr.reshape(S5_MODES, 1), ci.reshape(S5_MODES, 1),
        b_re.reshape(S5_MODES, S5_GROUP_CH), b_im.reshape(S5_MODES, S5_GROUP_CH))
    return lbr, lbi, bbr, bbi


def _s5_kernel(u_ref, bsb_ref, lr_ref, li_ref, cre_ref, cim_ref, d_ref, wglu_ref, o_ref,
               xs_ref, st_ref, ys_ref, *, steps, nb):
    M = S5_MODES
    W = S5_SUPER_MODES

    @pl.when(pl.program_id(0) == 0)
    def _():
        st_ref[...] = jnp.zeros_like(st_ref)

    half = steps // 2 * nb
    halves = (slice(0, half), slice(half, 2 * half))

    def input_drive(rows, sb):
        bu = _dot(u_ref[sb, rows, :].astype(BF16), bsb_ref[sb])
        xs_ref[rows, sb * W:(sb + 1) * W] = bu[:, :W]
        xs_ref[rows, M + sb * W:M + (sb + 1) * W] = bu[:, W:]

    def readout(rows, sb):
        y = (_dot(xs_ref[rows, sb * W:(sb + 1) * W].astype(BF16), cre_ref[sb])
             - _dot(xs_ref[rows, M + sb * W:M + (sb + 1) * W].astype(BF16), cim_ref[sb]))
        return y + d_ref[:, sb * S5_SUPER_CH:(sb + 1) * S5_SUPER_CH] * u_ref[sb, rows, :]

    def scan(first, carry, side_work):
        xr, xi = carry
        lr = lr_ref[...]
        li = li_ref[...]
        every = (steps // 2) // len(side_work)
        done = []
        for t in range(steps // 2):
            if t % every == 0:
                done.append(side_work[t // every]())
            r0 = (first + t) * nb
            nr = lr * xr - li * xi + xs_ref[r0:r0 + nb, 0:M]
            ni = lr * xi + li * xr + xs_ref[r0:r0 + nb, M:2 * M]
            xs_ref[r0:r0 + nb, 0:M] = nr
            xs_ref[r0:r0 + nb, M:2 * M] = ni
            xr, xi = nr, ni
        return (xr, xi), done

    for sb in range(S5_SUPER):
        input_drive(halves[0], sb)
    carry = (st_ref[:, 0:M], st_ref[:, M:2 * M])
    carry, _ = scan(0, carry, [functools.partial(input_drive, halves[1], sb) for sb in range(S5_SUPER)])
    carry, ys0 = scan(steps // 2, carry, [functools.partial(readout, halves[0], sb) for sb in range(S5_SUPER)])
    st_ref[:, 0:M] = carry[0]
    st_ref[:, M:2 * M] = carry[1]
    ys1 = [readout(halves[1], sb) for sb in range(S5_SUPER)]

    for rows, ys in zip(halves, (ys0, ys1)):
        y = jax.nn.gelu(jnp.concatenate(ys, axis=1))
        ab = _dot(y.astype(BF16), wglu_ref[...])
        out = ab[:, :S5_WIDTH] * jax.nn.sigmoid(ab[:, S5_WIDTH:])
        for sb in range(S5_SUPER):
            ys_ref[sb, rows, :] = out[:, sb * S5_SUPER_CH:(sb + 1) * S5_SUPER_CH]
    for sb in range(S5_SUPER):
        for b in range(nb):
            o_ref[b, :, sb * S5_SUPER_CH:(sb + 1) * S5_SUPER_CH] = ys_ref[sb, pl.ds(b, steps, stride=nb), :]


def _s5(u_slabs, bsb, lr_b, li_b, cre, cim, d, wglu, *, B, S, steps=64):
    rows = steps * B
    return pl.pallas_call(
        functools.partial(_s5_kernel, steps=steps, nb=B),
        out_shape=jax.ShapeDtypeStruct((B, S, S5_WIDTH), F32),
        grid=(S // steps,),
        in_specs=[pl.BlockSpec((S5_SUPER, rows, S5_SUPER_CH), lambda i: (0, i, 0)),
                  _resident((S5_SUPER, S5_SUPER_CH, 2 * S5_SUPER_MODES)),
                  _resident((B, S5_MODES)),
                  _resident((B, S5_MODES)),
                  _resident((S5_SUPER, S5_SUPER_MODES, S5_SUPER_CH)),
                  _resident((S5_SUPER, S5_SUPER_MODES, S5_SUPER_CH)),
                  _resident((1, S5_WIDTH)),
                  _resident((S5_WIDTH, 2 * S5_WIDTH))],
        out_specs=pl.BlockSpec((B, steps, S5_WIDTH), lambda i: (0, i, 0)),
        scratch_shapes=[pltpu.VMEM((rows, 2 * S5_MODES), F32),
                        pltpu.VMEM((B, 2 * S5_MODES), F32),
                        pltpu.VMEM((S5_SUPER, rows, S5_SUPER_CH), F32)],
        compiler_params=_cparams(1),
        name="s5",
    )(u_slabs, bsb, lr_b, li_b, cre, cim, d, wglu)


def _rope_tables(S):
    inv = ROPE_THETA ** (-jnp.arange(HALF, dtype=F32) / HALF)
    ang = jnp.arange(S, dtype=F32)[:, None] * inv[None, :]
    cos, sin = jnp.cos(ang), jnp.sin(ang)
    cos_t = jnp.tile(cos, (1, LANES // HALF))
    sin_t = jnp.tile(jnp.concatenate([-sin, sin], axis=1), (1, LANES // HEAD_DIM))
    end_pos = (jnp.arange(S // CMP_STRIDE, dtype=F32) * CMP_STRIDE + (CMP_LEN - 1))[:, None] * inv[None, :]
    cosc = jnp.tile(jnp.cos(end_pos), (1, LANES // HALF))
    sinc = jnp.tile(jnp.concatenate([-jnp.sin(end_pos), jnp.sin(end_pos)], axis=1), (1, LANES // HEAD_DIM))
    return cos_t, sin_t, cosc, sinc


def _structure_constants(S):
    ncmp = S // CMP_STRIDE
    nsel = S // SEL_BLOCK
    assert nsel <= LANES - HEAD_DIM
    j = jnp.arange(LANES, dtype=jnp.int32)[:, None]
    n = jnp.arange(ncmp, dtype=jnp.int32)[None, :]
    ovt = ((n * CMP_STRIDE < (j + 1) * SEL_BLOCK) & (n * CMP_STRIDE + CMP_LEN > j * SEL_BLOCK)
           & (j < nsel) & (n < ncmp - 1)).astype(BF16)
    key_blk = (jnp.arange(S, dtype=jnp.int32) // SEL_BLOCK)[:, None]
    blk_tab = (key_blk + HEAD_DIM == jnp.arange(LANES, dtype=jnp.int32)[None, :]).astype(F32)
    c = jnp.arange(HALF_W, dtype=jnp.int32)
    avg = ((c[:, None] // HEAD_DIM == c[None, :] // HEAD_DIM).astype(F32) / HEAD_DIM).astype(BF16)
    return ovt, blk_tab, avg


def _even_mixer(x, ffn_args, g, w_in, gm_w_s, gm_b, pe, w1, w2, consts, *, B, S):
    cos_t, sin_t, cosc, sinc, ovt, blk_tab, avg = consts
    w_pad = jnp.pad(w_in, ((0, 0), (0, EVEN_IN_PAD - EVEN_IN))).astype(BF16)
    x, uv, q, gl, kc, vc, ksa, vsa, kwa, vwa = _ffn_even_proj(x, *ffn_args, g, w_pad, cos_t, sin_t, blk_tab,
                                                              B=B, S=S)
    out_a = _gmlp(uv, gm_w_s.reshape(N_HEADS * GM_CHUNK, GM_CHUNK), avg,
                  jnp.repeat(gm_b.T, HEAD_DIM, axis=1))
    eye = jnp.eye(NSA_KV, dtype=F32)
    w1_bd = jnp.einsum('wlde,gh->wlgdhe', w1.reshape(2, CMP_LEN, HEAD_DIM, HEAD_DIM), eye)
    w2_bd = jnp.einsum('wde,gh->wgdhe', w2, eye)
    kcmp, vcmp = _compress(kc, vc, jnp.tile(pe.reshape(2, CMP_LEN, 1, HEAD_DIM), (1, 1, 1, NSA_KV)),
                           w1_bd.reshape(2, CMP_LEN, LANES, LANES).astype(BF16),
                           w2_bd.reshape(2, LANES, LANES).astype(BF16), cosc, sinc, B=B, S=S)
    out_b = _nsa(q, gl, ksa, vsa, kwa, vwa, kcmp, vcmp, ovt, B=B, S=S)
    return x, out_a, out_b


def _odd_mixer(x, ffn_args, g, w_in, pool_w, pool_scale, lam_re, lam_im, b_re, b_im, c_re, c_im,
               d_skip, log_dt, w_glu, *, B, S):
    x, zc, u_slabs = _ffn_odd_proj(x.reshape(B, S, D_MODEL), *ffn_args, g, w_in.astype(BF16), B=B, S=S)
    y_c = _pool(zc.reshape(B * S, HALF_W), pool_w, pool_scale.reshape(1, HALF_W), B=B, S=S)

    lbr, lbi, bbr, bbi = _s5_params(lam_re, lam_im, log_dt, b_re, b_im)
    gps = S5_GROUPS // S5_SUPER
    eye = jnp.eye(gps, dtype=F32)
    blockdiag_in = lambda bb: jnp.einsum(
        'sgpc,gh->sgchp', bb.reshape(S5_SUPER, gps, S5_STATE, S5_GROUP_CH), eye
    ).reshape(S5_SUPER, S5_SUPER_CH, S5_SUPER_MODES)
    blockdiag_out = lambda cc: jnp.einsum(
        'sgcp,gh->sgphc', cc.reshape(S5_SUPER, gps, S5_GROUP_CH, S5_STATE), eye
    ).reshape(S5_SUPER, S5_SUPER_MODES, S5_SUPER_CH)
    bsb = jnp.concatenate([blockdiag_in(bbr), blockdiag_in(bbi)], axis=2).astype(BF16)
    bcast = lambda a: jnp.broadcast_to(a.reshape(1, S5_MODES), (B, S5_MODES))
    y_d = _s5(u_slabs, bsb, bcast(lbr), bcast(lbi),
              blockdiag_out(c_re).astype(BF16), blockdiag_out(c_im).astype(BF16),
              d_skip.reshape(1, S5_WIDTH), w_glu.astype(BF16), B=B, S=S)
    return x.reshape(B * S, D_MODEL), y_c, y_d.reshape(B * S, S5_WIDTH)


def kernel(x, norm_w, ffn_w_gate, ffn_w_up, ffn_w_down, final_norm_w, ev_w_in, ev_w_out, gm_w_s, gm_b,
           nsa_cmp_pe, nsa_cmp_w1, nsa_cmp_w2, od_w_in, od_w_out, pool_w, pool_scale, s5_lam_re,
           s5_lam_im, s5_b_re, s5_b_im, s5_c_re, s5_c_im, s5_d, s5_log_dt, s5_w_glu):
    B, S, _ = x.shape
    consts = _rope_tables(S) + _structure_constants(S)
    wg, wu, wd = ffn_w_gate.astype(BF16), ffn_w_up.astype(BF16), ffn_w_down.astype(BF16)
    ffn_w = lambda l, k: (norm_w[l, 2 * k].reshape(1, D_MODEL), wg, wu, wd, l, k)
    gf = final_norm_w.reshape(1, D_MODEL)
    xt = x.reshape(B * S, D_MODEL)
    for l in range(DEPTH):
        g = norm_w[l, 1].reshape(1, D_MODEL)
        i = l // 2
        if l % 2 == 0:
            xt, a, b = _even_mixer(xt, ffn_w(l, 0), g, ev_w_in[i], gm_w_s[i], gm_b[i], nsa_cmp_pe[i],
                                   nsa_cmp_w1[i], nsa_cmp_w2[i], consts, B=B, S=S)
            w_out = ev_w_out[i]
        else:
            xt, a, b = _odd_mixer(xt, ffn_w(l, 0), g, od_w_in[i], pool_w[i], pool_scale[i], s5_lam_re[i],
                                  s5_lam_im[i], s5_b_re[i], s5_b_im[i], s5_c_re[i], s5_c_im[i], s5_d[i],
                                  s5_log_dt[i], s5_w_glu[i], B=B, S=S)
            w_out = od_w_out[i]
        xt = _mix_ffn(a, b, w_out.astype(BF16), xt, *ffn_w(l, 1), gf, final=(l == DEPTH - 1))
    return xt.reshape(B, S, D_MODEL)
```

```python
import functools

import jax
import jax.numpy as jnp
from jax import lax
from jax.experimental import pallas as pl
from jax.experimental.pallas import tpu as pltpu

F32 = jnp.float32
BF16 = jnp.bfloat16

D_MODEL = 1024
DEPTH = 4
HEAD_DIM = 64
HALF = HEAD_DIM // 2
ROPE_THETA = 10000.0
N_HEADS = 8
GM_CHUNK = 128
NSA_KV = 2
NSA_HPG = 4
CMP_LEN = 32
CMP_STRIDE = 16
SEL_BLOCK = 64
SEL_TOPN = 8
WINDOW = 512
Q_BLOCK = 256
POOL_WINDOWS = (2, 4, 8, 16)
POOL_GROUP = 128
S5_GROUPS = 32
S5_GROUP_CH = 16
S5_STATE = 64
S5_WIDTH = 512
S5_MODES = S5_GROUPS * S5_STATE
S5_SUPER = 4
S5_SUPER_CH = S5_WIDTH // S5_SUPER
S5_SUPER_MODES = S5_MODES // S5_SUPER
FFN_DIM = 2816
RMS_EPS = 1e-6
LN_EPS = 1e-5
NEG_INF = -1e30
SEL_FORCE = 1e4
LOG2E = 1.4426950408889634

LANES = 128
HALF_W = 512
EVEN_IN = 2328
EVEN_IN_PAD = 2432
COL_U, COL_V, COL_Q = 0, 512, 1024
COL_KC, COL_VC, COL_KS, COL_VS, COL_KW, COL_VW, COL_GL = 1536, 1664, 1792, 1920, 2048, 2176, 2304
SEL_STEP = 512
WIN_PIECE = 256
ONES_LANE = HEAD_DIM

VMEM_LIMIT = 56 * 1024 * 1024
FFN_STAGE = 128


def _cparams(n_axes):
    return pltpu.CompilerParams(dimension_semantics=("arbitrary",) * n_axes,
                                vmem_limit_bytes=VMEM_LIMIT)


def _resident(shape):
    nd = len(shape)
    return pl.BlockSpec(shape, lambda *_: (0,) * nd, pipeline_mode=pl.Buffered(1))


def _rows(tm, width, S):
    spt = S // tm
    return pl.BlockSpec((None, tm, width), lambda i: (i // spt, i % spt, 0))


def _rms(x, g):
    ms = jnp.mean(x * x, axis=-1, keepdims=True)
    return x * lax.rsqrt(ms + RMS_EPS) * g


def _dot(a, b):
    return jnp.dot(a, b, preferred_element_type=F32)


def _dot_nt(a, b):
    return lax.dot_general(a, b, (((1,), (1,)), ((), ())), preferred_element_type=F32)


def _split_bf16(x):
    hi = x.astype(BF16)
    return hi, (x - hi.astype(F32)).astype(BF16)


def _swiglu_half_step(x, g, wg_ref, wu_ref, wd_ref, fc):
    h = _rms(x, g).astype(BF16)
    acc = jnp.zeros_like(x)
    for c in range(FFN_DIM // fc):
        sl = slice(c * fc, (c + 1) * fc)
        a = _dot(h, wg_ref[:, sl])
        b = _dot(h, wu_ref[:, sl])
        t = (a * jax.nn.sigmoid(a) * b).astype(BF16)
        acc = acc + _dot(t, wd_ref[sl, :])
    return x + 0.5 * acc


def _mix_ffn_kernel(a_ref, b_ref, wo_ref, x_ref, g_ref, wg_hbm, wu_hbm, wd_hbm, gf_ref, o_ref, *scratch,
                    fc, final, l, k):
    wg_ref, wu_ref, wd_ref = _stage_ffn_weights((wg_hbm, wu_hbm, wd_hbm), scratch, l, k)
    x = x_ref[...] + _dot(a_ref[...].astype(BF16), wo_ref[0:HALF_W, :])
    x = x + _dot(b_ref[...].astype(BF16), wo_ref[HALF_W:2 * HALF_W, :])
    x = _swiglu_half_step(x, g_ref[...], wg_ref, wu_ref, wd_ref, fc)
    o_ref[...] = _rms(x, gf_ref[...]) if final else x


def _ffn_weight_specs():
    hbm = pl.BlockSpec(memory_space=pl.ANY)
    return [_resident((1, D_MODEL)), hbm, hbm, hbm]


def _ffn_weight_scratch():
    return [pltpu.VMEM((D_MODEL, FFN_DIM), BF16), pltpu.VMEM((D_MODEL, FFN_DIM), BF16),
            pltpu.VMEM((FFN_DIM, D_MODEL), BF16),
            pltpu.VMEM((2, D_MODEL, FFN_STAGE), F32), pltpu.VMEM((2, D_MODEL, FFN_STAGE), F32),
            pltpu.VMEM((2, FFN_STAGE, D_MODEL), F32), pltpu.SemaphoreType.DMA((3, 2))]


def _stage_ffn_weights(hbm, scratch, l, k):
    wg_hbm, wu_hbm, wd_hbm = hbm
    wg, wu, wd, sg, su, sd, sem = scratch

    def copies(c, slot):
        span = pl.ds(c * FFN_STAGE, FFN_STAGE)
        return (pltpu.make_async_copy(wg_hbm.at[l, k, :, span], sg.at[slot], sem.at[0, slot]),
                pltpu.make_async_copy(wu_hbm.at[l, k, :, span], su.at[slot], sem.at[1, slot]),
                pltpu.make_async_copy(wd_hbm.at[l, k, span, :], sd.at[slot], sem.at[2, slot]))

    @pl.when(pl.program_id(0) == 0)
    def _():
        n_tiles = FFN_DIM // FFN_STAGE
        for cp in copies(0, 0):
            cp.start()
        for c in range(n_tiles):
            slot = c % 2
            if c + 1 < n_tiles:
                for cp in copies(c + 1, 1 - slot):
                    cp.start()
            for cp in copies(c, slot):
                cp.wait()
            sl = slice(c * FFN_STAGE, (c + 1) * FFN_STAGE)
            wg[:, sl] = sg[slot].astype(BF16)
            wu[:, sl] = su[slot].astype(BF16)
            wd[sl, :] = sd[slot].astype(BF16)
    return wg, wu, wd


def _mix_ffn(a, b, wo, x, g, wg, wu, wd, l, k, gf, *, final, tm=512, fc=256):
    B, S, _ = x.shape
    return pl.pallas_call(
        functools.partial(_mix_ffn_kernel, fc=fc, final=final, l=l, k=k),
        out_shape=jax.ShapeDtypeStruct((B, S, D_MODEL), F32),
        grid=(B * S // tm,),
        in_specs=[_rows(tm, HALF_W, S), _rows(tm, HALF_W, S), _resident((D_MODEL, D_MODEL)),
                  _rows(tm, D_MODEL, S)] + _ffn_weight_specs() + [_resident((1, D_MODEL))],
        out_specs=_rows(tm, D_MODEL, S),
        scratch_shapes=_ffn_weight_scratch(),
        compiler_params=_cparams(1),
        name="mix_ffn",
    )(a, b, wo, x, g, wg, wu, wd, gf)


def _ffn_odd_proj_kernel(x_ref, g_ref, wg_hbm, wu_hbm, wd_hbm, gp_ref, wp_ref, xo_ref, zc_ref, u_ref,
                         *scratch, fc, nb, ts, l, k):
    wg_ref, wu_ref, wd_ref = _stage_ffn_weights((wg_hbm, wu_hbm, wd_hbm), scratch, l, k)
    x = _swiglu_half_step(x_ref[...].reshape(nb * ts, D_MODEL), g_ref[...], wg_ref, wu_ref, wd_ref, fc)
    xo_ref[...] = x.reshape(nb, ts, D_MODEL)
    z = _dot(_rms(x, gp_ref[...]).astype(BF16), wp_ref[...])
    zc_ref[...] = z[:, :HALF_W].reshape(nb, ts, HALF_W)
    for sb in range(S5_SUPER):
        c0 = HALF_W + sb * S5_SUPER_CH
        for b in range(nb):
            u_ref[sb, pl.ds(b, ts, stride=nb), :] = z[b * ts:(b + 1) * ts, c0:c0 + S5_SUPER_CH]


def _ffn_odd_proj(x, g, wg, wu, wd, l, k, gp, wp, *, B, S, ts=128, fc=256):
    whole = lambda width: pl.BlockSpec((B, ts, width), lambda i: (0, i, 0))
    return pl.pallas_call(
        functools.partial(_ffn_odd_proj_kernel, fc=fc, nb=B, ts=ts, l=l, k=k),
        out_shape=(jax.ShapeDtypeStruct((B, S, D_MODEL), F32), jax.ShapeDtypeStruct((B, S, HALF_W), F32),
                   jax.ShapeDtypeStruct((S5_SUPER, S * B, S5_SUPER_CH), F32)),
        grid=(S // ts,),
        in_specs=[whole(D_MODEL)] + _ffn_weight_specs()
                 + [_resident((1, D_MODEL)), _resident((D_MODEL, 2 * HALF_W))],
        out_specs=(whole(D_MODEL), whole(HALF_W),
                   pl.BlockSpec((S5_SUPER, ts * B, S5_SUPER_CH), lambda i: (0, i, 0))),
        scratch_shapes=_ffn_weight_scratch(),
        compiler_params=_cparams(1),
        name="ffn_odd_proj",
    )(x, g, wg, wu, wd, gp, wp)


def _ffn_even_proj_kernel(x_ref, g_ref, wg_hbm, wu_hbm, wd_hbm, gp_ref, w_ref, cos_ref, sin_ref, blk_ref,
                          xo_ref, uv_ref, q_ref, gl_ref, kc_ref, vc_ref, ksa_ref, vsa_ref, kwa_ref, vwa_ref,
                          *scratch, fc, l, k):
    wg_ref, wu_ref, wd_ref = _stage_ffn_weights((wg_hbm, wu_hbm, wd_hbm), scratch, l, k)
    x = _swiglu_half_step(x_ref[...], g_ref[...], wg_ref, wu_ref, wd_ref, fc)
    xo_ref[...] = x
    z = _dot(_rms(x, gp_ref[...]).astype(BF16), w_ref[...])
    cos = cos_ref[...]
    sin = sin_ref[...]
    lane = lax.broadcasted_iota(jnp.int32, cos.shape, 1)
    first_half = (lane % HEAD_DIM) < HALF
    low = lane < HEAD_DIM

    def slab(c0, rotary):
        xs = z[:, c0:c0 + LANES]
        if not rotary:
            return xs
        rot = jnp.where(first_half, pltpu.roll(xs, LANES - HALF, 1), pltpu.roll(xs, HALF, 1))
        return xs * cos + rot * sin

    uv_ref[...] = z[:, COL_U:COL_Q]
    for c in range(HALF_W // LANES):
        q_ref[:, c * LANES:(c + 1) * LANES] = slab(COL_Q + c * LANES, True)
    gl_ref[...] = z[:, COL_GL:COL_GL + LANES]
    kc_ref[...] = slab(COL_KC, False)
    vc_ref[...] = slab(COL_VC, False)
    ones = (lane == ONES_LANE).astype(F32)
    for c0, rotary, extra, dst in ((COL_KS, True, blk_ref[...], ksa_ref), (COL_VS, False, ones, vsa_ref),
                                   (COL_KW, True, 0.0, kwa_ref), (COL_VW, False, ones, vwa_ref)):
        xs = slab(c0, rotary)
        dst[0, 0] = jnp.where(low, xs, extra).astype(BF16)
        dst[0, 1] = jnp.where(low, pltpu.roll(xs, HEAD_DIM, 1), extra).astype(BF16)


def _ffn_even_proj(x, g, wg, wu, wd, l, k, gp, w, cos, sin, blk, *, B, S, tm=512, fc=256):
    spt = S // tm
    tok = lambda width: _rows(tm, width, S)
    pos = pl.BlockSpec((tm, LANES), lambda i: (i % spt, 0))
    grp = pl.BlockSpec((1, NSA_KV, tm, LANES), lambda i: (i // spt, 0, i % spt, 0))
    aug = jax.ShapeDtypeStruct((B, NSA_KV, S, LANES), BF16)
    tokens = lambda width: jax.ShapeDtypeStruct((B, S, width), F32)
    return pl.pallas_call(
        functools.partial(_ffn_even_proj_kernel, fc=fc, l=l, k=k),
        out_shape=(tokens(D_MODEL), tokens(2 * HALF_W), tokens(HALF_W), tokens(LANES), tokens(LANES),
                   tokens(LANES), aug, aug, aug, aug),
        grid=(B * S // tm,),
        in_specs=[tok(D_MODEL)] + _ffn_weight_specs()
                 + [_resident((1, D_MODEL)), _resident((D_MODEL, EVEN_IN_PAD)), pos, pos, pos],
        out_specs=(tok(D_MODEL), tok(2 * HALF_W), tok(HALF_W), tok(LANES), tok(LANES), tok(LANES),
                   grp, grp, grp, grp),
        scratch_shapes=_ffn_weight_scratch(),
        compiler_params=_cparams(1),
        name="ffn_even_proj",
    )(x, g, wg, wu, wd, gp, w, cos, sin, blk)


def _gmlp_kernel(u_ref, v_ref, ws_ref, avg_ref, bias_ref, o_ref, *, chunks):
    rows_all = N_HEADS * GM_CHUNK
    r = lax.broadcasted_iota(jnp.int32, (rows_all, GM_CHUNK), 0) & (GM_CHUNK - 1)
    c = lax.broadcasted_iota(jnp.int32, (rows_all, GM_CHUNK), 1)
    w_all = jnp.where(c <= r, ws_ref[...], 0.0).astype(BF16)
    avg = avg_ref[...]

    def head_mean(x):
        return _dot(x.astype(BF16), avg)

    v = jax.nn.gelu(v_ref[...])
    d = v - head_mean(v)
    vn = (d * lax.rsqrt(head_mean(d * d) + LN_EPS)).astype(BF16)
    lane_head = lax.broadcasted_iota(jnp.int32, (GM_CHUNK, HALF_W), 1) // HEAD_DIM
    for ci in range(chunks):
        rows = slice(ci * GM_CHUNK, (ci + 1) * GM_CHUNK)
        res = _dot(w_all, vn[rows, :])
        s = res[0:GM_CHUNK, :]
        for h in range(1, N_HEADS):
            s = jnp.where(lane_head == h, res[h * GM_CHUNK:(h + 1) * GM_CHUNK, :], s)
        o_ref[rows, :] = jax.nn.gelu(u_ref[rows, :]) * (s + bias_ref[...])


def _gmlp(uv, ws, avg, bias, *, chunks=4):
    B, S, _ = uv.shape
    tm = chunks * GM_CHUNK
    spt = S // tm
    return pl.pallas_call(
        functools.partial(_gmlp_kernel, chunks=chunks),
        out_shape=jax.ShapeDtypeStruct((B, S, HALF_W), F32),
        grid=(B * S // tm,),
        in_specs=[pl.BlockSpec((None, tm, HALF_W), lambda i: (i // spt, i % spt, 0)),
                  pl.BlockSpec((None, tm, HALF_W), lambda i: (i // spt, i % spt, 1)),
                  _resident((N_HEADS * GM_CHUNK, GM_CHUNK)),
                  _resident((HALF_W, HALF_W)),
                  _resident((GM_CHUNK, HALF_W))],
        out_specs=_rows(tm, HALF_W, S),
        compiler_params=_cparams(1),
        name="gmlp",
    )(uv, uv, ws, avg, bias)


def _cmp_kernel(k_ref, v_ref, pe_ref, w1_ref, w2_ref, cos_ref, sin_ref, ko_ref, vo_ref, *, nblk):
    lane = lax.broadcasted_iota(jnp.int32, (nblk, LANES), 1)
    low = lane < HEAD_DIM
    first_half = (lane % HEAD_DIM) < HALF
    for which, (src, dst) in enumerate(((k_ref, ko_ref), (v_ref, vo_ref))):
        pre = jnp.zeros((nblk, LANES), F32)
        for r in range(CMP_STRIDE):
            rows = src[pl.ds(r, nblk, stride=CMP_STRIDE), :]
            nxt = pltpu.roll(rows, nblk - 1, 0)
            pre = pre + _dot((rows + pe_ref[which, r]).astype(BF16), w1_ref[which, r])
            pre = pre + _dot((nxt + pe_ref[which, CMP_STRIDE + r]).astype(BF16), w1_ref[which, CMP_STRIDE + r])
        cmp = _dot(jax.nn.gelu(pre).astype(BF16), w2_ref[which])
        if which == 0:
            rot = jnp.where(first_half, pltpu.roll(cmp, LANES - HALF, 1), pltpu.roll(cmp, HALF, 1))
            cmp = cmp * cos_ref[...] + rot * sin_ref[...]
        dst[0, 0] = jnp.where(low, cmp, 0.0).astype(BF16)
        dst[0, 1] = jnp.where(low, pltpu.roll(cmp, HEAD_DIM, 1), 0.0).astype(BF16)


def _compress(kc, vc, pe, w1, w2, cosc, sinc, *, B, S):
    nblk = S // CMP_STRIDE
    out = jax.ShapeDtypeStruct((B, NSA_KV, nblk, LANES), BF16)
    out_spec = pl.BlockSpec((1, NSA_KV, nblk, LANES), lambda b: (b, 0, 0, 0))
    return pl.pallas_call(
        functools.partial(_cmp_kernel, nblk=nblk),
        out_shape=(out, out),
        grid=(B,),
        in_specs=[pl.BlockSpec((None, S, LANES), lambda b: (b, 0, 0)),
                  pl.BlockSpec((None, S, LANES), lambda b: (b, 0, 0)),
                  _resident((2, CMP_LEN, 1, LANES)),
                  _resident((2, CMP_LEN, LANES, LANES)),
                  _resident((2, LANES, LANES)),
                  _resident((nblk, LANES)),
                  _resident((nblk, LANES))],
        out_specs=(out_spec, out_spec),
        compiler_params=_cparams(1),
        name="nsa_compress",
    )(kc, vc, pe, w1, w2, cosc, sinc)


def _softmax_pv(pieces):
    ms = []
    for group in pieces:
        m = None
        for s, _ in group:
            pm = jnp.max(s, axis=-1, keepdims=True)
            m = pm if m is None else jnp.maximum(m, pm)
        ms.append(m)
    accs = [None] * len(pieces)
    for j in range(len(pieces[0])):
        for g, group in enumerate(pieces):
            s, vv = group[j]
            part = _dot(jnp.exp2(s - ms[g]).astype(BF16), vv)
            accs[g] = part if accs[g] is None else accs[g] + part
    return [acc / acc[:, ONES_LANE:ONES_LANE + 1] for acc in accs]


def _nsa_kernel(q_ref, gl_ref, ksa_ref, vsa_ref, kwa_ref, vwa_ref, kc_ref, vc_ref, ovt_ref, o_ref,
                osel_ref, *, S):
    TQ = Q_BLOCK
    R = NSA_HPG * TQ
    G = range(NSA_KV)
    ncmp = S // CMP_STRIDE
    nsel = S // SEL_BLOCK
    i = pl.program_id(1)
    t0 = i * TQ

    q = q_ref[...] * (HEAD_DIM ** -0.5 * LOG2E)
    gate = jax.nn.sigmoid(gl_ref[...])
    lane = lax.broadcasted_iota(jnp.int32, (TQ, LANES), 1)
    lane4 = lax.broadcasted_iota(jnp.int32, (R, LANES), 1)
    trow4 = t0 + (lax.broadcasted_iota(jnp.int32, (R, 1), 0) & (TQ - 1))

    def stack_heads(g):
        heads = []
        for hq in range(NSA_HPG):
            h = g * NSA_HPG + hq
            slab = q[:, (h // 2) * LANES:(h // 2 + 1) * LANES]
            if h % 2 == 1:
                slab = pltpu.roll(slab, HEAD_DIM, 1)
            heads.append(jnp.where(lane < HEAD_DIM, slab, 0.0))
        return jnp.concatenate(heads, axis=0)

    q4 = [stack_heads(g) for g in G]
    q4b = [x.astype(BF16) for x in q4]

    s_c = [_dot_nt(q4b[g], kc_ref[0, g]) for g in G]
    k0 = pl.multiple_of(jnp.maximum(i - WINDOW // TQ, 0) * TQ, TQ)
    win = [[] for _ in G]
    span = WINDOW + TQ
    for lo, hi in [(lo, min(lo + WIN_PIECE, span)) for lo in range(0, span, WIN_PIECE)]:
        kpos = k0 + lo + lax.broadcasted_iota(jnp.int32, (R, hi - lo), 1)
        allowed = (kpos <= trow4) & (trow4 - kpos < WINDOW)
        for g in G:
            s = _dot_nt(q4b[g], kwa_ref[0, g, pl.ds(k0 + lo, hi - lo), :])
            win[g].append((jnp.where(allowed, s, NEG_INF), vwa_ref[0, g, pl.ds(k0 + lo, hi - lo), :]))

    ncol = lax.broadcasted_iota(jnp.int32, (R, ncmp), 1)
    valid_c = (ncol * CMP_STRIDE + (CMP_LEN - 1) <= trow4) & (ncol < ncmp - 1)
    any_c = (trow4 >= CMP_LEN - 1).astype(F32)
    p_c = []
    for g in G:
        s = jnp.where(valid_c, s_c[g], NEG_INF)
        e = jnp.exp2(s - jnp.max(s, axis=-1, keepdims=True))
        p_c.append(e / jnp.sum(e, axis=-1, keepdims=True) * any_c)
    o_cmp = [_dot(p_c[g].astype(BF16), vc_ref[0, g]) for g in G]

    ovt = ovt_ref[...]
    jrow = lax.broadcasted_iota(jnp.int32, (nsel, TQ), 0)
    tl = t0 + lax.broadcasted_iota(jnp.int32, (nsel, TQ), 1)
    cur = jnp.right_shift(tl, 6)
    forced = (jrow == 0) | (jrow == cur) | (jrow == cur - 1)
    causal_blk = jrow * SEL_BLOCK <= tl
    imp_t = []
    for g in G:
        p = p_c[g]
        p_hi, p_lo = _split_bf16((p[0:TQ] + p[TQ:2 * TQ]) + (p[2 * TQ:3 * TQ] + p[3 * TQ:4 * TQ]))
        imp = (_dot_nt(ovt, p_hi) + _dot_nt(ovt, p_lo))[0:nsel, :]
        imp_t.append(jnp.where(forced, SEL_FORCE, jnp.where(causal_blk, imp, -SEL_FORCE)))

    SUB = 8
    sub = lax.broadcasted_iota(jnp.int32, (SUB, TQ), 0)
    chunks = [[imp_t[g][c * SUB:(c + 1) * SUB] for c in range(nsel // SUB)] for g in G]
    rank = [[jnp.zeros((SUB, TQ), F32) for _ in range(nsel // SUB)] for _ in G]
    for jp in range(nsel):
        for g in G:
            row = imp_t[g][jp:jp + 1, :]
            for c, x in enumerate(chunks[g]):
                ge = jnp.where(row >= x, 1.0, 0.0)
                gt = jnp.where(row > x, 1.0, 0.0)
                if jp < c * SUB:
                    inc = ge
                elif jp >= (c + 1) * SUB:
                    inc = gt
                else:
                    inc = jnp.where(sub > jp - c * SUB, ge, gt)
                rank[g][c] = rank[g][c] + inc

    q4a = []
    for g in G:
        sel_t = (jnp.concatenate(rank[g], axis=0) < SEL_TOPN).astype(F32)
        if nsel < LANES:
            sel_t = jnp.concatenate([sel_t, jnp.zeros((LANES - nsel, TQ), F32)], axis=0)
        bias = (pltpu.roll(sel_t.T, HEAD_DIM, 1) - 1.0) * (-NEG_INF)
        q4a.append(jnp.where(lane4 < HEAD_DIM, q4[g], jnp.concatenate([bias] * NSA_HPG, axis=0)).astype(BF16))

    o_win = _softmax_pv(win)

    col_step = lax.broadcasted_iota(jnp.int32, (R, SEL_STEP), 1)
    for n in range(S // SEL_STEP):
        @pl.when(i // (SEL_STEP // TQ) == n)
        def _(n=n):
            pieces = [[] for _ in G]
            for j in range(n + 1):
                lo, hi = j * SEL_STEP, (j + 1) * SEL_STEP
                for g in G:
                    s = _dot_nt(q4a[g], ksa_ref[0, g, lo:hi, :])
                    if j == n:
                        s = jnp.where(lo + col_step <= trow4, s, NEG_INF)
                    pieces[g].append((s, vsa_ref[0, g, lo:hi, :]))
            for g, o in enumerate(_softmax_pv(pieces)):
                osel_ref[g] = o

    for g in G:
        o_s = osel_ref[g]
        outs = []
        for hq in range(NSA_HPG):
            h = g * NSA_HPG + hq
            rows = slice(hq * TQ, (hq + 1) * TQ)
            outs.append(gate[:, 3 * h:3 * h + 1] * o_cmp[g][rows] + gate[:, 3 * h + 1:3 * h + 2] * o_s[rows]
                        + gate[:, 3 * h + 2:3 * h + 3] * o_win[g][rows])
        for pair in range(NSA_HPG // 2):
            c0 = (g * NSA_HPG + 2 * pair) * HEAD_DIM
            o_ref[:, c0:c0 + LANES] = jnp.where(lane < HEAD_DIM, outs[2 * pair],
                                                pltpu.roll(outs[2 * pair + 1], HEAD_DIM, 1))


def _nsa(q, gl, ksa, vsa, kwa, vwa, kcmp, vcmp, ovt, *, B, S):
    nq = S // Q_BLOCK
    ncmp = S // CMP_STRIDE
    per_batch = lambda rows: pl.BlockSpec((1, NSA_KV, rows, LANES), lambda b, i: (b, 0, 0, 0))
    return pl.pallas_call(
        functools.partial(_nsa_kernel, S=S),
        out_shape=jax.ShapeDtypeStruct((B, S, HALF_W), F32),
        grid=(B, nq),
        in_specs=[pl.BlockSpec((None, Q_BLOCK, HALF_W), lambda b, i: (b, i, 0)),
                  pl.BlockSpec((None, Q_BLOCK, LANES), lambda b, i: (b, i, 0)),
                  per_batch(S), per_batch(S), per_batch(S), per_batch(S),
                  per_batch(ncmp), per_batch(ncmp),
                  _resident((LANES, ncmp))],
        out_specs=pl.BlockSpec((None, Q_BLOCK, HALF_W), lambda b, i: (b, i, 0)),
        scratch_shapes=[pltpu.VMEM((NSA_KV, NSA_HPG * Q_BLOCK, LANES), F32)],
        compiler_params=_cparams(2),
        name="nsa_attn",
    )(q, gl, ksa, vsa, kwa, vwa, kcmp, vcmp, ovt)


def _pool_kernel(z_ref, pw_ref, ps_ref, o_ref):
    S = z_ref.shape[0]
    row = lax.broadcasted_iota(jnp.int32, (S, POOL_GROUP), 0)
    for gi, w in enumerate(POOL_WINDOWS):
        cols = slice(gi * POOL_GROUP, (gi + 1) * POOL_GROUP)
        zg = z_ref[:, cols]
        acc = zg
        k = 1
        while k < w:
            acc = acc + jnp.where(row >= k, pltpu.roll(acc, k, 0), 0.0)
            k *= 2
        cnt = jnp.minimum(row + 1, w).astype(F32)
        pooled = acc / cnt - zg
        y = _dot(pooled.astype(BF16), pw_ref[gi].astype(BF16))
        o_ref[:, cols] = y * ps_ref[:, cols]


def _pool(z, pw, ps, *, B, S):
    return pl.pallas_call(
        _pool_kernel,
        out_shape=jax.ShapeDtypeStruct((B, S, HALF_W), F32),
        grid=(B,),
        in_specs=[pl.BlockSpec((None, S, HALF_W), lambda b: (b, 0, 0)),
                  _resident((len(POOL_WINDOWS), POOL_GROUP, POOL_GROUP)),
                  _resident((1, HALF_W))],
        out_specs=pl.BlockSpec((None, S, HALF_W), lambda b: (b, 0, 0)),
        compiler_params=_cparams(1),
        name="pool",
    )(z, pw, ps)


def _s5_disc_kernel(lr_ref, li_ref, ldt_ref, lbr_ref, lbi_ref, cr_ref, ci_ref):
    lr = lr_ref[...]
    li = li_ref[...]
    dt = jnp.exp(ldt_ref[...])
    mag = jnp.exp(lr * dt)
    ang = li * dt
    lbr = mag * jnp.cos(ang)
    lbi = mag * jnp.sin(ang)
    nr = lbr - 1.0
    den = lr * lr + li * li
    lbr_ref[...] = lbr
    lbi_ref[...] = lbi
    cr_ref[...] = (nr * lr + lbi * li) / den
    ci_ref[...] = (lbi * lr - nr * li) / den


def _s5_bbar_kernel(cr_ref, ci_ref, br_ref, bi_ref, or_ref, oi_ref):
    cr = cr_ref[...]
    ci = ci_ref[...]
    br = br_ref[...]
    bi = bi_ref[...]
    or_ref[...] = cr * br - ci * bi
    oi_ref[...] = cr * bi + ci * br


def _s5_params(lam_re, lam_im, log_dt, b_re, b_im):
    gp = jax.ShapeDtypeStruct((S5_GROUPS, S5_STATE), F32)
    lbr, lbi, cr, ci = pl.pallas_call(_s5_disc_kernel, out_shape=(gp, gp, gp, gp), name="s5_disc")(
        lam_re, lam_im, log_dt.reshape(S5_GROUPS, 1))
    flat = jax.ShapeDtypeStruct((S5_MODES, S5_GROUP_CH), F32)
    bbr, bbi = pl.pallas_call(_s5_bbar_kernel, out_shape=(flat, flat), name="s5_bbar")(
        cr.reshape(S5_MODES, 1), ci.reshape(S5_MODES, 1),
        b_re.reshape(S5_MODES, S5_GROUP_CH), b_im.reshape(S5_MODES, S5_GROUP_CH))
    return lbr, lbi, bbr, bbi


def _s5_kernel(u_ref, bsb_ref, lr_ref, li_ref, cre_ref, cim_ref, d_ref, wglu_ref, o_ref,
               xs_ref, st_ref, ys_ref, *, steps, nb):
    M = S5_MODES
    W = S5_SUPER_MODES

    @pl.when(pl.program_id(0) == 0)
    def _():
        st_ref[...] = jnp.zeros_like(st_ref)

    half = steps // 2 * nb
    halves = (slice(0, half), slice(half, 2 * half))

    def input_drive(rows, sb):
        bu = _dot(u_ref[sb, rows, :].astype(BF16), bsb_ref[sb])
        xs_ref[rows, sb * W:(sb + 1) * W] = bu[:, :W]
        xs_ref[rows, M + sb * W:M + (sb + 1) * W] = bu[:, W:]

    def readout(rows, sb):
        y = (_dot(xs_ref[rows, sb * W:(sb + 1) * W].astype(BF16), cre_ref[sb])
             - _dot(xs_ref[rows, M + sb * W:M + (sb + 1) * W].astype(BF16), cim_ref[sb]))
        return y + d_ref[:, sb * S5_SUPER_CH:(sb + 1) * S5_SUPER_CH] * u_ref[sb, rows, :]

    def scan(first, carry, side_work):
        xr, xi = carry
        lr = lr_ref[...]
        li = li_ref[...]
        every = (steps // 2) // len(side_work)
        done = []
        for t in range(steps // 2):
            if t % every == 0:
                done.append(side_work[t // every]())
            r0 = (first + t) * nb
            nr = lr * xr - li * xi + xs_ref[r0:r0 + nb, 0:M]
            ni = lr * xi + li * xr + xs_ref[r0:r0 + nb, M:2 * M]
            xs_ref[r0:r0 + nb, 0:M] = nr
            xs_ref[r0:r0 + nb, M:2 * M] = ni
            xr, xi = nr, ni
        return (xr, xi), done

    for sb in range(S5_SUPER):
        input_drive(halves[0], sb)
    carry = (st_ref[:, 0:M], st_ref[:, M:2 * M])
    carry, _ = scan(0, carry, [functools.partial(input_drive, halves[1], sb) for sb in range(S5_SUPER)])
    carry, ys0 = scan(steps // 2, carry, [functools.partial(readout, halves[0], sb) for sb in range(S5_SUPER)])
    st_ref[:, 0:M] = carry[0]
    st_ref[:, M:2 * M] = carry[1]
    ys1 = [readout(halves[1], sb) for sb in range(S5_SUPER)]

    for rows, ys in zip(halves, (ys0, ys1)):
        y = jax.nn.gelu(jnp.concatenate(ys, axis=1))
        ab = _dot(y.astype(BF16), wglu_ref[...])
        out = ab[:, :S5_WIDTH] * jax.nn.sigmoid(ab[:, S5_WIDTH:])
        for sb in range(S5_SUPER):
            ys_ref[sb, rows, :] = out[:, sb * S5_SUPER_CH:(sb + 1) * S5_SUPER_CH]
    for sb in range(S5_SUPER):
        for b in range(nb):
            o_ref[b, :, sb * S5_SUPER_CH:(sb + 1) * S5_SUPER_CH] = ys_ref[sb, pl.ds(b, steps, stride=nb), :]


def _s5(u_slabs, bsb, lr_b, li_b, cre, cim, d, wglu, *, B, S, steps=64):
    rows = steps * B
    return pl.pallas_call(
        functools.partial(_s5_kernel, steps=steps, nb=B),
        out_shape=jax.ShapeDtypeStruct((B, S, S5_WIDTH), F32),
        grid=(S // steps,),
        in_specs=[pl.BlockSpec((S5_SUPER, rows, S5_SUPER_CH), lambda i: (0, i, 0)),
                  _resident((S5_SUPER, S5_SUPER_CH, 2 * S5_SUPER_MODES)),
                  _resident((B, S5_MODES)),
                  _resident((B, S5_MODES)),
                  _resident((S5_SUPER, S5_SUPER_MODES, S5_SUPER_CH)),
                  _resident((S5_SUPER, S5_SUPER_MODES, S5_SUPER_CH)),
                  _resident((1, S5_WIDTH)),
                  _resident((S5_WIDTH, 2 * S5_WIDTH))],
        out_specs=pl.BlockSpec((B, steps, S5_WIDTH), lambda i: (0, i, 0)),
        scratch_shapes=[pltpu.VMEM((rows, 2 * S5_MODES), F32),
                        pltpu.VMEM((B, 2 * S5_MODES), F32),
                        pltpu.VMEM((S5_SUPER, rows, S5_SUPER_CH), F32)],
        compiler_params=_cparams(1),
        name="s5",
    )(u_slabs, bsb, lr_b, li_b, cre, cim, d, wglu)


def _rope_tables(S):
    inv = ROPE_THETA ** (-jnp.arange(HALF, dtype=F32) / HALF)
    ang = jnp.arange(S, dtype=F32)[:, None] * inv[None, :]
    cos, sin = jnp.cos(ang), jnp.sin(ang)
    cos_t = jnp.tile(cos, (1, LANES // HALF))
    sin_t = jnp.tile(jnp.concatenate([-sin, sin], axis=1), (1, LANES // HEAD_DIM))
    end_pos = (jnp.arange(S // CMP_STRIDE, dtype=F32) * CMP_STRIDE + (CMP_LEN - 1))[:, None] * inv[None, :]
    cosc = jnp.tile(jnp.cos(end_pos), (1, LANES // HALF))
    sinc = jnp.tile(jnp.concatenate([-jnp.sin(end_pos), jnp.sin(end_pos)], axis=1), (1, LANES // HEAD_DIM))
    return cos_t, sin_t, cosc, sinc


def _structure_constants(S):
    ncmp = S // CMP_STRIDE
    nsel = S // SEL_BLOCK
    assert nsel <= LANES - HEAD_DIM
    j = jnp.arange(LANES, dtype=jnp.int32)[:, None]
    n = jnp.arange(ncmp, dtype=jnp.int32)[None, :]
    ovt = ((n * CMP_STRIDE < (j + 1) * SEL_BLOCK) & (n * CMP_STRIDE + CMP_LEN > j * SEL_BLOCK)
           & (j < nsel) & (n < ncmp - 1)).astype(BF16)
    key_blk = (jnp.arange(S, dtype=jnp.int32) // SEL_BLOCK)[:, None]
    blk_tab = (key_blk + HEAD_DIM == jnp.arange(LANES, dtype=jnp.int32)[None, :]).astype(F32)
    c = jnp.arange(HALF_W, dtype=jnp.int32)
    avg = ((c[:, None] // HEAD_DIM == c[None, :] // HEAD_DIM).astype(F32) / HEAD_DIM).astype(BF16)
    return ovt, blk_tab, avg


def _even_mixer(x, ffn_args, g, w_in, gm_w_s, gm_b, pe, w1, w2, consts, *, B, S):
    cos_t, sin_t, cosc, sinc, ovt, blk_tab, avg = consts
    w_pad = jnp.pad(w_in, ((0, 0), (0, EVEN_IN_PAD - EVEN_IN))).astype(BF16)
    x, uv, q, gl, kc, vc, ksa, vsa, kwa, vwa = _ffn_even_proj(x, *ffn_args, g, w_pad, cos_t, sin_t, blk_tab,
                                                              B=B, S=S)
    out_a = _gmlp(uv, gm_w_s.reshape(N_HEADS * GM_CHUNK, GM_CHUNK), avg,
                  jnp.repeat(gm_b.T, HEAD_DIM, axis=1))
    eye = jnp.eye(NSA_KV, dtype=F32)
    w1_bd = jnp.einsum('wlde,gh->wlgdhe', w1.reshape(2, CMP_LEN, HEAD_DIM, HEAD_DIM), eye)
    w2_bd = jnp.einsum('wde,gh->wgdhe', w2, eye)
    kcmp, vcmp = _compress(kc, vc, jnp.tile(pe.reshape(2, CMP_LEN, 1, HEAD_DIM), (1, 1, 1, NSA_KV)),
                           w1_bd.reshape(2, CMP_LEN, LANES, LANES).astype(BF16),
                           w2_bd.reshape(2, LANES, LANES).astype(BF16), cosc, sinc, B=B, S=S)
    out_b = _nsa(q, gl, ksa, vsa, kwa, vwa, kcmp, vcmp, ovt, B=B, S=S)
    return x, out_a, out_b


def _odd_mixer(x, ffn_args, g, w_in, pool_w, pool_scale, lam_re, lam_im, b_re, b_im, c_re, c_im,
               d_skip, log_dt, w_glu, *, B, S):
    x, zc, u_slabs = _ffn_odd_proj(x, *ffn_args, g, w_in.astype(BF16), B=B, S=S)
    y_c = _pool(zc, pool_w, pool_scale.reshape(1, HALF_W), B=B, S=S)

    lbr, lbi, bbr, bbi = _s5_params(lam_re, lam_im, log_dt, b_re, b_im)
    gps = S5_GROUPS // S5_SUPER
    eye = jnp.eye(gps, dtype=F32)
    blockdiag_in = lambda bb: jnp.einsum(
        'sgpc,gh->sgchp', bb.reshape(S5_SUPER, gps, S5_STATE, S5_GROUP_CH), eye
    ).reshape(S5_SUPER, S5_SUPER_CH, S5_SUPER_MODES)
    blockdiag_out = lambda cc: jnp.einsum(
        'sgcp,gh->sgphc', cc.reshape(S5_SUPER, gps, S5_GROUP_CH, S5_STATE), eye
    ).reshape(S5_SUPER, S5_SUPER_MODES, S5_SUPER_CH)
    bsb = jnp.concatenate([blockdiag_in(bbr), blockdiag_in(bbi)], axis=2).astype(BF16)
    bcast = lambda a: jnp.broadcast_to(a.reshape(1, S5_MODES), (B, S5_MODES))
    y_d = _s5(u_slabs, bsb, bcast(lbr), bcast(lbi),
              blockdiag_out(c_re).astype(BF16), blockdiag_out(c_im).astype(BF16),
              d_skip.reshape(1, S5_WIDTH), w_glu.astype(BF16), B=B, S=S)
    return x, y_c, y_d


def kernel(x, norm_w, ffn_w_gate, ffn_w_up, ffn_w_down, final_norm_w, ev_w_in, ev_w_out, gm_w_s, gm_b,
           nsa_cmp_pe, nsa_cmp_w1, nsa_cmp_w2, od_w_in, od_w_out, pool_w, pool_scale, s5_lam_re,
           s5_lam_im, s5_b_re, s5_b_im, s5_c_re, s5_c_im, s5_d, s5_log_dt, s5_w_glu):
    B, S, _ = x.shape
    consts = _rope_tables(S) + _structure_constants(S)
    ffn_w = lambda l, k: (norm_w[l, 2 * k].reshape(1, D_MODEL), ffn_w_gate, ffn_w_up, ffn_w_down, l, k)
    gf = final_norm_w.reshape(1, D_MODEL)
    xt = x
    for l in range(DEPTH):
        g = norm_w[l, 1].reshape(1, D_MODEL)
        i = l // 2
        if l % 2 == 0:
            xt, a, b = _even_mixer(xt, ffn_w(l, 0), g, ev_w_in[i], gm_w_s[i], gm_b[i], nsa_cmp_pe[i],
                                   nsa_cmp_w1[i], nsa_cmp_w2[i], consts, B=B, S=S)
            w_out = ev_w_out[i]
        else:
            xt, a, b = _odd_mixer(xt, ffn_w(l, 0), g, od_w_in[i], pool_w[i], pool_scale[i], s5_lam_re[i],
                                  s5_lam_im[i], s5_b_re[i], s5_b_im[i], s5_c_re[i], s5_c_im[i], s5_d[i],
                                  s5_log_dt[i], s5_w_glu[i], B=B, S=S)
            w_out = od_w_out[i]
        xt = _mix_ffn(a, b, w_out.astype(BF16), xt, *ffn_w(l, 1), gf, final=(l == DEPTH - 1))
    return xt
```

```python
import functools

import jax
import jax.numpy as jnp
from jax import lax
from jax.experimental import pallas as pl
from jax.experimental.pallas import tpu as pltpu

F32 = jnp.float32
BF16 = jnp.bfloat16

D_MODEL = 1024
DEPTH = 4
HEAD_DIM = 64
HALF = HEAD_DIM // 2
ROPE_THETA = 10000.0
N_HEADS = 8
GM_CHUNK = 128
NSA_KV = 2
NSA_HPG = 4
CMP_LEN = 32
CMP_STRIDE = 16
SEL_BLOCK = 64
SEL_TOPN = 8
WINDOW = 512
Q_BLOCK = 256
POOL_WINDOWS = (2, 4, 8, 16)
POOL_GROUP = 128
S5_GROUPS = 32
S5_GROUP_CH = 16
S5_STATE = 64
S5_WIDTH = 512
S5_MODES = S5_GROUPS * S5_STATE
S5_SUPER = 4
S5_SUPER_CH = S5_WIDTH // S5_SUPER
S5_SUPER_MODES = S5_MODES // S5_SUPER
FFN_DIM = 2816
RMS_EPS = 1e-6
LN_EPS = 1e-5
NEG_INF = -1e30
SEL_FORCE = 1e4
LOG2E = 1.4426950408889634

LANES = 128
HALF_W = 512
EVEN_IN = 2328
EVEN_IN_PAD = 2432
COL_U, COL_V, COL_Q = 0, 512, 1024
COL_KC, COL_VC, COL_KS, COL_VS, COL_KW, COL_VW, COL_GL = 1536, 1664, 1792, 1920, 2048, 2176, 2304
SEL_STEP = 512
WIN_PIECE = 256
ONES_LANE = HEAD_DIM

VMEM_LIMIT = 56 * 1024 * 1024
FFN_STAGE = 128


def _cparams(n_axes):
    return pltpu.CompilerParams(dimension_semantics=("arbitrary",) * n_axes,
                                vmem_limit_bytes=VMEM_LIMIT)


def _resident(shape):
    nd = len(shape)
    return pl.BlockSpec(shape, lambda *_: (0,) * nd, pipeline_mode=pl.Buffered(1))


def _rows(tm, width, S):
    spt = S // tm
    return pl.BlockSpec((None, tm, width), lambda i: (i // spt, i % spt, 0))


def _rms(x, g):
    ms = jnp.mean(x * x, axis=-1, keepdims=True)
    return x * lax.rsqrt(ms + RMS_EPS) * g


def _dot(a, b):
    return jnp.dot(a, b, preferred_element_type=F32)


def _dot_nt(a, b):
    return lax.dot_general(a, b, (((1,), (1,)), ((), ())), preferred_element_type=F32)


def _split_bf16(x):
    hi = x.astype(BF16)
    return hi, (x - hi.astype(F32)).astype(BF16)


def _swiglu_half_step(x, g, wg_ref, wu_ref, wd_ref, fc, before_chunk=None):
    h = _rms(x, g).astype(BF16)
    acc = jnp.zeros_like(x)
    for c in range(FFN_DIM // fc):
        if before_chunk is not None:
            before_chunk(c)
        sl = slice(c * fc, (c + 1) * fc)
        a = _dot(h, wg_ref[:, sl])
        b = _dot(h, wu_ref[:, sl])
        t = (a * jax.nn.sigmoid(a) * b).astype(BF16)
        acc = acc + _dot(t, wd_ref[sl, :])
    return x + 0.5 * acc


def _mix_ffn_kernel(a_ref, b_ref, wo_ref, x_ref, g_ref, wg_hbm, wu_hbm, wd_hbm, gf_ref, o_ref, *scratch,
                    fc, final, l, k):
    x = x_ref[...] + _dot(a_ref[...].astype(BF16), wo_ref[0:HALF_W, :])
    x = x + _dot(b_ref[...].astype(BF16), wo_ref[HALF_W:2 * HALF_W, :])
    def store(v):
        o_ref[...] = v

    _ffn_half_step(x, g_ref[...], (wg_hbm, wu_hbm, wd_hbm), scratch, store, fc, l, k)
    if final:
        o_ref[...] = _rms(o_ref[...], gf_ref[...])


def _ffn_weight_specs():
    hbm = pl.BlockSpec(memory_space=pl.ANY)
    return [_resident((1, D_MODEL)), hbm, hbm, hbm]


def _ffn_weight_scratch():
    return [pltpu.VMEM((D_MODEL, FFN_DIM), BF16), pltpu.VMEM((D_MODEL, FFN_DIM), BF16),
            pltpu.VMEM((FFN_DIM, D_MODEL), BF16),
            pltpu.VMEM((2, D_MODEL, FFN_STAGE), F32), pltpu.VMEM((2, D_MODEL, FFN_STAGE), F32),
            pltpu.VMEM((2, FFN_STAGE, D_MODEL), F32), pltpu.SemaphoreType.DMA((3, 2))]


def _ffn_half_step(x, g, hbm, scratch, store, fc, l, k):
    wg_hbm, wu_hbm, wd_hbm = hbm
    wg, wu, wd, sg, su, sd, sem = scratch
    n_tiles = FFN_DIM // FFN_STAGE
    per_chunk = fc // FFN_STAGE

    def copies(t, slot):
        span = pl.ds(t * FFN_STAGE, FFN_STAGE)
        return (pltpu.make_async_copy(wg_hbm.at[l, k, :, span], sg.at[slot], sem.at[0, slot]),
                pltpu.make_async_copy(wu_hbm.at[l, k, :, span], su.at[slot], sem.at[1, slot]),
                pltpu.make_async_copy(wd_hbm.at[l, k, span, :], sd.at[slot], sem.at[2, slot]))

    def stage_chunk(c):
        for t in range(c * per_chunk, (c + 1) * per_chunk):
            slot = t % 2
            if t + 1 < n_tiles:
                for cp in copies(t + 1, 1 - slot):
                    cp.start()
            for cp in copies(t, slot):
                cp.wait()
            sl = slice(t * FFN_STAGE, (t + 1) * FFN_STAGE)
            wg[:, sl] = sg[slot].astype(BF16)
            wu[:, sl] = su[slot].astype(BF16)
            wd[sl, :] = sd[slot].astype(BF16)

    @pl.when(pl.program_id(0) == 0)
    def _():
        for cp in copies(0, 0):
            cp.start()
        store(_swiglu_half_step(x, g, wg, wu, wd, fc, before_chunk=stage_chunk))

    @pl.when(pl.program_id(0) > 0)
    def _():
        store(_swiglu_half_step(x, g, wg, wu, wd, fc))


def _mix_ffn(a, b, wo, x, g, wg, wu, wd, l, k, gf, *, final, tm=512, fc=256):
    B, S, _ = x.shape
    return pl.pallas_call(
        functools.partial(_mix_ffn_kernel, fc=fc, final=final, l=l, k=k),
        out_shape=jax.ShapeDtypeStruct((B, S, D_MODEL), F32),
        grid=(B * S // tm,),
        in_specs=[_rows(tm, HALF_W, S), _rows(tm, HALF_W, S), _resident((D_MODEL, D_MODEL)),
                  _rows(tm, D_MODEL, S)] + _ffn_weight_specs() + [_resident((1, D_MODEL))],
        out_specs=_rows(tm, D_MODEL, S),
        scratch_shapes=_ffn_weight_scratch(),
        compiler_params=_cparams(1),
        name="mix_ffn",
    )(a, b, wo, x, g, wg, wu, wd, gf)


def _ffn_odd_proj_kernel(x_ref, g_ref, wg_hbm, wu_hbm, wd_hbm, gp_ref, wp_ref, xo_ref, zc_ref, u_ref,
                         *scratch, fc, nb, ts, l, k):
    def store(v):
        xo_ref[...] = v.reshape(nb, ts, D_MODEL)

    _ffn_half_step(x_ref[...].reshape(nb * ts, D_MODEL), g_ref[...], (wg_hbm, wu_hbm, wd_hbm), scratch, store,
                   fc, l, k)
    x = xo_ref[...].reshape(nb * ts, D_MODEL)
    z = _dot(_rms(x, gp_ref[...]).astype(BF16), wp_ref[...])
    zc_ref[...] = z[:, :HALF_W].reshape(nb, ts, HALF_W)
    for sb in range(S5_SUPER):
        c0 = HALF_W + sb * S5_SUPER_CH
        for b in range(nb):
            u_ref[sb, pl.ds(b, ts, stride=nb), :] = z[b * ts:(b + 1) * ts, c0:c0 + S5_SUPER_CH]


def _ffn_odd_proj(x, g, wg, wu, wd, l, k, gp, wp, *, B, S, ts=64, fc=256):
    whole = lambda width: pl.BlockSpec((B, ts, width), lambda i: (0, i, 0))
    return pl.pallas_call(
        functools.partial(_ffn_odd_proj_kernel, fc=fc, nb=B, ts=ts, l=l, k=k),
        out_shape=(jax.ShapeDtypeStruct((B, S, D_MODEL), F32), jax.ShapeDtypeStruct((B, S, HALF_W), F32),
                   jax.ShapeDtypeStruct((S5_SUPER, S * B, S5_SUPER_CH), F32)),
        grid=(S // ts,),
        in_specs=[whole(D_MODEL)] + _ffn_weight_specs()
                 + [_resident((1, D_MODEL)), _resident((D_MODEL, 2 * HALF_W))],
        out_specs=(whole(D_MODEL), whole(HALF_W),
                   pl.BlockSpec((S5_SUPER, ts * B, S5_SUPER_CH), lambda i: (0, i, 0))),
        scratch_shapes=_ffn_weight_scratch(),
        compiler_params=_cparams(1),
        name="ffn_odd_proj",
    )(x, g, wg, wu, wd, gp, wp)


def _ffn_even_proj_kernel(x_ref, g_ref, wg_hbm, wu_hbm, wd_hbm, gp_ref, w_ref, cos_ref, sin_ref, blk_ref,
                          xo_ref, uv_ref, q_ref, gl_ref, kc_ref, vc_ref, ksa_ref, vsa_ref, kwa_ref, vwa_ref,
                          *scratch, fc, l, k):
    def store(v):
        xo_ref[...] = v

    _ffn_half_step(x_ref[...], g_ref[...], (wg_hbm, wu_hbm, wd_hbm), scratch, store, fc, l, k)
    x = xo_ref[...]
    z = _dot(_rms(x, gp_ref[...]).astype(BF16), w_ref[...])
    cos = cos_ref[...]
    sin = sin_ref[...]
    lane = lax.broadcasted_iota(jnp.int32, cos.shape, 1)
    first_half = (lane % HEAD_DIM) < HALF
    low = lane < HEAD_DIM

    def slab(c0, rotary):
        xs = z[:, c0:c0 + LANES]
        if not rotary:
            return xs
        rot = jnp.where(first_half, pltpu.roll(xs, LANES - HALF, 1), pltpu.roll(xs, HALF, 1))
        return xs * cos + rot * sin

    uv_ref[...] = z[:, COL_U:COL_Q]
    for c in range(HALF_W // LANES):
        q_ref[:, c * LANES:(c + 1) * LANES] = slab(COL_Q + c * LANES, True)
    gl_ref[...] = z[:, COL_GL:COL_GL + LANES]
    kc_ref[...] = slab(COL_KC, False)
    vc_ref[...] = slab(COL_VC, False)
    ones = (lane == ONES_LANE).astype(F32)
    for c0, rotary, extra, dst in ((COL_KS, True, blk_ref[...], ksa_ref), (COL_VS, False, ones, vsa_ref),
                                   (COL_KW, True, 0.0, kwa_ref), (COL_VW, False, ones, vwa_ref)):
        xs = slab(c0, rotary)
        dst[0, 0] = jnp.where(low, xs, extra).astype(BF16)
        dst[0, 1] = jnp.where(low, pltpu.roll(xs, HEAD_DIM, 1), extra).astype(BF16)


def _ffn_even_proj(x, g, wg, wu, wd, l, k, gp, w, cos, sin, blk, *, B, S, tm=512, fc=256):
    spt = S // tm
    tok = lambda width: _rows(tm, width, S)
    pos = pl.BlockSpec((tm, LANES), lambda i: (i % spt, 0))
    grp = pl.BlockSpec((1, NSA_KV, tm, LANES), lambda i: (i // spt, 0, i % spt, 0))
    aug = jax.ShapeDtypeStruct((B, NSA_KV, S, LANES), BF16)
    tokens = lambda width: jax.ShapeDtypeStruct((B, S, width), F32)
    return pl.pallas_call(
        functools.partial(_ffn_even_proj_kernel, fc=fc, l=l, k=k),
        out_shape=(tokens(D_MODEL), tokens(2 * HALF_W), tokens(HALF_W), tokens(LANES), tokens(LANES),
                   tokens(LANES), aug, aug, aug, aug),
        grid=(B * S // tm,),
        in_specs=[tok(D_MODEL)] + _ffn_weight_specs()
                 + [_resident((1, D_MODEL)), _resident((D_MODEL, EVEN_IN_PAD)), pos, pos, pos],
        out_specs=(tok(D_MODEL), tok(2 * HALF_W), tok(HALF_W), tok(LANES), tok(LANES), tok(LANES),
                   grp, grp, grp, grp),
        scratch_shapes=_ffn_weight_scratch(),
        compiler_params=_cparams(1),
        name="ffn_even_proj",
    )(x, g, wg, wu, wd, gp, w, cos, sin, blk)


def _gmlp_kernel(u_ref, v_ref, ws_ref, avg_ref, bias_ref, o_ref, *, chunks):
    rows_all = N_HEADS * GM_CHUNK
    r = lax.broadcasted_iota(jnp.int32, (rows_all, GM_CHUNK), 0) & (GM_CHUNK - 1)
    c = lax.broadcasted_iota(jnp.int32, (rows_all, GM_CHUNK), 1)
    w_all = jnp.where(c <= r, ws_ref[...], 0.0).astype(BF16)
    avg = avg_ref[...]

    def head_mean(x):
        return _dot(x.astype(BF16), avg)

    v = jax.nn.gelu(v_ref[...])
    d = v - head_mean(v)
    vn = (d * lax.rsqrt(head_mean(d * d) + LN_EPS)).astype(BF16)
    lane_head = lax.broadcasted_iota(jnp.int32, (GM_CHUNK, HALF_W), 1) // HEAD_DIM
    for ci in range(chunks):
        rows = slice(ci * GM_CHUNK, (ci + 1) * GM_CHUNK)
        res = _dot(w_all, vn[rows, :])
        s = res[0:GM_CHUNK, :]
        for h in range(1, N_HEADS):
            s = jnp.where(lane_head == h, res[h * GM_CHUNK:(h + 1) * GM_CHUNK, :], s)
        o_ref[rows, :] = jax.nn.gelu(u_ref[rows, :]) * (s + bias_ref[...])


def _gmlp(uv, ws, avg, bias, *, chunks=4):
    B, S, _ = uv.shape
    tm = chunks * GM_CHUNK
    spt = S // tm
    return pl.pallas_call(
        functools.partial(_gmlp_kernel, chunks=chunks),
        out_shape=jax.ShapeDtypeStruct((B, S, HALF_W), F32),
        grid=(B * S // tm,),
        in_specs=[pl.BlockSpec((None, tm, HALF_W), lambda i: (i // spt, i % spt, 0)),
                  pl.BlockSpec((None, tm, HALF_W), lambda i: (i // spt, i % spt, 1)),
                  _resident((N_HEADS * GM_CHUNK, GM_CHUNK)),
                  _resident((HALF_W, HALF_W)),
                  _resident((GM_CHUNK, HALF_W))],
        out_specs=_rows(tm, HALF_W, S),
        compiler_params=_cparams(1),
        name="gmlp",
    )(uv, uv, ws, avg, bias)


def _cmp_kernel(k_ref, v_ref, pe_ref, w1_ref, w2_ref, cos_ref, sin_ref, ko_ref, vo_ref, *, nblk):
    lane = lax.broadcasted_iota(jnp.int32, (nblk, LANES), 1)
    low = lane < HEAD_DIM
    first_half = (lane % HEAD_DIM) < HALF

    def both_groups(w):
        z = jnp.zeros_like(w)
        return jnp.concatenate([jnp.concatenate([w, z], axis=1), jnp.concatenate([z, w], axis=1)],
                               axis=0).astype(BF16)

    for which, (src, dst) in enumerate(((k_ref, ko_ref), (v_ref, vo_ref))):
        pre = jnp.zeros((nblk, LANES), F32)
        for r in range(CMP_STRIDE):
            rows = src[pl.ds(r, nblk, stride=CMP_STRIDE), :]
            nxt = pltpu.roll(rows, nblk - 1, 0)
            pre = pre + _dot((rows + pe_ref[which, r]).astype(BF16), both_groups(w1_ref[which, r]))
            pre = pre + _dot((nxt + pe_ref[which, CMP_STRIDE + r]).astype(BF16),
                             both_groups(w1_ref[which, CMP_STRIDE + r]))
        cmp = _dot(jax.nn.gelu(pre).astype(BF16), both_groups(w2_ref[which]))
        if which == 0:
            rot = jnp.where(first_half, pltpu.roll(cmp, LANES - HALF, 1), pltpu.roll(cmp, HALF, 1))
            cmp = cmp * cos_ref[...] + rot * sin_ref[...]
        dst[0, 0] = jnp.where(low, cmp, 0.0).astype(BF16)
        dst[0, 1] = jnp.where(low, pltpu.roll(cmp, HEAD_DIM, 1), 0.0).astype(BF16)


def _compress(kc, vc, pe, w1, w2, cosc, sinc, *, B, S):
    nblk = S // CMP_STRIDE
    out = jax.ShapeDtypeStruct((B, NSA_KV, nblk, LANES), BF16)
    out_spec = pl.BlockSpec((1, NSA_KV, nblk, LANES), lambda b: (b, 0, 0, 0))
    return pl.pallas_call(
        functools.partial(_cmp_kernel, nblk=nblk),
        out_shape=(out, out),
        grid=(B,),
        in_specs=[pl.BlockSpec((None, S, LANES), lambda b: (b, 0, 0)),
                  pl.BlockSpec((None, S, LANES), lambda b: (b, 0, 0)),
                  _resident((2, CMP_LEN, 1, LANES)),
                  _resident((2, CMP_LEN, HEAD_DIM, HEAD_DIM)),
                  _resident((2, HEAD_DIM, HEAD_DIM)),
                  _resident((nblk, LANES)),
                  _resident((nblk, LANES))],
        out_specs=(out_spec, out_spec),
        compiler_params=_cparams(1),
        name="nsa_compress",
    )(kc, vc, pe, w1, w2, cosc, sinc)


def _softmax_pv(pieces):
    ms = []
    for group in pieces:
        m = None
        for s, _ in group:
            pm = jnp.max(s, axis=-1, keepdims=True)
            m = pm if m is None else jnp.maximum(m, pm)
        ms.append(m)
    accs = [None] * len(pieces)
    for j in range(len(pieces[0])):
        for g, group in enumerate(pieces):
            s, vv = group[j]
            part = _dot(jnp.exp2(s - ms[g]).astype(BF16), vv)
            accs[g] = part if accs[g] is None else accs[g] + part
    return [acc / acc[:, ONES_LANE:ONES_LANE + 1] for acc in accs]


def _nsa_kernel(q_ref, gl_ref, ksa_ref, vsa_ref, kwa_ref, vwa_ref, kc_ref, vc_ref, ovt_ref, o_ref,
                osel_ref, *, S):
    TQ = Q_BLOCK
    R = NSA_HPG * TQ
    G = range(NSA_KV)
    ncmp = S // CMP_STRIDE
    nsel = S // SEL_BLOCK
    i = pl.program_id(1)
    t0 = i * TQ

    q = q_ref[...] * (HEAD_DIM ** -0.5 * LOG2E)
    gate = jax.nn.sigmoid(gl_ref[...])
    lane = lax.broadcasted_iota(jnp.int32, (TQ, LANES), 1)
    lane4 = lax.broadcasted_iota(jnp.int32, (R, LANES), 1)
    trow4 = t0 + (lax.broadcasted_iota(jnp.int32, (R, 1), 0) & (TQ - 1))

    def stack_heads(g):
        heads = []
        for hq in range(NSA_HPG):
            h = g * NSA_HPG + hq
            slab = q[:, (h // 2) * LANES:(h // 2 + 1) * LANES]
            if h % 2 == 1:
                slab = pltpu.roll(slab, HEAD_DIM, 1)
            heads.append(jnp.where(lane < HEAD_DIM, slab, 0.0))
        return jnp.concatenate(heads, axis=0)

    q4 = [stack_heads(g) for g in G]
    q4b = [x.astype(BF16) for x in q4]

    s_c = [_dot_nt(q4b[g], kc_ref[0, g]) for g in G]
    k0 = pl.multiple_of(jnp.maximum(i - WINDOW // TQ, 0) * TQ, TQ)
    win = [[] for _ in G]
    span = WINDOW + TQ
    for lo, hi in [(lo, min(lo + WIN_PIECE, span)) for lo in range(0, span, WIN_PIECE)]:
        kpos = k0 + lo + lax.broadcasted_iota(jnp.int32, (R, hi - lo), 1)
        allowed = (kpos <= trow4) & (trow4 - kpos < WINDOW)
        for g in G:
            s = _dot_nt(q4b[g], kwa_ref[0, g, pl.ds(k0 + lo, hi - lo), :])
            win[g].append((jnp.where(allowed, s, NEG_INF), vwa_ref[0, g, pl.ds(k0 + lo, hi - lo), :]))

    ncol = lax.broadcasted_iota(jnp.int32, (R, ncmp), 1)
    valid_c = (ncol * CMP_STRIDE + (CMP_LEN - 1) <= trow4) & (ncol < ncmp - 1)
    any_c = (trow4 >= CMP_LEN - 1).astype(F32)
    p_c = []
    for g in G:
        s = jnp.where(valid_c, s_c[g], NEG_INF)
        e = jnp.exp2(s - jnp.max(s, axis=-1, keepdims=True))
        p_c.append(e / jnp.sum(e, axis=-1, keepdims=True) * any_c)
    o_cmp = [_dot(p_c[g].astype(BF16), vc_ref[0, g]) for g in G]

    ovt = ovt_ref[...]
    jrow = lax.broadcasted_iota(jnp.int32, (nsel, TQ), 0)
    tl = t0 + lax.broadcasted_iota(jnp.int32, (nsel, TQ), 1)
    cur = jnp.right_shift(tl, 6)
    forced = (jrow == 0) | (jrow == cur) | (jrow == cur - 1)
    causal_blk = jrow * SEL_BLOCK <= tl
    imp_t = []
    for g in G:
        p = p_c[g]
        p_hi, p_lo = _split_bf16((p[0:TQ] + p[TQ:2 * TQ]) + (p[2 * TQ:3 * TQ] + p[3 * TQ:4 * TQ]))
        imp = (_dot_nt(ovt, p_hi) + _dot_nt(ovt, p_lo))[0:nsel, :]
        imp_t.append(jnp.where(forced, SEL_FORCE, jnp.where(causal_blk, imp, -SEL_FORCE)))

    SUB = 8
    sub = lax.broadcasted_iota(jnp.int32, (SUB, TQ), 0)
    chunks = [[imp_t[g][c * SUB:(c + 1) * SUB] for c in range(nsel // SUB)] for g in G]
    rank = [[jnp.zeros((SUB, TQ), F32) for _ in range(nsel // SUB)] for _ in G]
    for jp in range(nsel):
        for g in G:
            row = imp_t[g][jp:jp + 1, :]
            for c, x in enumerate(chunks[g]):
                ge = jnp.where(row >= x, 1.0, 0.0)
                gt = jnp.where(row > x, 1.0, 0.0)
                if jp < c * SUB:
                    inc = ge
                elif jp >= (c + 1) * SUB:
                    inc = gt
                else:
                    inc = jnp.where(sub > jp - c * SUB, ge, gt)
                rank[g][c] = rank[g][c] + inc

    q4a = []
    for g in G:
        sel_t = (jnp.concatenate(rank[g], axis=0) < SEL_TOPN).astype(F32)
        if nsel < LANES:
            sel_t = jnp.concatenate([sel_t, jnp.zeros((LANES - nsel, TQ), F32)], axis=0)
        bias = (pltpu.roll(sel_t.T, HEAD_DIM, 1) - 1.0) * (-NEG_INF)
        q4a.append(jnp.where(lane4 < HEAD_DIM, q4[g], jnp.concatenate([bias] * NSA_HPG, axis=0)).astype(BF16))

    o_win = _softmax_pv(win)

    col_step = lax.broadcasted_iota(jnp.int32, (R, SEL_STEP), 1)
    for n in range(S // SEL_STEP):
        @pl.when(i // (SEL_STEP // TQ) == n)
        def _(n=n):
            pieces = [[] for _ in G]
            for j in range(n + 1):
                lo, hi = j * SEL_STEP, (j + 1) * SEL_STEP
                for g in G:
                    s = _dot_nt(q4a[g], ksa_ref[0, g, lo:hi, :])
                    if j == n:
                        s = jnp.where(lo + col_step <= trow4, s, NEG_INF)
                    pieces[g].append((s, vsa_ref[0, g, lo:hi, :]))
            for g, o in enumerate(_softmax_pv(pieces)):
                osel_ref[g] = o

    for g in G:
        o_s = osel_ref[g]
        outs = []
        for hq in range(NSA_HPG):
            h = g * NSA_HPG + hq
            rows = slice(hq * TQ, (hq + 1) * TQ)
            outs.append(gate[:, 3 * h:3 * h + 1] * o_cmp[g][rows] + gate[:, 3 * h + 1:3 * h + 2] * o_s[rows]
                        + gate[:, 3 * h + 2:3 * h + 3] * o_win[g][rows])
        for pair in range(NSA_HPG // 2):
            c0 = (g * NSA_HPG + 2 * pair) * HEAD_DIM
            o_ref[:, c0:c0 + LANES] = jnp.where(lane < HEAD_DIM, outs[2 * pair],
                                                pltpu.roll(outs[2 * pair + 1], HEAD_DIM, 1))


def _nsa(q, gl, ksa, vsa, kwa, vwa, kcmp, vcmp, ovt, *, B, S):
    nq = S // Q_BLOCK
    ncmp = S // CMP_STRIDE
    per_batch = lambda rows: pl.BlockSpec((1, NSA_KV, rows, LANES), lambda b, i: (b, 0, 0, 0))
    return pl.pallas_call(
        functools.partial(_nsa_kernel, S=S),
        out_shape=jax.ShapeDtypeStruct((B, S, HALF_W), F32),
        grid=(B, nq),
        in_specs=[pl.BlockSpec((None, Q_BLOCK, HALF_W), lambda b, i: (b, i, 0)),
                  pl.BlockSpec((None, Q_BLOCK, LANES), lambda b, i: (b, i, 0)),
                  per_batch(S), per_batch(S), per_batch(S), per_batch(S),
                  per_batch(ncmp), per_batch(ncmp),
                  _resident((LANES, ncmp))],
        out_specs=pl.BlockSpec((None, Q_BLOCK, HALF_W), lambda b, i: (b, i, 0)),
        scratch_shapes=[pltpu.VMEM((NSA_KV, NSA_HPG * Q_BLOCK, LANES), F32)],
        compiler_params=_cparams(2),
        name="nsa_attn",
    )(q, gl, ksa, vsa, kwa, vwa, kcmp, vcmp, ovt)


def _pool_kernel(z_ref, pw_ref, ps_ref, o_ref):
    S = z_ref.shape[0]
    row = lax.broadcasted_iota(jnp.int32, (S, POOL_GROUP), 0)
    for gi, w in enumerate(POOL_WINDOWS):
        cols = slice(gi * POOL_GROUP, (gi + 1) * POOL_GROUP)
        zg = z_ref[:, cols]
        acc = zg
        k = 1
        while k < w:
            acc = acc + jnp.where(row >= k, pltpu.roll(acc, k, 0), 0.0)
            k *= 2
        cnt = jnp.minimum(row + 1, w).astype(F32)
        pooled = acc / cnt - zg
        y = _dot(pooled.astype(BF16), pw_ref[gi].astype(BF16))
        o_ref[:, cols] = y * ps_ref[:, cols]


def _pool(z, pw, ps, *, B, S):
    return pl.pallas_call(
        _pool_kernel,
        out_shape=jax.ShapeDtypeStruct((B, S, HALF_W), F32),
        grid=(B,),
        in_specs=[pl.BlockSpec((None, S, HALF_W), lambda b: (b, 0, 0)),
                  _resident((len(POOL_WINDOWS), POOL_GROUP, POOL_GROUP)),
                  _resident((1, HALF_W))],
        out_specs=pl.BlockSpec((None, S, HALF_W), lambda b: (b, 0, 0)),
        compiler_params=_cparams(1),
        name="pool",
    )(z, pw, ps)


def _s5_disc_kernel(lr_ref, li_ref, ldt_ref, lbr_ref, lbi_ref, cr_ref, ci_ref):
    lr = lr_ref[...]
    li = li_ref[...]
    dt = jnp.exp(ldt_ref[...])
    mag = jnp.exp(lr * dt)
    ang = li * dt
    lbr = mag * jnp.cos(ang)
    lbi = mag * jnp.sin(ang)
    nr = lbr - 1.0
    den = lr * lr + li * li
    lbr_ref[...] = lbr
    lbi_ref[...] = lbi
    cr_ref[...] = (nr * lr + lbi * li) / den
    ci_ref[...] = (lbi * lr - nr * li) / den


def _s5_bbar_kernel(cr_ref, ci_ref, br_ref, bi_ref, or_ref, oi_ref):
    cr = cr_ref[...]
    ci = ci_ref[...]
    br = br_ref[...]
    bi = bi_ref[...]
    or_ref[...] = cr * br - ci * bi
    oi_ref[...] = cr * bi + ci * br


def _s5_params(lam_re, lam_im, log_dt, b_re, b_im):
    gp = jax.ShapeDtypeStruct((S5_GROUPS, S5_STATE), F32)
    lbr, lbi, cr, ci = pl.pallas_call(_s5_disc_kernel, out_shape=(gp, gp, gp, gp), name="s5_disc")(
        lam_re, lam_im, log_dt.reshape(S5_GROUPS, 1))
    flat = jax.ShapeDtypeStruct((S5_MODES, S5_GROUP_CH), F32)
    bbr, bbi = pl.pallas_call(_s5_bbar_kernel, out_shape=(flat, flat), name="s5_bbar")(
        cr.reshape(S5_MODES, 1), ci.reshape(S5_MODES, 1),
        b_re.reshape(S5_MODES, S5_GROUP_CH), b_im.reshape(S5_MODES, S5_GROUP_CH))
    return lbr, lbi, bbr, bbi


def _s5_kernel(u_ref, bsb_ref, lr_ref, li_ref, cre_ref, cim_ref, d_ref, wglu_ref, o_ref,
               xs_ref, st_ref, ys_ref, *, steps, nb):
    M = S5_MODES
    W = S5_SUPER_MODES

    @pl.when(pl.program_id(0) == 0)
    def _():
        st_ref[...] = jnp.zeros_like(st_ref)

    half = steps // 2 * nb
    halves = (slice(0, half), slice(half, 2 * half))

    def input_drive(rows, sb):
        bu = _dot(u_ref[sb, rows, :].astype(BF16), bsb_ref[sb])
        xs_ref[rows, sb * W:(sb + 1) * W] = bu[:, :W]
        xs_ref[rows, M + sb * W:M + (sb + 1) * W] = bu[:, W:]

    def readout(rows, sb):
        y = (_dot(xs_ref[rows, sb * W:(sb + 1) * W].astype(BF16), cre_ref[sb])
             - _dot(xs_ref[rows, M + sb * W:M + (sb + 1) * W].astype(BF16), cim_ref[sb]))
        return y + d_ref[:, sb * S5_SUPER_CH:(sb + 1) * S5_SUPER_CH] * u_ref[sb, rows, :]

    def scan(first, carry, side_work):
        xr, xi = carry
        lr = lr_ref[...]
        li = li_ref[...]
        every = (steps // 2) // len(side_work)
        done = []
        for t in range(steps // 2):
            if t % every == 0:
                done.append(side_work[t // every]())
            r0 = (first + t) * nb
            nr = lr * xr - li * xi + xs_ref[r0:r0 + nb, 0:M]
            ni = lr * xi + li * xr + xs_ref[r0:r0 + nb, M:2 * M]
            xs_ref[r0:r0 + nb, 0:M] = nr
            xs_ref[r0:r0 + nb, M:2 * M] = ni
            xr, xi = nr, ni
        return (xr, xi), done

    for sb in range(S5_SUPER):
        input_drive(halves[0], sb)
    carry = (st_ref[:, 0:M], st_ref[:, M:2 * M])
    carry, _ = scan(0, carry, [functools.partial(input_drive, halves[1], sb) for sb in range(S5_SUPER)])
    carry, ys0 = scan(steps // 2, carry, [functools.partial(readout, halves[0], sb) for sb in range(S5_SUPER)])
    st_ref[:, 0:M] = carry[0]
    st_ref[:, M:2 * M] = carry[1]
    ys1 = [readout(halves[1], sb) for sb in range(S5_SUPER)]

    for rows, ys in zip(halves, (ys0, ys1)):
        y = jax.nn.gelu(jnp.concatenate(ys, axis=1))
        ab = _dot(y.astype(BF16), wglu_ref[...])
        out = ab[:, :S5_WIDTH] * jax.nn.sigmoid(ab[:, S5_WIDTH:])
        for sb in range(S5_SUPER):
            ys_ref[sb, rows, :] = out[:, sb * S5_SUPER_CH:(sb + 1) * S5_SUPER_CH]
    for sb in range(S5_SUPER):
        for b in range(nb):
            o_ref[b, :, sb * S5_SUPER_CH:(sb + 1) * S5_SUPER_CH] = ys_ref[sb, pl.ds(b, steps, stride=nb), :]


def _s5(u_slabs, bsb, lr_b, li_b, cre, cim, d, wglu, *, B, S, steps=64):
    rows = steps * B
    return pl.pallas_call(
        functools.partial(_s5_kernel, steps=steps, nb=B),
        out_shape=jax.ShapeDtypeStruct((B, S, S5_WIDTH), F32),
        grid=(S // steps,),
        in_specs=[pl.BlockSpec((S5_SUPER, rows, S5_SUPER_CH), lambda i: (0, i, 0)),
                  _resident((S5_SUPER, S5_SUPER_CH, 2 * S5_SUPER_MODES)),
                  _resident((B, S5_MODES)),
                  _resident((B, S5_MODES)),
                  _resident((S5_SUPER, S5_SUPER_MODES, S5_SUPER_CH)),
                  _resident((S5_SUPER, S5_SUPER_MODES, S5_SUPER_CH)),
                  _resident((1, S5_WIDTH)),
                  _resident((S5_WIDTH, 2 * S5_WIDTH))],
        out_specs=pl.BlockSpec((B, steps, S5_WIDTH), lambda i: (0, i, 0)),
        scratch_shapes=[pltpu.VMEM((rows, 2 * S5_MODES), F32),
                        pltpu.VMEM((B, 2 * S5_MODES), F32),
                        pltpu.VMEM((S5_SUPER, rows, S5_SUPER_CH), F32)],
        compiler_params=_cparams(1),
        name="s5",
    )(u_slabs, bsb, lr_b, li_b, cre, cim, d, wglu)


def _rope_tables(S):
    inv = ROPE_THETA ** (-jnp.arange(HALF, dtype=F32) / HALF)
    ang = jnp.arange(S, dtype=F32)[:, None] * inv[None, :]
    cos, sin = jnp.cos(ang), jnp.sin(ang)
    cos_t = jnp.tile(cos, (1, LANES // HALF))
    sin_t = jnp.tile(jnp.concatenate([-sin, sin], axis=1), (1, LANES // HEAD_DIM))
    end_pos = (jnp.arange(S // CMP_STRIDE, dtype=F32) * CMP_STRIDE + (CMP_LEN - 1))[:, None] * inv[None, :]
    cosc = jnp.tile(jnp.cos(end_pos), (1, LANES // HALF))
    sinc = jnp.tile(jnp.concatenate([-jnp.sin(end_pos), jnp.sin(end_pos)], axis=1), (1, LANES // HEAD_DIM))
    return cos_t, sin_t, cosc, sinc


def _structure_constants(S):
    ncmp = S // CMP_STRIDE
    nsel = S // SEL_BLOCK
    assert nsel <= LANES - HEAD_DIM
    j = jnp.arange(LANES, dtype=jnp.int32)[:, None]
    n = jnp.arange(ncmp, dtype=jnp.int32)[None, :]
    ovt = ((n * CMP_STRIDE < (j + 1) * SEL_BLOCK) & (n * CMP_STRIDE + CMP_LEN > j * SEL_BLOCK)
           & (j < nsel) & (n < ncmp - 1)).astype(BF16)
    key_blk = (jnp.arange(S, dtype=jnp.int32) // SEL_BLOCK)[:, None]
    blk_tab = (key_blk + HEAD_DIM == jnp.arange(LANES, dtype=jnp.int32)[None, :]).astype(F32)
    c = jnp.arange(HALF_W, dtype=jnp.int32)
    avg = ((c[:, None] // HEAD_DIM == c[None, :] // HEAD_DIM).astype(F32) / HEAD_DIM).astype(BF16)
    return ovt, blk_tab, avg


def _even_mixer(x, ffn_args, g, w_in, gm_w_s, gm_b, pe, w1, w2, consts, *, B, S):
    cos_t, sin_t, cosc, sinc, ovt, blk_tab, avg = consts
    w_pad = jnp.pad(w_in, ((0, 0), (0, EVEN_IN_PAD - EVEN_IN))).astype(BF16)
    x, uv, q, gl, kc, vc, ksa, vsa, kwa, vwa = _ffn_even_proj(x, *ffn_args, g, w_pad, cos_t, sin_t, blk_tab,
                                                              B=B, S=S)
    out_a = _gmlp(uv, gm_w_s.reshape(N_HEADS * GM_CHUNK, GM_CHUNK), avg,
                  jnp.repeat(gm_b.T, HEAD_DIM, axis=1))
    kcmp, vcmp = _compress(kc, vc, jnp.tile(pe.reshape(2, CMP_LEN, 1, HEAD_DIM), (1, 1, 1, NSA_KV)),
                           w1.reshape(2, CMP_LEN, HEAD_DIM, HEAD_DIM), w2, cosc, sinc, B=B, S=S)
    out_b = _nsa(q, gl, ksa, vsa, kwa, vwa, kcmp, vcmp, ovt, B=B, S=S)
    return x, out_a, out_b


def _odd_mixer(x, ffn_args, g, w_in, pool_w, pool_scale, lam_re, lam_im, b_re, b_im, c_re, c_im,
               d_skip, log_dt, w_glu, *, B, S):
    x, zc, u_slabs = _ffn_odd_proj(x, *ffn_args, g, w_in.astype(BF16), B=B, S=S)
    y_c = _pool(zc, pool_w, pool_scale.reshape(1, HALF_W), B=B, S=S)

    lbr, lbi, bbr, bbi = _s5_params(lam_re, lam_im, log_dt, b_re, b_im)
    gps = S5_GROUPS // S5_SUPER
    eye = jnp.eye(gps, dtype=F32)
    blockdiag_in = lambda bb: jnp.einsum(
        'sgpc,gh->sgchp', bb.reshape(S5_SUPER, gps, S5_STATE, S5_GROUP_CH), eye
    ).reshape(S5_SUPER, S5_SUPER_CH, S5_SUPER_MODES)
    blockdiag_out = lambda cc: jnp.einsum(
        'sgcp,gh->sgphc', cc.reshape(S5_SUPER, gps, S5_GROUP_CH, S5_STATE), eye
    ).reshape(S5_SUPER, S5_SUPER_MODES, S5_SUPER_CH)
    bsb = jnp.concatenate([blockdiag_in(bbr), blockdiag_in(bbi)], axis=2).astype(BF16)
    bcast = lambda a: jnp.broadcast_to(a.reshape(1, S5_MODES), (B, S5_MODES))
    y_d = _s5(u_slabs, bsb, bcast(lbr), bcast(lbi),
              blockdiag_out(c_re).astype(BF16), blockdiag_out(c_im).astype(BF16),
              d_skip.reshape(1, S5_WIDTH), w_glu.astype(BF16), B=B, S=S)
    return x, y_c, y_d


def kernel(x, norm_w, ffn_w_gate, ffn_w_up, ffn_w_down, final_norm_w, ev_w_in, ev_w_out, gm_w_s, gm_b,
           nsa_cmp_pe, nsa_cmp_w1, nsa_cmp_w2, od_w_in, od_w_out, pool_w, pool_scale, s5_lam_re,
           s5_lam_im, s5_b_re, s5_b_im, s5_c_re, s5_c_im, s5_d, s5_log_dt, s5_w_glu):
    B, S, _ = x.shape
    consts = _rope_tables(S) + _structure_constants(S)
    ffn_w = lambda l, k: (norm_w[l, 2 * k].reshape(1, D_MODEL), ffn_w_gate, ffn_w_up, ffn_w_down, l, k)
    gf = final_norm_w.reshape(1, D_MODEL)
    xt = x
    for l in range(DEPTH):
        g = norm_w[l, 1].reshape(1, D_MODEL)
        i = l // 2
        if l % 2 == 0:
            xt, a, b = _even_mixer(xt, ffn_w(l, 0), g, ev_w_in[i], gm_w_s[i], gm_b[i], nsa_cmp_pe[i],
                                   nsa_cmp_w1[i], nsa_cmp_w2[i], consts, B=B, S=S)
            w_out = ev_w_out[i]
        else:
            xt, a, b = _odd_mixer(xt, ffn_w(l, 0), g, od_w_in[i], pool_w[i], pool_scale[i], s5_lam_re[i],
                                  s5_lam_im[i], s5_b_re[i], s5_b_im[i], s5_c_re[i], s5_c_im[i], s5_d[i],
                                  s5_log_dt[i], s5_w_glu[i], B=B, S=S)
            w_out = od_w_out[i]
        xt = _mix_ffn(a, b, w_out.astype(BF16), xt, *ffn_w(l, 1), gf, final=(l == DEPTH - 1))
    return xt
```

```python
import functools

import jax
import jax.numpy as jnp
from jax import lax
from jax.experimental import pallas as pl
from jax.experimental.pallas import tpu as pltpu

F32 = jnp.float32
BF16 = jnp.bfloat16

D_MODEL = 1024
DEPTH = 4
HEAD_DIM = 64
HALF = HEAD_DIM // 2
ROPE_THETA = 10000.0
N_HEADS = 8
GM_CHUNK = 128
NSA_KV = 2
NSA_HPG = 4
CMP_LEN = 32
CMP_STRIDE = 16
SEL_BLOCK = 64
SEL_TOPN = 8
WINDOW = 512
Q_BLOCK = 256
POOL_WINDOWS = (2, 4, 8, 16)
POOL_GROUP = 128
S5_GROUPS = 32
S5_GROUP_CH = 16
S5_STATE = 64
S5_WIDTH = 512
S5_MODES = S5_GROUPS * S5_STATE
S5_SUPER = 4
S5_SUPER_CH = S5_WIDTH // S5_SUPER
S5_SUPER_MODES = S5_MODES // S5_SUPER
FFN_DIM = 2816
RMS_EPS = 1e-6
LN_EPS = 1e-5
NEG_INF = -1e30
SEL_FORCE = 1e4
LOG2E = 1.4426950408889634

LANES = 128
HALF_W = 512
EVEN_IN = 2328
EVEN_IN_PAD = 2432
COL_U, COL_V, COL_Q = 0, 512, 1024
COL_KC, COL_VC, COL_KS, COL_VS, COL_KW, COL_VW, COL_GL = 1536, 1664, 1792, 1920, 2048, 2176, 2304
SEL_STEP = 256
WIN_PIECE = 256
ONES_LANE = HEAD_DIM

VMEM_LIMIT = 56 * 1024 * 1024
FFN_STAGE = 128
UP_STAGE_ROWS = 64


def _cparams(n_axes):
    return pltpu.CompilerParams(dimension_semantics=("arbitrary",) * n_axes,
                                vmem_limit_bytes=VMEM_LIMIT)


def _resident(shape):
    nd = len(shape)
    return pl.BlockSpec(shape, lambda *_: (0,) * nd, pipeline_mode=pl.Buffered(1))


def _rows(tm, width, S):
    spt = S // tm
    return pl.BlockSpec((None, tm, width), lambda i: (i // spt, i % spt, 0))


def _rms(x, g):
    ms = jnp.mean(x * x, axis=-1, keepdims=True)
    return x * lax.rsqrt(ms + RMS_EPS) * g


def _dot(a, b):
    return jnp.dot(a, b, preferred_element_type=F32)


def _dot_nt(a, b):
    return lax.dot_general(a, b, (((1,), (1,)), ((), ())), preferred_element_type=F32)


def _split_bf16(x):
    hi = x.astype(BF16)
    return hi, (x - hi.astype(F32)).astype(BF16)


def _swiglu_half_step(x, g, wg_ref, wu_ref, wd_ref, fc, before_chunk=None):
    h = _rms(x, g).astype(BF16)
    acc = jnp.zeros_like(x)
    for c in range(FFN_DIM // fc):
        if before_chunk is not None:
            before_chunk(c)
        sl = slice(c * fc, (c + 1) * fc)
        a = _dot(h, wg_ref[:, sl])
        b = _dot(h, wu_ref[:, sl])
        t = (a * jax.nn.sigmoid(a) * b).astype(BF16)
        acc = acc + _dot(t, wd_ref[sl, :])
    return x + 0.5 * acc


def _mix_ffn_kernel(a_ref, b_ref, wo_ref, x_ref, g_ref, wg_hbm, wu_hbm, wd_hbm, gf_ref, o_ref, *scratch,
                    fc, final, l, k):
    x = x_ref[...] + _dot(a_ref[...].astype(BF16), wo_ref[0:HALF_W, :])
    x = x + _dot(b_ref[...].astype(BF16), wo_ref[HALF_W:2 * HALF_W, :])
    def store(v):
        o_ref[...] = v

    _ffn_half_step(x, g_ref[...], (wg_hbm, wu_hbm, wd_hbm), scratch, store, fc, l, k)
    if final:
        o_ref[...] = _rms(o_ref[...], gf_ref[...])


def _ffn_weight_specs():
    hbm = pl.BlockSpec(memory_space=pl.ANY)
    return [_resident((1, D_MODEL)), hbm, hbm, hbm]


def _ffn_weight_scratch():
    return [pltpu.VMEM((D_MODEL, FFN_DIM), BF16), pltpu.VMEM((D_MODEL, FFN_DIM), BF16),
            pltpu.VMEM((FFN_DIM, D_MODEL), BF16),
            pltpu.VMEM((2, UP_STAGE_ROWS, FFN_DIM), F32), pltpu.VMEM((2, FFN_STAGE, D_MODEL), F32),
            pltpu.SemaphoreType.DMA((2, 2))]


def _ffn_half_step(x, g, hbm, scratch, store, fc, l, k):
    wg_hbm, wu_hbm, wd_hbm = hbm
    wg, wu, wd, s_up, s_down, sem = scratch
    n_up = D_MODEL // UP_STAGE_ROWS
    n_down = FFN_DIM // FFN_STAGE
    per_chunk = fc // FFN_STAGE

    def up_copy(i, slot):
        src = wg_hbm if i < n_up else wu_hbm
        rows = pl.ds((i % n_up) * UP_STAGE_ROWS, UP_STAGE_ROWS)
        return pltpu.make_async_copy(src.at[l, k, rows, :], s_up.at[slot], sem.at[0, slot])

    def down_copy(t, slot):
        return pltpu.make_async_copy(wd_hbm.at[l, k, pl.ds(t * FFN_STAGE, FFN_STAGE), :], s_down.at[slot],
                                     sem.at[1, slot])

    def stage_gate_up():
        up_copy(0, 0).start()
        for i in range(2 * n_up):
            slot = i % 2
            if i + 1 < 2 * n_up:
                up_copy(i + 1, 1 - slot).start()
            up_copy(i, slot).wait()
            dst = wg if i < n_up else wu
            r0 = (i % n_up) * UP_STAGE_ROWS
            dst[r0:r0 + UP_STAGE_ROWS, :] = s_up[slot].astype(BF16)

    def stage_down_chunk(c):
        for t in range(c * per_chunk, (c + 1) * per_chunk):
            slot = t % 2
            if t + 1 < n_down:
                down_copy(t + 1, 1 - slot).start()
            down_copy(t, slot).wait()
            wd[t * FFN_STAGE:(t + 1) * FFN_STAGE, :] = s_down[slot].astype(BF16)

    @pl.when(pl.program_id(0) == 0)
    def _():
        down_copy(0, 0).start()
        stage_gate_up()
        store(_swiglu_half_step(x, g, wg, wu, wd, fc, before_chunk=stage_down_chunk))

    @pl.when(pl.program_id(0) > 0)
    def _():
        store(_swiglu_half_step(x, g, wg, wu, wd, fc))


def _mix_ffn(a, b, wo, x, g, wg, wu, wd, l, k, gf, *, final, tm=512, fc=256):
    B, S, _ = x.shape
    return pl.pallas_call(
        functools.partial(_mix_ffn_kernel, fc=fc, final=final, l=l, k=k),
        out_shape=jax.ShapeDtypeStruct((B, S, D_MODEL), F32),
        grid=(B * S // tm,),
        in_specs=[_rows(tm, HALF_W, S), _rows(tm, HALF_W, S), _resident((D_MODEL, D_MODEL)),
                  _rows(tm, D_MODEL, S)] + _ffn_weight_specs() + [_resident((1, D_MODEL))],
        out_specs=_rows(tm, D_MODEL, S),
        scratch_shapes=_ffn_weight_scratch(),
        compiler_params=_cparams(1),
        name="mix_ffn",
    )(a, b, wo, x, g, wg, wu, wd, gf)


def _ffn_odd_proj_kernel(x_ref, g_ref, wg_hbm, wu_hbm, wd_hbm, gp_ref, wp_ref, xo_ref, zc_ref, u_ref,
                         *scratch, fc, nb, ts, l, k):
    def store(v):
        xo_ref[...] = v.reshape(nb, ts, D_MODEL)

    _ffn_half_step(x_ref[...].reshape(nb * ts, D_MODEL), g_ref[...], (wg_hbm, wu_hbm, wd_hbm), scratch, store,
                   fc, l, k)
    x = xo_ref[...].reshape(nb * ts, D_MODEL)
    z = _dot(_rms(x, gp_ref[...]).astype(BF16), wp_ref[...])
    zc_ref[...] = z[:, :HALF_W].reshape(nb, ts, HALF_W)
    for sb in range(S5_SUPER):
        c0 = HALF_W + sb * S5_SUPER_CH
        for b in range(nb):
            u_ref[sb, pl.ds(b, ts, stride=nb), :] = z[b * ts:(b + 1) * ts, c0:c0 + S5_SUPER_CH]


def _ffn_odd_proj(x, g, wg, wu, wd, l, k, gp, wp, *, B, S, ts=64, fc=256):
    whole = lambda width: pl.BlockSpec((B, ts, width), lambda i: (0, i, 0))
    return pl.pallas_call(
        functools.partial(_ffn_odd_proj_kernel, fc=fc, nb=B, ts=ts, l=l, k=k),
        out_shape=(jax.ShapeDtypeStruct((B, S, D_MODEL), F32), jax.ShapeDtypeStruct((B, S, HALF_W), F32),
                   jax.ShapeDtypeStruct((S5_SUPER, S * B, S5_SUPER_CH), F32)),
        grid=(S // ts,),
        in_specs=[whole(D_MODEL)] + _ffn_weight_specs()
                 + [_resident((1, D_MODEL)), _resident((D_MODEL, 2 * HALF_W))],
        out_specs=(whole(D_MODEL), whole(HALF_W),
                   pl.BlockSpec((S5_SUPER, ts * B, S5_SUPER_CH), lambda i: (0, i, 0))),
        scratch_shapes=_ffn_weight_scratch(),
        compiler_params=_cparams(1),
        name="ffn_odd_proj",
    )(x, g, wg, wu, wd, gp, wp)


def _ffn_even_proj_kernel(x_ref, g_ref, wg_hbm, wu_hbm, wd_hbm, gp_ref, w_ref, cos_ref, sin_ref, blk_ref,
                          xo_ref, uv_ref, q_ref, gl_ref, kc_ref, vc_ref, ksa_ref, vsa_ref, kwa_ref, vwa_ref,
                          *scratch, fc, l, k):
    def store(v):
        xo_ref[...] = v

    _ffn_half_step(x_ref[...], g_ref[...], (wg_hbm, wu_hbm, wd_hbm), scratch, store, fc, l, k)
    x = xo_ref[...]
    z = _dot(_rms(x, gp_ref[...]).astype(BF16), w_ref[...])
    cos = cos_ref[...]
    sin = sin_ref[...]
    lane = lax.broadcasted_iota(jnp.int32, cos.shape, 1)
    first_half = (lane % HEAD_DIM) < HALF
    low = lane < HEAD_DIM

    def slab(c0, rotary):
        xs = z[:, c0:c0 + LANES]
        if not rotary:
            return xs
        rot = jnp.where(first_half, pltpu.roll(xs, LANES - HALF, 1), pltpu.roll(xs, HALF, 1))
        return xs * cos + rot * sin

    uv_ref[...] = z[:, COL_U:COL_Q]
    for c in range(HALF_W // LANES):
        q_ref[:, c * LANES:(c + 1) * LANES] = slab(COL_Q + c * LANES, True)
    gl_ref[...] = z[:, COL_GL:COL_GL + LANES]
    kc_ref[...] = slab(COL_KC, False)
    vc_ref[...] = slab(COL_VC, False)
    ones = (lane == ONES_LANE).astype(F32)
    for c0, rotary, extra, dst in ((COL_KS, True, blk_ref[...], ksa_ref), (COL_VS, False, ones, vsa_ref),
                                   (COL_KW, True, 0.0, kwa_ref), (COL_VW, False, ones, vwa_ref)):
        xs = slab(c0, rotary)
        dst[0, 0] = jnp.where(low, xs, extra).astype(BF16)
        dst[0, 1] = jnp.where(low, pltpu.roll(xs, HEAD_DIM, 1), extra).astype(BF16)


def _ffn_even_proj(x, g, wg, wu, wd, l, k, gp, w, cos, sin, blk, *, B, S, tm=512, fc=256):
    spt = S // tm
    tok = lambda width: _rows(tm, width, S)
    pos = pl.BlockSpec((tm, LANES), lambda i: (i % spt, 0))
    grp = pl.BlockSpec((1, NSA_KV, tm, LANES), lambda i: (i // spt, 0, i % spt, 0))
    aug = jax.ShapeDtypeStruct((B, NSA_KV, S, LANES), BF16)
    tokens = lambda width: jax.ShapeDtypeStruct((B, S, width), F32)
    return pl.pallas_call(
        functools.partial(_ffn_even_proj_kernel, fc=fc, l=l, k=k),
        out_shape=(tokens(D_MODEL), tokens(2 * HALF_W), tokens(HALF_W), tokens(LANES), tokens(LANES),
                   tokens(LANES), aug, aug, aug, aug),
        grid=(B * S // tm,),
        in_specs=[tok(D_MODEL)] + _ffn_weight_specs()
                 + [_resident((1, D_MODEL)), _resident((D_MODEL, EVEN_IN_PAD)), pos, pos, pos],
        out_specs=(tok(D_MODEL), tok(2 * HALF_W), tok(HALF_W), tok(LANES), tok(LANES), tok(LANES),
                   grp, grp, grp, grp),
        scratch_shapes=_ffn_weight_scratch(),
        compiler_params=_cparams(1),
        name="ffn_even_proj",
    )(x, g, wg, wu, wd, gp, w, cos, sin, blk)


def _gmlp_kernel(u_ref, v_ref, ws_ref, avg_ref, bias_ref, o_ref, *, chunks):
    rows_all = N_HEADS * GM_CHUNK
    r = lax.broadcasted_iota(jnp.int32, (rows_all, GM_CHUNK), 0) & (GM_CHUNK - 1)
    c = lax.broadcasted_iota(jnp.int32, (rows_all, GM_CHUNK), 1)
    w_all = jnp.where(c <= r, ws_ref[...], 0.0).astype(BF16)
    avg = avg_ref[...]

    def head_mean(x):
        return _dot(x.astype(BF16), avg)

    v = jax.nn.gelu(v_ref[...])
    d = v - head_mean(v)
    vn = (d * lax.rsqrt(head_mean(d * d) + LN_EPS)).astype(BF16)
    lane_head = lax.broadcasted_iota(jnp.int32, (GM_CHUNK, HALF_W), 1) // HEAD_DIM
    for ci in range(chunks):
        rows = slice(ci * GM_CHUNK, (ci + 1) * GM_CHUNK)
        res = _dot(w_all, vn[rows, :])
        s = res[0:GM_CHUNK, :]
        for h in range(1, N_HEADS):
            s = jnp.where(lane_head == h, res[h * GM_CHUNK:(h + 1) * GM_CHUNK, :], s)
        o_ref[rows, :] = jax.nn.gelu(u_ref[rows, :]) * (s + bias_ref[...])


def _gmlp(uv, ws, avg, bias, *, chunks=4):
    B, S, _ = uv.shape
    tm = chunks * GM_CHUNK
    spt = S // tm
    return pl.pallas_call(
        functools.partial(_gmlp_kernel, chunks=chunks),
        out_shape=jax.ShapeDtypeStruct((B, S, HALF_W), F32),
        grid=(B * S // tm,),
        in_specs=[pl.BlockSpec((None, tm, HALF_W), lambda i: (i // spt, i % spt, 0)),
                  pl.BlockSpec((None, tm, HALF_W), lambda i: (i // spt, i % spt, 1)),
                  _resident((N_HEADS * GM_CHUNK, GM_CHUNK)),
                  _resident((HALF_W, HALF_W)),
                  _resident((GM_CHUNK, HALF_W))],
        out_specs=_rows(tm, HALF_W, S),
        compiler_params=_cparams(1),
        name="gmlp",
    )(uv, uv, ws, avg, bias)


def _cmp_kernel(k_ref, v_ref, pe_ref, w1_ref, w2_ref, cos_ref, sin_ref, ko_ref, vo_ref, *, nblk):
    lane = lax.broadcasted_iota(jnp.int32, (nblk, LANES), 1)
    low = lane < HEAD_DIM
    first_half = (lane % HEAD_DIM) < HALF

    def both_groups(w):
        z = jnp.zeros_like(w)
        return jnp.concatenate([jnp.concatenate([w, z], axis=1), jnp.concatenate([z, w], axis=1)],
                               axis=0).astype(BF16)

    for which, (src, dst) in enumerate(((k_ref, ko_ref), (v_ref, vo_ref))):
        pre = jnp.zeros((nblk, LANES), F32)
        for r in range(CMP_STRIDE):
            rows = src[pl.ds(r, nblk, stride=CMP_STRIDE), :]
            nxt = pltpu.roll(rows, nblk - 1, 0)
            pre = pre + _dot((rows + pe_ref[which, r]).astype(BF16), both_groups(w1_ref[which, r]))
            pre = pre + _dot((nxt + pe_ref[which, CMP_STRIDE + r]).astype(BF16),
                             both_groups(w1_ref[which, CMP_STRIDE + r]))
        cmp = _dot(jax.nn.gelu(pre).astype(BF16), both_groups(w2_ref[which]))
        if which == 0:
            rot = jnp.where(first_half, pltpu.roll(cmp, LANES - HALF, 1), pltpu.roll(cmp, HALF, 1))
            cmp = cmp * cos_ref[...] + rot * sin_ref[...]
        dst[0, 0] = jnp.where(low, cmp, 0.0).astype(BF16)
        dst[0, 1] = jnp.where(low, pltpu.roll(cmp, HEAD_DIM, 1), 0.0).astype(BF16)


def _compress(kc, vc, pe, w1, w2, cosc, sinc, *, B, S):
    nblk = S // CMP_STRIDE
    out = jax.ShapeDtypeStruct((B, NSA_KV, nblk, LANES), BF16)
    out_spec = pl.BlockSpec((1, NSA_KV, nblk, LANES), lambda b: (b, 0, 0, 0))
    return pl.pallas_call(
        functools.partial(_cmp_kernel, nblk=nblk),
        out_shape=(out, out),
        grid=(B,),
        in_specs=[pl.BlockSpec((None, S, LANES), lambda b: (b, 0, 0)),
                  pl.BlockSpec((None, S, LANES), lambda b: (b, 0, 0)),
                  _resident((2, CMP_LEN, 1, LANES)),
                  _resident((2, CMP_LEN, HEAD_DIM, HEAD_DIM)),
                  _resident((2, HEAD_DIM, HEAD_DIM)),
                  _resident((nblk, LANES)),
                  _resident((nblk, LANES))],
        out_specs=(out_spec, out_spec),
        compiler_params=_cparams(1),
        name="nsa_compress",
    )(kc, vc, pe, w1, w2, cosc, sinc)


def _softmax_pv(pieces):
    ms = []
    for group in pieces:
        m = None
        for s, _ in group:
            pm = jnp.max(s, axis=-1, keepdims=True)
            m = pm if m is None else jnp.maximum(m, pm)
        ms.append(m)
    accs = [None] * len(pieces)
    for j in range(len(pieces[0])):
        for g, group in enumerate(pieces):
            s, vv = group[j]
            part = _dot(jnp.exp2(s - ms[g]).astype(BF16), vv)
            accs[g] = part if accs[g] is None else accs[g] + part
    return [acc / acc[:, ONES_LANE:ONES_LANE + 1] for acc in accs]


def _nsa_kernel(q_ref, gl_ref, ksa_ref, vsa_ref, kwa_ref, vwa_ref, kc_ref, vc_ref, ovt_ref, o_ref,
                osel_ref, *, S):
    TQ = Q_BLOCK
    R = NSA_HPG * TQ
    G = range(NSA_KV)
    ncmp = S // CMP_STRIDE
    nsel = S // SEL_BLOCK
    i = pl.program_id(1)
    t0 = i * TQ

    q = q_ref[...] * (HEAD_DIM ** -0.5 * LOG2E)
    gate = jax.nn.sigmoid(gl_ref[...])
    lane = lax.broadcasted_iota(jnp.int32, (TQ, LANES), 1)
    lane4 = lax.broadcasted_iota(jnp.int32, (R, LANES), 1)
    trow4 = t0 + (lax.broadcasted_iota(jnp.int32, (R, 1), 0) & (TQ - 1))

    def stack_heads(g):
        heads = []
        for hq in range(NSA_HPG):
            h = g * NSA_HPG + hq
            slab = q[:, (h // 2) * LANES:(h // 2 + 1) * LANES]
            if h % 2 == 1:
                slab = pltpu.roll(slab, HEAD_DIM, 1)
            heads.append(jnp.where(lane < HEAD_DIM, slab, 0.0))
        return jnp.concatenate(heads, axis=0)

    q4 = [stack_heads(g) for g in G]
    q4b = [x.astype(BF16) for x in q4]

    s_c = [_dot_nt(q4b[g], kc_ref[0, g]) for g in G]
    k0 = pl.multiple_of(jnp.maximum(i - WINDOW // TQ, 0) * TQ, TQ)
    win = [[] for _ in G]
    span = WINDOW + TQ
    for lo, hi in [(lo, min(lo + WIN_PIECE, span)) for lo in range(0, span, WIN_PIECE)]:
        kpos = k0 + lo + lax.broadcasted_iota(jnp.int32, (R, hi - lo), 1)
        allowed = (kpos <= trow4) & (trow4 - kpos < WINDOW)
        for g in G:
            s = _dot_nt(q4b[g], kwa_ref[0, g, pl.ds(k0 + lo, hi - lo), :])
            win[g].append((jnp.where(allowed, s, NEG_INF), vwa_ref[0, g, pl.ds(k0 + lo, hi - lo), :]))

    ncol = lax.broadcasted_iota(jnp.int32, (R, ncmp), 1)
    valid_c = (ncol * CMP_STRIDE + (CMP_LEN - 1) <= trow4) & (ncol < ncmp - 1)
    any_c = (trow4 >= CMP_LEN - 1).astype(F32)
    p_c = []
    for g in G:
        s = jnp.where(valid_c, s_c[g], NEG_INF)
        e = jnp.exp2(s - jnp.max(s, axis=-1, keepdims=True))
        p_c.append(e / jnp.sum(e, axis=-1, keepdims=True) * any_c)
    o_cmp = [_dot(p_c[g].astype(BF16), vc_ref[0, g]) for g in G]

    ovt = ovt_ref[...]
    jrow = lax.broadcasted_iota(jnp.int32, (nsel, TQ), 0)
    tl = t0 + lax.broadcasted_iota(jnp.int32, (nsel, TQ), 1)
    cur = jnp.right_shift(tl, 6)
    forced = (jrow == 0) | (jrow == cur) | (jrow == cur - 1)
    causal_blk = jrow * SEL_BLOCK <= tl
    imp_t = []
    for g in G:
        p = p_c[g]
        p_hi, p_lo = _split_bf16((p[0:TQ] + p[TQ:2 * TQ]) + (p[2 * TQ:3 * TQ] + p[3 * TQ:4 * TQ]))
        imp = (_dot_nt(ovt, p_hi) + _dot_nt(ovt, p_lo))[0:nsel, :]
        imp_t.append(jnp.where(forced, SEL_FORCE, jnp.where(causal_blk, imp, -SEL_FORCE)))

    SUB = 8
    sub = lax.broadcasted_iota(jnp.int32, (SUB, TQ), 0)
    chunks = [[imp_t[g][c * SUB:(c + 1) * SUB] for c in range(nsel // SUB)] for g in G]
    rank = [[jnp.zeros((SUB, TQ), F32) for _ in range(nsel // SUB)] for _ in G]
    for jp in range(nsel):
        for g in G:
            row = imp_t[g][jp:jp + 1, :]
            for c, x in enumerate(chunks[g]):
                ge = jnp.where(row >= x, 1.0, 0.0)
                gt = jnp.where(row > x, 1.0, 0.0)
                if jp < c * SUB:
                    inc = ge
                elif jp >= (c + 1) * SUB:
                    inc = gt
                else:
                    inc = jnp.where(sub > jp - c * SUB, ge, gt)
                rank[g][c] = rank[g][c] + inc

    q4a = []
    for g in G:
        sel_t = (jnp.concatenate(rank[g], axis=0) < SEL_TOPN).astype(F32)
        if nsel < LANES:
            sel_t = jnp.concatenate([sel_t, jnp.zeros((LANES - nsel, TQ), F32)], axis=0)
        bias = (pltpu.roll(sel_t.T, HEAD_DIM, 1) - 1.0) * (-NEG_INF)
        q4a.append(jnp.where(lane4 < HEAD_DIM, q4[g], jnp.concatenate([bias] * NSA_HPG, axis=0)).astype(BF16))

    o_win = _softmax_pv(win)

    col_step = lax.broadcasted_iota(jnp.int32, (R, SEL_STEP), 1)
    for n in range(S // SEL_STEP):
        @pl.when(i // (SEL_STEP // TQ) == n)
        def _(n=n):
            pieces = [[] for _ in G]
            for j in range(n + 1):
                lo, hi = j * SEL_STEP, (j + 1) * SEL_STEP
                for g in G:
                    s = _dot_nt(q4a[g], ksa_ref[0, g, lo:hi, :])
                    if j == n:
                        s = jnp.where(lo + col_step <= trow4, s, NEG_INF)
                    pieces[g].append((s, vsa_ref[0, g, lo:hi, :]))
            for g, o in enumerate(_softmax_pv(pieces)):
                osel_ref[g] = o

    for g in G:
        o_s = osel_ref[g]
        outs = []
        for hq in range(NSA_HPG):
            h = g * NSA_HPG + hq
            rows = slice(hq * TQ, (hq + 1) * TQ)
            outs.append(gate[:, 3 * h:3 * h + 1] * o_cmp[g][rows] + gate[:, 3 * h + 1:3 * h + 2] * o_s[rows]
                        + gate[:, 3 * h + 2:3 * h + 3] * o_win[g][rows])
        for pair in range(NSA_HPG // 2):
            c0 = (g * NSA_HPG + 2 * pair) * HEAD_DIM
            o_ref[:, c0:c0 + LANES] = jnp.where(lane < HEAD_DIM, outs[2 * pair],
                                                pltpu.roll(outs[2 * pair + 1], HEAD_DIM, 1))


def _nsa(q, gl, ksa, vsa, kwa, vwa, kcmp, vcmp, ovt, *, B, S):
    nq = S // Q_BLOCK
    ncmp = S // CMP_STRIDE
    per_batch = lambda rows: pl.BlockSpec((1, NSA_KV, rows, LANES), lambda b, i: (b, 0, 0, 0))
    return pl.pallas_call(
        functools.partial(_nsa_kernel, S=S),
        out_shape=jax.ShapeDtypeStruct((B, S, HALF_W), F32),
        grid=(B, nq),
        in_specs=[pl.BlockSpec((None, Q_BLOCK, HALF_W), lambda b, i: (b, i, 0)),
                  pl.BlockSpec((None, Q_BLOCK, LANES), lambda b, i: (b, i, 0)),
                  per_batch(S), per_batch(S), per_batch(S), per_batch(S),
                  per_batch(ncmp), per_batch(ncmp),
                  _resident((LANES, ncmp))],
        out_specs=pl.BlockSpec((None, Q_BLOCK, HALF_W), lambda b, i: (b, i, 0)),
        scratch_shapes=[pltpu.VMEM((NSA_KV, NSA_HPG * Q_BLOCK, LANES), F32)],
        compiler_params=_cparams(2),
        name="nsa_attn",
    )(q, gl, ksa, vsa, kwa, vwa, kcmp, vcmp, ovt)


def _pool_kernel(z_ref, pw_ref, ps_ref, o_ref):
    S = z_ref.shape[0]
    row = lax.broadcasted_iota(jnp.int32, (S, POOL_GROUP), 0)
    for gi, w in enumerate(POOL_WINDOWS):
        cols = slice(gi * POOL_GROUP, (gi + 1) * POOL_GROUP)
        zg = z_ref[:, cols]
        acc = zg
        k = 1
        while k < w:
            acc = acc + jnp.where(row >= k, pltpu.roll(acc, k, 0), 0.0)
            k *= 2
        cnt = jnp.minimum(row + 1, w).astype(F32)
        pooled = acc / cnt - zg
        y = _dot(pooled.astype(BF16), pw_ref[gi].astype(BF16))
        o_ref[:, cols] = y * ps_ref[:, cols]


def _pool(z, pw, ps, *, B, S):
    return pl.pallas_call(
        _pool_kernel,
        out_shape=jax.ShapeDtypeStruct((B, S, HALF_W), F32),
        grid=(B,),
        in_specs=[pl.BlockSpec((None, S, HALF_W), lambda b: (b, 0, 0)),
                  _resident((len(POOL_WINDOWS), POOL_GROUP, POOL_GROUP)),
                  _resident((1, HALF_W))],
        out_specs=pl.BlockSpec((None, S, HALF_W), lambda b: (b, 0, 0)),
        compiler_params=_cparams(1),
        name="pool",
    )(z, pw, ps)


def _s5_disc_kernel(lr_ref, li_ref, ldt_ref, lbr_ref, lbi_ref, cr_ref, ci_ref):
    lr = lr_ref[...]
    li = li_ref[...]
    dt = jnp.exp(ldt_ref[...])
    mag = jnp.exp(lr * dt)
    ang = li * dt
    lbr = mag * jnp.cos(ang)
    lbi = mag * jnp.sin(ang)
    nr = lbr - 1.0
    den = lr * lr + li * li
    lbr_ref[...] = lbr
    lbi_ref[...] = lbi
    cr_ref[...] = (nr * lr + lbi * li) / den
    ci_ref[...] = (lbi * lr - nr * li) / den


def _s5_bbar_kernel(cr_ref, ci_ref, br_ref, bi_ref, or_ref, oi_ref):
    cr = cr_ref[...]
    ci = ci_ref[...]
    br = br_ref[...]
    bi = bi_ref[...]
    or_ref[...] = cr * br - ci * bi
    oi_ref[...] = cr * bi + ci * br


def _s5_params(lam_re, lam_im, log_dt, b_re, b_im):
    gp = jax.ShapeDtypeStruct((S5_GROUPS, S5_STATE), F32)
    lbr, lbi, cr, ci = pl.pallas_call(_s5_disc_kernel, out_shape=(gp, gp, gp, gp), name="s5_disc")(
        lam_re, lam_im, log_dt.reshape(S5_GROUPS, 1))
    flat = jax.ShapeDtypeStruct((S5_MODES, S5_GROUP_CH), F32)
    bbr, bbi = pl.pallas_call(_s5_bbar_kernel, out_shape=(flat, flat), name="s5_bbar")(
        cr.reshape(S5_MODES, 1), ci.reshape(S5_MODES, 1),
        b_re.reshape(S5_MODES, S5_GROUP_CH), b_im.reshape(S5_MODES, S5_GROUP_CH))
    return lbr, lbi, bbr, bbi


def _s5_kernel(u_ref, bsb_ref, lr_ref, li_ref, cre_ref, cim_ref, d_ref, wglu_ref, o_ref,
               xs_ref, st_ref, ys_ref, *, steps, nb):
    M = S5_MODES
    W = S5_SUPER_MODES

    @pl.when(pl.program_id(0) == 0)
    def _():
        st_ref[...] = jnp.zeros_like(st_ref)

    half = steps // 2 * nb
    halves = (slice(0, half), slice(half, 2 * half))

    def input_drive(rows, sb):
        bu = _dot(u_ref[sb, rows, :].astype(BF16), bsb_ref[sb])
        xs_ref[rows, sb * W:(sb + 1) * W] = bu[:, :W]
        xs_ref[rows, M + sb * W:M + (sb + 1) * W] = bu[:, W:]

    def readout(rows, sb):
        y = (_dot(xs_ref[rows, sb * W:(sb + 1) * W].astype(BF16), cre_ref[sb])
             - _dot(xs_ref[rows, M + sb * W:M + (sb + 1) * W].astype(BF16), cim_ref[sb]))
        return y + d_ref[:, sb * S5_SUPER_CH:(sb + 1) * S5_SUPER_CH] * u_ref[sb, rows, :]

    def scan(first, carry, side_work):
        xr, xi = carry
        lr = lr_ref[...]
        li = li_ref[...]
        every = (steps // 2) // len(side_work)
        done = []
        for t in range(steps // 2):
            if t % every == 0:
                done.append(side_work[t // every]())
            r0 = (first + t) * nb
            nr = lr * xr - li * xi + xs_ref[r0:r0 + nb, 0:M]
            ni = lr * xi + li * xr + xs_ref[r0:r0 + nb, M:2 * M]
            xs_ref[r0:r0 + nb, 0:M] = nr
            xs_ref[r0:r0 + nb, M:2 * M] = ni
            xr, xi = nr, ni
        return (xr, xi), done

    for sb in range(S5_SUPER):
        input_drive(halves[0], sb)
    carry = (st_ref[:, 0:M], st_ref[:, M:2 * M])
    carry, _ = scan(0, carry, [functools.partial(input_drive, halves[1], sb) for sb in range(S5_SUPER)])
    carry, ys0 = scan(steps // 2, carry, [functools.partial(readout, halves[0], sb) for sb in range(S5_SUPER)])
    st_ref[:, 0:M] = carry[0]
    st_ref[:, M:2 * M] = carry[1]
    ys1 = [readout(halves[1], sb) for sb in range(S5_SUPER)]

    for rows, ys in zip(halves, (ys0, ys1)):
        y = jax.nn.gelu(jnp.concatenate(ys, axis=1))
        ab = _dot(y.astype(BF16), wglu_ref[...])
        out = ab[:, :S5_WIDTH] * jax.nn.sigmoid(ab[:, S5_WIDTH:])
        for sb in range(S5_SUPER):
            ys_ref[sb, rows, :] = out[:, sb * S5_SUPER_CH:(sb + 1) * S5_SUPER_CH]
    for sb in range(S5_SUPER):
        for b in range(nb):
            o_ref[b, :, sb * S5_SUPER_CH:(sb + 1) * S5_SUPER_CH] = ys_ref[sb, pl.ds(b, steps, stride=nb), :]


def _s5(u_slabs, bsb, lr_b, li_b, cre, cim, d, wglu, *, B, S, steps=64):
    rows = steps * B
    return pl.pallas_call(
        functools.partial(_s5_kernel, steps=steps, nb=B),
        out_shape=jax.ShapeDtypeStruct((B, S, S5_WIDTH), F32),
        grid=(S // steps,),
        in_specs=[pl.BlockSpec((S5_SUPER, rows, S5_SUPER_CH), lambda i: (0, i, 0)),
                  _resident((S5_SUPER, S5_SUPER_CH, 2 * S5_SUPER_MODES)),
                  _resident((B, S5_MODES)),
                  _resident((B, S5_MODES)),
                  _resident((S5_SUPER, S5_SUPER_MODES, S5_SUPER_CH)),
                  _resident((S5_SUPER, S5_SUPER_MODES, S5_SUPER_CH)),
                  _resident((1, S5_WIDTH)),
                  _resident((S5_WIDTH, 2 * S5_WIDTH))],
        out_specs=pl.BlockSpec((B, steps, S5_WIDTH), lambda i: (0, i, 0)),
        scratch_shapes=[pltpu.VMEM((rows, 2 * S5_MODES), F32),
                        pltpu.VMEM((B, 2 * S5_MODES), F32),
                        pltpu.VMEM((S5_SUPER, rows, S5_SUPER_CH), F32)],
        compiler_params=_cparams(1),
        name="s5",
    )(u_slabs, bsb, lr_b, li_b, cre, cim, d, wglu)


def _rope_tables(S):
    inv = ROPE_THETA ** (-jnp.arange(HALF, dtype=F32) / HALF)
    ang = jnp.arange(S, dtype=F32)[:, None] * inv[None, :]
    cos, sin = jnp.cos(ang), jnp.sin(ang)
    cos_t = jnp.tile(cos, (1, LANES // HALF))
    sin_t = jnp.tile(jnp.concatenate([-sin, sin], axis=1), (1, LANES // HEAD_DIM))
    end_pos = (jnp.arange(S // CMP_STRIDE, dtype=F32) * CMP_STRIDE + (CMP_LEN - 1))[:, None] * inv[None, :]
    cosc = jnp.tile(jnp.cos(end_pos), (1, LANES // HALF))
    sinc = jnp.tile(jnp.concatenate([-jnp.sin(end_pos), jnp.sin(end_pos)], axis=1), (1, LANES // HEAD_DIM))
    return cos_t, sin_t, cosc, sinc


def _structure_constants(S):
    ncmp = S // CMP_STRIDE
    nsel = S // SEL_BLOCK
    assert nsel <= LANES - HEAD_DIM
    j = jnp.arange(LANES, dtype=jnp.int32)[:, None]
    n = jnp.arange(ncmp, dtype=jnp.int32)[None, :]
    ovt = ((n * CMP_STRIDE < (j + 1) * SEL_BLOCK) & (n * CMP_STRIDE + CMP_LEN > j * SEL_BLOCK)
           & (j < nsel) & (n < ncmp - 1)).astype(BF16)
    key_blk = (jnp.arange(S, dtype=jnp.int32) // SEL_BLOCK)[:, None]
    blk_tab = (key_blk + HEAD_DIM == jnp.arange(LANES, dtype=jnp.int32)[None, :]).astype(F32)
    c = jnp.arange(HALF_W, dtype=jnp.int32)
    avg = ((c[:, None] // HEAD_DIM == c[None, :] // HEAD_DIM).astype(F32) / HEAD_DIM).astype(BF16)
    return ovt, blk_tab, avg


def _even_mixer(x, ffn_args, g, w_in, gm_w_s, gm_b, pe, w1, w2, consts, *, B, S):
    cos_t, sin_t, cosc, sinc, ovt, blk_tab, avg = consts
    w_pad = jnp.pad(w_in, ((0, 0), (0, EVEN_IN_PAD - EVEN_IN))).astype(BF16)
    x, uv, q, gl, kc, vc, ksa, vsa, kwa, vwa = _ffn_even_proj(x, *ffn_args, g, w_pad, cos_t, sin_t, blk_tab,
                                                              B=B, S=S)
    out_a = _gmlp(uv, gm_w_s.reshape(N_HEADS * GM_CHUNK, GM_CHUNK), avg,
                  jnp.repeat(gm_b.T, HEAD_DIM, axis=1))
    kcmp, vcmp = _compress(kc, vc, jnp.tile(pe.reshape(2, CMP_LEN, 1, HEAD_DIM), (1, 1, 1, NSA_KV)),
                           w1.reshape(2, CMP_LEN, HEAD_DIM, HEAD_DIM), w2, cosc, sinc, B=B, S=S)
    out_b = _nsa(q, gl, ksa, vsa, kwa, vwa, kcmp, vcmp, ovt, B=B, S=S)
    return x, out_a, out_b


def _odd_mixer(x, ffn_args, g, w_in, pool_w, pool_scale, lam_re, lam_im, b_re, b_im, c_re, c_im,
               d_skip, log_dt, w_glu, *, B, S):
    x, zc, u_slabs = _ffn_odd_proj(x, *ffn_args, g, w_in.astype(BF16), B=B, S=S)
    y_c = _pool(zc, pool_w, pool_scale.reshape(1, HALF_W), B=B, S=S)

    lbr, lbi, bbr, bbi = _s5_params(lam_re, lam_im, log_dt, b_re, b_im)
    gps = S5_GROUPS // S5_SUPER
    eye = jnp.eye(gps, dtype=F32)
    blockdiag_in = lambda bb: jnp.einsum(
        'sgpc,gh->sgchp', bb.reshape(S5_SUPER, gps, S5_STATE, S5_GROUP_CH), eye
    ).reshape(S5_SUPER, S5_SUPER_CH, S5_SUPER_MODES)
    blockdiag_out = lambda cc: jnp.einsum(
        'sgcp,gh->sgphc', cc.reshape(S5_SUPER, gps, S5_GROUP_CH, S5_STATE), eye
    ).reshape(S5_SUPER, S5_SUPER_MODES, S5_SUPER_CH)
    bsb = jnp.concatenate([blockdiag_in(bbr), blockdiag_in(bbi)], axis=2).astype(BF16)
    bcast = lambda a: jnp.broadcast_to(a.reshape(1, S5_MODES), (B, S5_MODES))
    y_d = _s5(u_slabs, bsb, bcast(lbr), bcast(lbi),
              blockdiag_out(c_re).astype(BF16), blockdiag_out(c_im).astype(BF16),
              d_skip.reshape(1, S5_WIDTH), w_glu.astype(BF16), B=B, S=S)
    return x, y_c, y_d


def kernel(x, norm_w, ffn_w_gate, ffn_w_up, ffn_w_down, final_norm_w, ev_w_in, ev_w_out, gm_w_s, gm_b,
           nsa_cmp_pe, nsa_cmp_w1, nsa_cmp_w2, od_w_in, od_w_out, pool_w, pool_scale, s5_lam_re,
           s5_lam_im, s5_b_re, s5_b_im, s5_c_re, s5_c_im, s5_d, s5_log_dt, s5_w_glu):
    B, S, _ = x.shape
    consts = _rope_tables(S) + _structure_constants(S)
    ffn_w = lambda l, k: (norm_w[l, 2 * k].reshape(1, D_MODEL), ffn_w_gate, ffn_w_up, ffn_w_down, l, k)
    gf = final_norm_w.reshape(1, D_MODEL)
    xt = x
    for l in range(DEPTH):
        g = norm_w[l, 1].reshape(1, D_MODEL)
        i = l // 2
        if l % 2 == 0:
            xt, a, b = _even_mixer(xt, ffn_w(l, 0), g, ev_w_in[i], gm_w_s[i], gm_b[i], nsa_cmp_pe[i],
                                   nsa_cmp_w1[i], nsa_cmp_w2[i], consts, B=B, S=S)
            w_out = ev_w_out[i]
        else:
            xt, a, b = _odd_mixer(xt, ffn_w(l, 0), g, od_w_in[i], pool_w[i], pool_scale[i], s5_lam_re[i],
                                  s5_lam_im[i], s5_b_re[i], s5_b_im[i], s5_c_re[i], s5_c_im[i], s5_d[i],
                                  s5_log_dt[i], s5_w_glu[i], B=B, S=S)
            w_out = od_w_out[i]
        xt = _mix_ffn(a, b, w_out.astype(BF16), xt, *ffn_w(l, 1), gf, final=(l == DEPTH - 1))
    return xt
```

```python
import functools

import jax
import jax.numpy as jnp
from jax import lax
from jax.experimental import pallas as pl
from jax.experimental.pallas import tpu as pltpu

F32 = jnp.float32
BF16 = jnp.bfloat16

D_MODEL = 1024
DEPTH = 4
HEAD_DIM = 64
HALF = HEAD_DIM // 2
ROPE_THETA = 10000.0
N_HEADS = 8
GM_CHUNK = 128
NSA_KV = 2
NSA_HPG = 4
CMP_LEN = 32
CMP_STRIDE = 16
SEL_BLOCK = 64
SEL_TOPN = 8
WINDOW = 512
Q_BLOCK = 256
POOL_WINDOWS = (2, 4, 8, 16)
POOL_GROUP = 128
S5_GROUPS = 32
S5_GROUP_CH = 16
S5_STATE = 64
S5_WIDTH = 512
S5_MODES = S5_GROUPS * S5_STATE
S5_SUPER = 4
S5_SUPER_CH = S5_WIDTH // S5_SUPER
S5_SUPER_MODES = S5_MODES // S5_SUPER
FFN_DIM = 2816
RMS_EPS = 1e-6
LN_EPS = 1e-5
NEG_INF = -1e30
SEL_FORCE = 1e4
LOG2E = 1.4426950408889634

LANES = 128
HALF_W = 512
EVEN_IN = 2328
EVEN_IN_PAD = 2432
COL_U, COL_V, COL_Q = 0, 512, 1024
COL_KC, COL_VC, COL_KS, COL_VS, COL_KW, COL_VW, COL_GL = 1536, 1664, 1792, 1920, 2048, 2176, 2304
SEL_STEP = 256
WIN_PIECE = 256
ONES_LANE = HEAD_DIM

VMEM_LIMIT = 56 * 1024 * 1024


def _cparams(n_axes):
    return pltpu.CompilerParams(dimension_semantics=("arbitrary",) * n_axes,
                                vmem_limit_bytes=VMEM_LIMIT)


def _resident(shape):
    nd = len(shape)
    return pl.BlockSpec(shape, lambda *_: (0,) * nd, pipeline_mode=pl.Buffered(1))


def _rows(tm, width, S):
    spt = S // tm
    return pl.BlockSpec((None, tm, width), lambda i: (i // spt, i % spt, 0))


def _rms(x, g):
    ms = jnp.mean(x * x, axis=-1, keepdims=True)
    return x * lax.rsqrt(ms + RMS_EPS) * g


def _dot(a, b):
    return jnp.dot(a, b, preferred_element_type=F32)


def _dot_nt(a, b):
    return lax.dot_general(a, b, (((1,), (1,)), ((), ())), preferred_element_type=F32)


def _split_bf16(x):
    hi = x.astype(BF16)
    return hi, (x - hi.astype(F32)).astype(BF16)


def _swiglu_half_step(x, g, wg_ref, wu_ref, wd_ref, fc):
    h = _rms(x, g).astype(BF16)
    acc = jnp.zeros_like(x)
    for c in range(FFN_DIM // fc):
        sl = slice(c * fc, (c + 1) * fc)
        a = _dot(h, wg_ref[:, sl])
        b = _dot(h, wu_ref[:, sl])
        t = (a * jax.nn.sigmoid(a) * b).astype(BF16)
        acc = acc + _dot(t, wd_ref[sl, :])
    return x + 0.5 * acc


def _mix_ffn_kernel(a_ref, b_ref, wo_ref, x_ref, g_ref, wg_ref, wu_ref, wd_ref, gf_ref, o_ref, *, fc, final):
    x = x_ref[...] + _dot(a_ref[...].astype(BF16), wo_ref[0:HALF_W, :])
    x = x + _dot(b_ref[...].astype(BF16), wo_ref[HALF_W:2 * HALF_W, :])
    x = _swiglu_half_step(x, g_ref[...], wg_ref, wu_ref, wd_ref, fc)
    o_ref[...] = _rms(x, gf_ref[...]) if final else x


def _ffn_weight_specs(l, k):
    pick = lambda rows, cols: pl.BlockSpec((None, None, rows, cols), lambda *_: (l, k, 0, 0),
                                           pipeline_mode=pl.Buffered(1))
    return [_resident((1, D_MODEL)), pick(D_MODEL, FFN_DIM), pick(D_MODEL, FFN_DIM), pick(FFN_DIM, D_MODEL)]


def _mix_ffn(a, b, wo, x, g, wg, wu, wd, l, k, gf, *, final, tm=512, fc=256):
    B, S, _ = x.shape
    return pl.pallas_call(
        functools.partial(_mix_ffn_kernel, fc=fc, final=final),
        out_shape=jax.ShapeDtypeStruct((B, S, D_MODEL), F32),
        grid=(B * S // tm,),
        in_specs=[_rows(tm, HALF_W, S), _rows(tm, HALF_W, S), _resident((D_MODEL, D_MODEL)),
                  _rows(tm, D_MODEL, S)] + _ffn_weight_specs(l, k) + [_resident((1, D_MODEL))],
        out_specs=_rows(tm, D_MODEL, S),
        compiler_params=_cparams(1),
        name="mix_ffn",
    )(a, b, wo, x, g, wg, wu, wd, gf)


def _ffn_odd_proj_kernel(x_ref, g_ref, wg_ref, wu_ref, wd_ref, gp_ref, wp_ref, xo_ref, zc_ref, u_ref,
                         *, fc, nb, ts):
    x = _swiglu_half_step(x_ref[...].reshape(nb * ts, D_MODEL), g_ref[...], wg_ref, wu_ref, wd_ref, fc)
    xo_ref[...] = x.reshape(nb, ts, D_MODEL)
    z = _dot(_rms(x, gp_ref[...]).astype(BF16), wp_ref[...])
    zc_ref[...] = z[:, :HALF_W].reshape(nb, ts, HALF_W)
    for sb in range(S5_SUPER):
        c0 = HALF_W + sb * S5_SUPER_CH
        for b in range(nb):
            u_ref[sb, pl.ds(b, ts, stride=nb), :] = z[b * ts:(b + 1) * ts, c0:c0 + S5_SUPER_CH]


def _ffn_odd_proj(x, g, wg, wu, wd, l, k, gp, wp, *, B, S, ts=128, fc=256):
    whole = lambda width: pl.BlockSpec((B, ts, width), lambda i: (0, i, 0))
    return pl.pallas_call(
        functools.partial(_ffn_odd_proj_kernel, fc=fc, nb=B, ts=ts),
        out_shape=(jax.ShapeDtypeStruct((B, S, D_MODEL), F32), jax.ShapeDtypeStruct((B, S, HALF_W), F32),
                   jax.ShapeDtypeStruct((S5_SUPER, S * B, S5_SUPER_CH), F32)),
        grid=(S // ts,),
        in_specs=[whole(D_MODEL)] + _ffn_weight_specs(l, k)
                 + [_resident((1, D_MODEL)), _resident((D_MODEL, 2 * HALF_W))],
        out_specs=(whole(D_MODEL), whole(HALF_W),
                   pl.BlockSpec((S5_SUPER, ts * B, S5_SUPER_CH), lambda i: (0, i, 0))),
        compiler_params=_cparams(1),
        name="ffn_odd_proj",
    )(x, g, wg, wu, wd, gp, wp)


def _ffn_even_proj_kernel(x_ref, g_ref, wg_ref, wu_ref, wd_ref, gp_ref, w_ref, cos_ref, sin_ref, blk_ref,
                          xo_ref, uv_ref, q_ref, gl_ref, kc_ref, vc_ref, ksa_ref, vsa_ref, kwa_ref, vwa_ref,
                          *, fc):
    x = _swiglu_half_step(x_ref[...], g_ref[...], wg_ref, wu_ref, wd_ref, fc)
    xo_ref[...] = x
    z = _dot(_rms(x, gp_ref[...]).astype(BF16), w_ref[...])
    cos = cos_ref[...]
    sin = sin_ref[...]
    lane = lax.broadcasted_iota(jnp.int32, cos.shape, 1)
    first_half = (lane % HEAD_DIM) < HALF
    low = lane < HEAD_DIM

    def slab(c0, rotary):
        xs = z[:, c0:c0 + LANES]
        if not rotary:
            return xs
        rot = jnp.where(first_half, pltpu.roll(xs, LANES - HALF, 1), pltpu.roll(xs, HALF, 1))
        return xs * cos + rot * sin

    uv_ref[...] = z[:, COL_U:COL_Q]
    for c in range(HALF_W // LANES):
        q_ref[:, c * LANES:(c + 1) * LANES] = slab(COL_Q + c * LANES, True)
    gl_ref[...] = z[:, COL_GL:COL_GL + LANES]
    kc_ref[...] = slab(COL_KC, False)
    vc_ref[...] = slab(COL_VC, False)
    ones = (lane == ONES_LANE).astype(F32)
    for c0, rotary, extra, dst in ((COL_KS, True, blk_ref[...], ksa_ref), (COL_VS, False, ones, vsa_ref),
                                   (COL_KW, True, 0.0, kwa_ref), (COL_VW, False, ones, vwa_ref)):
        xs = slab(c0, rotary)
        dst[0, 0] = jnp.where(low, xs, extra).astype(BF16)
        dst[0, 1] = jnp.where(low, pltpu.roll(xs, HEAD_DIM, 1), extra).astype(BF16)


def _ffn_even_proj(x, g, wg, wu, wd, l, k, gp, w, cos, sin, blk, *, B, S, tm=512, fc=256):
    spt = S // tm
    tok = lambda width: _rows(tm, width, S)
    pos = pl.BlockSpec((tm, LANES), lambda i: (i % spt, 0))
    grp = pl.BlockSpec((1, NSA_KV, tm, LANES), lambda i: (i // spt, 0, i % spt, 0))
    aug = jax.ShapeDtypeStruct((B, NSA_KV, S, LANES), BF16)
    tokens = lambda width: jax.ShapeDtypeStruct((B, S, width), F32)
    return pl.pallas_call(
        functools.partial(_ffn_even_proj_kernel, fc=fc),
        out_shape=(tokens(D_MODEL), tokens(2 * HALF_W), tokens(HALF_W), tokens(LANES), tokens(LANES),
                   tokens(LANES), aug, aug, aug, aug),
        grid=(B * S // tm,),
        in_specs=[tok(D_MODEL)] + _ffn_weight_specs(l, k)
                 + [_resident((1, D_MODEL)), _resident((D_MODEL, EVEN_IN_PAD)), pos, pos, pos],
        out_specs=(tok(D_MODEL), tok(2 * HALF_W), tok(HALF_W), tok(LANES), tok(LANES), tok(LANES),
                   grp, grp, grp, grp),
        compiler_params=_cparams(1),
        name="ffn_even_proj",
    )(x, g, wg, wu, wd, gp, w, cos, sin, blk)


def _gmlp_kernel(u_ref, v_ref, ws_ref, avg_ref, bias_ref, o_ref, *, chunks):
    rows_all = N_HEADS * GM_CHUNK
    r = lax.broadcasted_iota(jnp.int32, (rows_all, GM_CHUNK), 0) & (GM_CHUNK - 1)
    c = lax.broadcasted_iota(jnp.int32, (rows_all, GM_CHUNK), 1)
    w_all = jnp.where(c <= r, ws_ref[...], 0.0).astype(BF16)
    avg = avg_ref[...]

    def head_mean(x):
        return _dot(x.astype(BF16), avg)

    v = jax.nn.gelu(v_ref[...])
    d = v - head_mean(v)
    vn = (d * lax.rsqrt(head_mean(d * d) + LN_EPS)).astype(BF16)
    lane_head = lax.broadcasted_iota(jnp.int32, (GM_CHUNK, HALF_W), 1) // HEAD_DIM
    for ci in range(chunks):
        rows = slice(ci * GM_CHUNK, (ci + 1) * GM_CHUNK)
        res = _dot(w_all, vn[rows, :])
        s = res[0:GM_CHUNK, :]
        for h in range(1, N_HEADS):
            s = jnp.where(lane_head == h, res[h * GM_CHUNK:(h + 1) * GM_CHUNK, :], s)
        o_ref[rows, :] = jax.nn.gelu(u_ref[rows, :]) * (s + bias_ref[...])


def _gmlp(uv, ws, avg, bias, *, chunks=4):
    B, S, _ = uv.shape
    tm = chunks * GM_CHUNK
    spt = S // tm
    return pl.pallas_call(
        functools.partial(_gmlp_kernel, chunks=chunks),
        out_shape=jax.ShapeDtypeStruct((B, S, HALF_W), F32),
        grid=(B * S // tm,),
        in_specs=[pl.BlockSpec((None, tm, HALF_W), lambda i: (i // spt, i % spt, 0)),
                  pl.BlockSpec((None, tm, HALF_W), lambda i: (i // spt, i % spt, 1)),
                  _resident((N_HEADS * GM_CHUNK, GM_CHUNK)),
                  _resident((HALF_W, HALF_W)),
                  _resident((GM_CHUNK, HALF_W))],
        out_specs=_rows(tm, HALF_W, S),
        compiler_params=_cparams(1),
        name="gmlp",
    )(uv, uv, ws, avg, bias)


def _cmp_kernel(k_ref, v_ref, pe_ref, w1_ref, w2_ref, cos_ref, sin_ref, ko_ref, vo_ref, *, nblk):
    lane = lax.broadcasted_iota(jnp.int32, (nblk, LANES), 1)
    low = lane < HEAD_DIM
    first_half = (lane % HEAD_DIM) < HALF

    def both_groups(w):
        z = jnp.zeros_like(w)
        return jnp.concatenate([jnp.concatenate([w, z], axis=1), jnp.concatenate([z, w], axis=1)],
                               axis=0).astype(BF16)

    for which, (src, dst) in enumerate(((k_ref, ko_ref), (v_ref, vo_ref))):
        pre = jnp.zeros((nblk, LANES), F32)
        for r in range(CMP_STRIDE):
            rows = src[pl.ds(r, nblk, stride=CMP_STRIDE), :]
            nxt = pltpu.roll(rows, nblk - 1, 0)
            pre = pre + _dot((rows + pe_ref[which, r]).astype(BF16), both_groups(w1_ref[which, r]))
            pre = pre + _dot((nxt + pe_ref[which, CMP_STRIDE + r]).astype(BF16),
                             both_groups(w1_ref[which, CMP_STRIDE + r]))
        cmp = _dot(jax.nn.gelu(pre).astype(BF16), both_groups(w2_ref[which]))
        if which == 0:
            rot = jnp.where(first_half, pltpu.roll(cmp, LANES - HALF, 1), pltpu.roll(cmp, HALF, 1))
            cmp = cmp * cos_ref[...] + rot * sin_ref[...]
        dst[0, 0] = jnp.where(low, cmp, 0.0).astype(BF16)
        dst[0, 1] = jnp.where(low, pltpu.roll(cmp, HEAD_DIM, 1), 0.0).astype(BF16)


def _compress(kc, vc, pe, w1, w2, cosc, sinc, *, B, S):
    nblk = S // CMP_STRIDE
    out = jax.ShapeDtypeStruct((B, NSA_KV, nblk, LANES), BF16)
    out_spec = pl.BlockSpec((1, NSA_KV, nblk, LANES), lambda b: (b, 0, 0, 0))
    return pl.pallas_call(
        functools.partial(_cmp_kernel, nblk=nblk),
        out_shape=(out, out),
        grid=(B,),
        in_specs=[pl.BlockSpec((None, S, LANES), lambda b: (b, 0, 0)),
                  pl.BlockSpec((None, S, LANES), lambda b: (b, 0, 0)),
                  _resident((2, CMP_LEN, 1, LANES)),
                  _resident((2, CMP_LEN, HEAD_DIM, HEAD_DIM)),
                  _resident((2, HEAD_DIM, HEAD_DIM)),
                  _resident((nblk, LANES)),
                  _resident((nblk, LANES))],
        out_specs=(out_spec, out_spec),
        compiler_params=_cparams(1),
        name="nsa_compress",
    )(kc, vc, pe, w1, w2, cosc, sinc)


def _softmax_pv(pieces):
    ms = []
    for group in pieces:
        m = None
        for s, _ in group:
            pm = jnp.max(s, axis=-1, keepdims=True)
            m = pm if m is None else jnp.maximum(m, pm)
        ms.append(m)
    accs = [None] * len(pieces)
    for j in range(len(pieces[0])):
        for g, group in enumerate(pieces):
            s, vv = group[j]
            part = _dot(jnp.exp2(s - ms[g]).astype(BF16), vv)
            accs[g] = part if accs[g] is None else accs[g] + part
    return [acc / acc[:, ONES_LANE:ONES_LANE + 1] for acc in accs]


def _nsa_kernel(q_ref, gl_ref, ksa_ref, vsa_ref, kwa_ref, vwa_ref, kc_ref, vc_ref, ovt_ref, o_ref,
                osel_ref, *, S):
    TQ = Q_BLOCK
    R = NSA_HPG * TQ
    G = range(NSA_KV)
    ncmp = S // CMP_STRIDE
    nsel = S // SEL_BLOCK
    i = pl.program_id(1)
    t0 = i * TQ

    q = q_ref[...] * (HEAD_DIM ** -0.5 * LOG2E)
    gate = jax.nn.sigmoid(gl_ref[...])
    lane = lax.broadcasted_iota(jnp.int32, (TQ, LANES), 1)
    lane4 = lax.broadcasted_iota(jnp.int32, (R, LANES), 1)
    trow = t0 + lax.broadcasted_iota(jnp.int32, (TQ, 1), 0)

    def mask_heads(s, allowed):
        return jnp.concatenate([jnp.where(allowed, s[h * TQ:(h + 1) * TQ], NEG_INF) for h in range(NSA_HPG)],
                               axis=0)

    def stack_heads(g):
        heads = []
        for hq in range(NSA_HPG):
            h = g * NSA_HPG + hq
            slab = q[:, (h // 2) * LANES:(h // 2 + 1) * LANES]
            if h % 2 == 1:
                slab = pltpu.roll(slab, HEAD_DIM, 1)
            heads.append(jnp.where(lane < HEAD_DIM, slab, 0.0))
        return jnp.concatenate(heads, axis=0)

    q4 = [stack_heads(g) for g in G]
    q4b = [x.astype(BF16) for x in q4]

    s_c = [_dot_nt(q4b[g], kc_ref[0, g]) for g in G]
    k0 = pl.multiple_of(jnp.maximum(i - WINDOW // TQ, 0) * TQ, TQ)
    win = [[] for _ in G]
    span = WINDOW + TQ
    for lo, hi in [(lo, min(lo + WIN_PIECE, span)) for lo in range(0, span, WIN_PIECE)]:
        kpos = k0 + lo + lax.broadcasted_iota(jnp.int32, (TQ, hi - lo), 1)
        allowed = (kpos <= trow) & (trow - kpos < WINDOW)
        for g in G:
            s = _dot_nt(q4b[g], kwa_ref[0, g, pl.ds(k0 + lo, hi - lo), :])
            win[g].append((mask_heads(s, allowed), vwa_ref[0, g, pl.ds(k0 + lo, hi - lo), :]))

    ncol = lax.broadcasted_iota(jnp.int32, (TQ, ncmp), 1)
    valid_c = (ncol * CMP_STRIDE + (CMP_LEN - 1) <= trow) & (ncol < ncmp - 1)
    any_c = jnp.concatenate([(trow >= CMP_LEN - 1).astype(F32)] * NSA_HPG, axis=0)
    p_c = []
    for g in G:
        s = mask_heads(s_c[g], valid_c)
        e = jnp.exp2(s - jnp.max(s, axis=-1, keepdims=True))
        p_c.append(e / jnp.sum(e, axis=-1, keepdims=True) * any_c)
    o_cmp = [_dot(p_c[g].astype(BF16), vc_ref[0, g]) for g in G]

    ovt = ovt_ref[...]
    jrow = lax.broadcasted_iota(jnp.int32, (nsel, TQ), 0)
    tl = t0 + lax.broadcasted_iota(jnp.int32, (nsel, TQ), 1)
    cur = jnp.right_shift(tl, 6)
    forced = (jrow == 0) | (jrow == cur) | (jrow == cur - 1)
    causal_blk = jrow * SEL_BLOCK <= tl
    imp_t = []
    for g in G:
        p = p_c[g]
        p_hi, p_lo = _split_bf16((p[0:TQ] + p[TQ:2 * TQ]) + (p[2 * TQ:3 * TQ] + p[3 * TQ:4 * TQ]))
        imp = (_dot_nt(ovt, p_hi) + _dot_nt(ovt, p_lo))[0:nsel, :]
        imp_t.append(jnp.where(forced, SEL_FORCE, jnp.where(causal_blk, imp, -SEL_FORCE)))

    SUB = 8
    sub = lax.broadcasted_iota(jnp.int32, (SUB, TQ), 0)
    chunks = [[imp_t[g][c * SUB:(c + 1) * SUB] for c in range(nsel // SUB)] for g in G]
    rank = [[jnp.zeros((SUB, TQ), F32) for _ in range(nsel // SUB)] for _ in G]
    for jp in range(nsel):
        for g in G:
            row = imp_t[g][jp:jp + 1, :]
            for c, x in enumerate(chunks[g]):
                ge = jnp.where(row >= x, 1.0, 0.0)
                gt = jnp.where(row > x, 1.0, 0.0)
                if jp < c * SUB:
                    inc = ge
                elif jp >= (c + 1) * SUB:
                    inc = gt
                else:
                    inc = jnp.where(sub > jp - c * SUB, ge, gt)
                rank[g][c] = rank[g][c] + inc

    q4a = []
    for g in G:
        sel_t = (jnp.concatenate(rank[g], axis=0) < SEL_TOPN).astype(F32)
        if nsel < LANES:
            sel_t = jnp.concatenate([sel_t, jnp.zeros((LANES - nsel, TQ), F32)], axis=0)
        bias = (pltpu.roll(sel_t.T, HEAD_DIM, 1) - 1.0) * (-NEG_INF)
        q4a.append(jnp.where(lane4 < HEAD_DIM, q4[g], jnp.concatenate([bias] * NSA_HPG, axis=0)).astype(BF16))

    o_win = _softmax_pv(win)

    col_step = lax.broadcasted_iota(jnp.int32, (TQ, SEL_STEP), 1)
    for n in range(S // SEL_STEP):
        @pl.when(i // (SEL_STEP // TQ) == n)
        def _(n=n):
            pieces = [[] for _ in G]
            for j in range(n + 1):
                lo, hi = j * SEL_STEP, (j + 1) * SEL_STEP
                for g in G:
                    s = _dot_nt(q4a[g], ksa_ref[0, g, lo:hi, :])
                    if j == n:
                        s = mask_heads(s, lo + col_step <= trow)
                    pieces[g].append((s, vsa_ref[0, g, lo:hi, :]))
            for g, o in enumerate(_softmax_pv(pieces)):
                osel_ref[g] = o

    for g in G:
        o_s = osel_ref[g]
        outs = []
        for hq in range(NSA_HPG):
            h = g * NSA_HPG + hq
            rows = slice(hq * TQ, (hq + 1) * TQ)
            outs.append(gate[:, 3 * h:3 * h + 1] * o_cmp[g][rows] + gate[:, 3 * h + 1:3 * h + 2] * o_s[rows]
                        + gate[:, 3 * h + 2:3 * h + 3] * o_win[g][rows])
        for pair in range(NSA_HPG // 2):
            c0 = (g * NSA_HPG + 2 * pair) * HEAD_DIM
            o_ref[:, c0:c0 + LANES] = jnp.where(lane < HEAD_DIM, outs[2 * pair],
                                                pltpu.roll(outs[2 * pair + 1], HEAD_DIM, 1))


def _nsa(q, gl, ksa, vsa, kwa, vwa, kcmp, vcmp, ovt, *, B, S):
    nq = S // Q_BLOCK
    ncmp = S // CMP_STRIDE
    per_batch = lambda rows: pl.BlockSpec((1, NSA_KV, rows, LANES), lambda b, i: (b, 0, 0, 0))
    return pl.pallas_call(
        functools.partial(_nsa_kernel, S=S),
        out_shape=jax.ShapeDtypeStruct((B, S, HALF_W), F32),
        grid=(B, nq),
        in_specs=[pl.BlockSpec((None, Q_BLOCK, HALF_W), lambda b, i: (b, i, 0)),
                  pl.BlockSpec((None, Q_BLOCK, LANES), lambda b, i: (b, i, 0)),
                  per_batch(S), per_batch(S), per_batch(S), per_batch(S),
                  per_batch(ncmp), per_batch(ncmp),
                  _resident((LANES, ncmp))],
        out_specs=pl.BlockSpec((None, Q_BLOCK, HALF_W), lambda b, i: (b, i, 0)),
        scratch_shapes=[pltpu.VMEM((NSA_KV, NSA_HPG * Q_BLOCK, LANES), F32)],
        compiler_params=_cparams(2),
        name="nsa_attn",
    )(q, gl, ksa, vsa, kwa, vwa, kcmp, vcmp, ovt)


def _pool_kernel(z_ref, pw_ref, ps_ref, o_ref):
    S = z_ref.shape[0]
    row = lax.broadcasted_iota(jnp.int32, (S, POOL_GROUP), 0)
    for gi, w in enumerate(POOL_WINDOWS):
        cols = slice(gi * POOL_GROUP, (gi + 1) * POOL_GROUP)
        zg = z_ref[:, cols]
        acc = zg
        k = 1
        while k < w:
            acc = acc + jnp.where(row >= k, pltpu.roll(acc, k, 0), 0.0)
            k *= 2
        cnt = jnp.minimum(row + 1, w).astype(F32)
        pooled = acc / cnt - zg
        y = _dot(pooled.astype(BF16), pw_ref[gi].astype(BF16))
        o_ref[:, cols] = y * ps_ref[:, cols]


def _pool(z, pw, ps, *, B, S):
    return pl.pallas_call(
        _pool_kernel,
        out_shape=jax.ShapeDtypeStruct((B, S, HALF_W), F32),
        grid=(B,),
        in_specs=[pl.BlockSpec((None, S, HALF_W), lambda b: (b, 0, 0)),
                  _resident((len(POOL_WINDOWS), POOL_GROUP, POOL_GROUP)),
                  _resident((1, HALF_W))],
        out_specs=pl.BlockSpec((None, S, HALF_W), lambda b: (b, 0, 0)),
        compiler_params=_cparams(1),
        name="pool",
    )(z, pw, ps)


def _s5_disc_kernel(lr_ref, li_ref, ldt_ref, lbr_ref, lbi_ref, cr_ref, ci_ref):
    lr = lr_ref[...]
    li = li_ref[...]
    dt = jnp.exp(ldt_ref[...])
    mag = jnp.exp(lr * dt)
    ang = li * dt
    lbr = mag * jnp.cos(ang)
    lbi = mag * jnp.sin(ang)
    nr = lbr - 1.0
    den = lr * lr + li * li
    lbr_ref[...] = lbr
    lbi_ref[...] = lbi
    cr_ref[...] = (nr * lr + lbi * li) / den
    ci_ref[...] = (lbi * lr - nr * li) / den


def _s5_bbar_kernel(cr_ref, ci_ref, br_ref, bi_ref, or_ref, oi_ref):
    cr = cr_ref[...]
    ci = ci_ref[...]
    br = br_ref[...]
    bi = bi_ref[...]
    or_ref[...] = cr * br - ci * bi
    oi_ref[...] = cr * bi + ci * br


def _s5_params(lam_re, lam_im, log_dt, b_re, b_im):
    gp = jax.ShapeDtypeStruct((S5_GROUPS, S5_STATE), F32)
    lbr, lbi, cr, ci = pl.pallas_call(_s5_disc_kernel, out_shape=(gp, gp, gp, gp), name="s5_disc")(
        lam_re, lam_im, log_dt.reshape(S5_GROUPS, 1))
    flat = jax.ShapeDtypeStruct((S5_MODES, S5_GROUP_CH), F32)
    bbr, bbi = pl.pallas_call(_s5_bbar_kernel, out_shape=(flat, flat), name="s5_bbar")(
        cr.reshape(S5_MODES, 1), ci.reshape(S5_MODES, 1),
        b_re.reshape(S5_MODES, S5_GROUP_CH), b_im.reshape(S5_MODES, S5_GROUP_CH))
    return lbr, lbi, bbr, bbi


def _s5_kernel(u_ref, bsb_ref, lr_ref, li_ref, cre_ref, cim_ref, d_ref, wglu_ref, o_ref,
               xs_ref, st_ref, ys_ref, *, steps, nb):
    M = S5_MODES
    W = S5_SUPER_MODES

    @pl.when(pl.program_id(0) == 0)
    def _():
        st_ref[...] = jnp.zeros_like(st_ref)

    half = steps // 2 * nb
    halves = (slice(0, half), slice(half, 2 * half))

    def input_drive(rows, sb):
        bu = _dot(u_ref[sb, rows, :].astype(BF16), bsb_ref[sb])
        xs_ref[rows, sb * W:(sb + 1) * W] = bu[:, :W]
        xs_ref[rows, M + sb * W:M + (sb + 1) * W] = bu[:, W:]

    def readout(rows, sb):
        y = (_dot(xs_ref[rows, sb * W:(sb + 1) * W].astype(BF16), cre_ref[sb])
             - _dot(xs_ref[rows, M + sb * W:M + (sb + 1) * W].astype(BF16), cim_ref[sb]))
        return y + d_ref[:, sb * S5_SUPER_CH:(sb + 1) * S5_SUPER_CH] * u_ref[sb, rows, :]

    def scan(first, carry, side_work):
        xr, xi = carry
        lr = lr_ref[...]
        li = li_ref[...]
        every = (steps // 2) // len(side_work)
        done = []
        for t in range(steps // 2):
            if t % every == 0:
                done.append(side_work[t // every]())
            r0 = (first + t) * nb
            nr = lr * xr - li * xi + xs_ref[r0:r0 + nb, 0:M]
            ni = lr * xi + li * xr + xs_ref[r0:r0 + nb, M:2 * M]
            xs_ref[r0:r0 + nb, 0:M] = nr
            xs_ref[r0:r0 + nb, M:2 * M] = ni
            xr, xi = nr, ni
        return (xr, xi), done

    for sb in range(S5_SUPER):
        input_drive(halves[0], sb)
    carry = (st_ref[:, 0:M], st_ref[:, M:2 * M])
    carry, _ = scan(0, carry, [functools.partial(input_drive, halves[1], sb) for sb in range(S5_SUPER)])
    carry, ys0 = scan(steps // 2, carry, [functools.partial(readout, halves[0], sb) for sb in range(S5_SUPER)])
    st_ref[:, 0:M] = carry[0]
    st_ref[:, M:2 * M] = carry[1]
    ys1 = [readout(halves[1], sb) for sb in range(S5_SUPER)]

    for rows, ys in zip(halves, (ys0, ys1)):
        y = jax.nn.gelu(jnp.concatenate(ys, axis=1))
        ab = _dot(y.astype(BF16), wglu_ref[...])
        out = ab[:, :S5_WIDTH] * jax.nn.sigmoid(ab[:, S5_WIDTH:])
        for sb in range(S5_SUPER):
            ys_ref[sb, rows, :] = out[:, sb * S5_SUPER_CH:(sb + 1) * S5_SUPER_CH]
    for sb in range(S5_SUPER):
        for b in range(nb):
            o_ref[b, :, sb * S5_SUPER_CH:(sb + 1) * S5_SUPER_CH] = ys_ref[sb, pl.ds(b, steps, stride=nb), :]


def _s5(u_slabs, bsb, lr_b, li_b, cre, cim, d, wglu, *, B, S, steps=64):
    rows = steps * B
    return pl.pallas_call(
        functools.partial(_s5_kernel, steps=steps, nb=B),
        out_shape=jax.ShapeDtypeStruct((B, S, S5_WIDTH), F32),
        grid=(S // steps,),
        in_specs=[pl.BlockSpec((S5_SUPER, rows, S5_SUPER_CH), lambda i: (0, i, 0)),
                  _resident((S5_SUPER, S5_SUPER_CH, 2 * S5_SUPER_MODES)),
                  _resident((B, S5_MODES)),
                  _resident((B, S5_MODES)),
                  _resident((S5_SUPER, S5_SUPER_MODES, S5_SUPER_CH)),
                  _resident((S5_SUPER, S5_SUPER_MODES, S5_SUPER_CH)),
                  _resident((1, S5_WIDTH)),
                  _resident((S5_WIDTH, 2 * S5_WIDTH))],
        out_specs=pl.BlockSpec((B, steps, S5_WIDTH), lambda i: (0, i, 0)),
        scratch_shapes=[pltpu.VMEM((rows, 2 * S5_MODES), F32),
                        pltpu.VMEM((B, 2 * S5_MODES), F32),
                        pltpu.VMEM((S5_SUPER, rows, S5_SUPER_CH), F32)],
        compiler_params=_cparams(1),
        name="s5",
    )(u_slabs, bsb, lr_b, li_b, cre, cim, d, wglu)


def _rope_tables(S):
    inv = ROPE_THETA ** (-jnp.arange(HALF, dtype=F32) / HALF)
    ang = jnp.arange(S, dtype=F32)[:, None] * inv[None, :]
    cos, sin = jnp.cos(ang), jnp.sin(ang)
    cos_t = jnp.tile(cos, (1, LANES // HALF))
    sin_t = jnp.tile(jnp.concatenate([-sin, sin], axis=1), (1, LANES // HEAD_DIM))
    end_pos = (jnp.arange(S // CMP_STRIDE, dtype=F32) * CMP_STRIDE + (CMP_LEN - 1))[:, None] * inv[None, :]
    cosc = jnp.tile(jnp.cos(end_pos), (1, LANES // HALF))
    sinc = jnp.tile(jnp.concatenate([-jnp.sin(end_pos), jnp.sin(end_pos)], axis=1), (1, LANES // HEAD_DIM))
    return cos_t, sin_t, cosc, sinc


def _structure_constants(S):
    ncmp = S // CMP_STRIDE
    nsel = S // SEL_BLOCK
    assert nsel <= LANES - HEAD_DIM
    j = jnp.arange(LANES, dtype=jnp.int32)[:, None]
    n = jnp.arange(ncmp, dtype=jnp.int32)[None, :]
    ovt = ((n * CMP_STRIDE < (j + 1) * SEL_BLOCK) & (n * CMP_STRIDE + CMP_LEN > j * SEL_BLOCK)
           & (j < nsel) & (n < ncmp - 1)).astype(BF16)
    key_blk = (jnp.arange(S, dtype=jnp.int32) // SEL_BLOCK)[:, None]
    blk_tab = (key_blk + HEAD_DIM == jnp.arange(LANES, dtype=jnp.int32)[None, :]).astype(F32)
    c = jnp.arange(HALF_W, dtype=jnp.int32)
    avg = ((c[:, None] // HEAD_DIM == c[None, :] // HEAD_DIM).astype(F32) / HEAD_DIM).astype(BF16)
    return ovt, blk_tab, avg


def _even_mixer(x, ffn_args, g, w_in, gm_w_s, gm_b, pe, w1, w2, consts, *, B, S):
    cos_t, sin_t, cosc, sinc, ovt, blk_tab, avg = consts
    w_pad = jnp.pad(w_in, ((0, 0), (0, EVEN_IN_PAD - EVEN_IN))).astype(BF16)
    x, uv, q, gl, kc, vc, ksa, vsa, kwa, vwa = _ffn_even_proj(x, *ffn_args, g, w_pad, cos_t, sin_t, blk_tab,
                                                              B=B, S=S)
    out_a = _gmlp(uv, gm_w_s.reshape(N_HEADS * GM_CHUNK, GM_CHUNK), avg,
                  jnp.repeat(gm_b.T, HEAD_DIM, axis=1))
    kcmp, vcmp = _compress(kc, vc, jnp.tile(pe.reshape(2, CMP_LEN, 1, HEAD_DIM), (1, 1, 1, NSA_KV)),
                           w1.reshape(2, CMP_LEN, HEAD_DIM, HEAD_DIM), w2, cosc, sinc, B=B, S=S)
    out_b = _nsa(q, gl, ksa, vsa, kwa, vwa, kcmp, vcmp, ovt, B=B, S=S)
    return x, out_a, out_b


def _odd_mixer(x, ffn_args, g, w_in, pool_w, pool_scale, lam_re, lam_im, b_re, b_im, c_re, c_im,
               d_skip, log_dt, w_glu, *, B, S):
    x, zc, u_slabs = _ffn_odd_proj(x, *ffn_args, g, w_in.astype(BF16), B=B, S=S)
    y_c = _pool(zc, pool_w, pool_scale.reshape(1, HALF_W), B=B, S=S)

    lbr, lbi, bbr, bbi = _s5_params(lam_re, lam_im, log_dt, b_re, b_im)
    gps = S5_GROUPS // S5_SUPER
    eye = jnp.eye(gps, dtype=F32)
    blockdiag_in = lambda bb: jnp.einsum(
        'sgpc,gh->sgchp', bb.reshape(S5_SUPER, gps, S5_STATE, S5_GROUP_CH), eye
    ).reshape(S5_SUPER, S5_SUPER_CH, S5_SUPER_MODES)
    blockdiag_out = lambda cc: jnp.einsum(
        'sgcp,gh->sgphc', cc.reshape(S5_SUPER, gps, S5_GROUP_CH, S5_STATE), eye
    ).reshape(S5_SUPER, S5_SUPER_MODES, S5_SUPER_CH)
    bsb = jnp.concatenate([blockdiag_in(bbr), blockdiag_in(bbi)], axis=2).astype(BF16)
    bcast = lambda a: jnp.broadcast_to(a.reshape(1, S5_MODES), (B, S5_MODES))
    y_d = _s5(u_slabs, bsb, bcast(lbr), bcast(lbi),
              blockdiag_out(c_re).astype(BF16), blockdiag_out(c_im).astype(BF16),
              d_skip.reshape(1, S5_WIDTH), w_glu.astype(BF16), B=B, S=S)
    return x, y_c, y_d


def kernel(x, norm_w, ffn_w_gate, ffn_w_up, ffn_w_down, final_norm_w, ev_w_in, ev_w_out, gm_w_s, gm_b,
           nsa_cmp_pe, nsa_cmp_w1, nsa_cmp_w2, od_w_in, od_w_out, pool_w, pool_scale, s5_lam_re,
           s5_lam_im, s5_b_re, s5_b_im, s5_c_re, s5_c_im, s5_d, s5_log_dt, s5_w_glu):
    B, S, _ = x.shape
    consts = _rope_tables(S) + _structure_constants(S)
    wg, wu, wd = ffn_w_gate.astype(BF16), ffn_w_up.astype(BF16), ffn_w_down.astype(BF16)
    ffn_w = lambda l, k: (norm_w[l, 2 * k].reshape(1, D_MODEL), wg, wu, wd, l, k)
    gf = final_norm_w.reshape(1, D_MODEL)
    xt = x
    for l in range(DEPTH):
        g = norm_w[l, 1].reshape(1, D_MODEL)
        i = l // 2
        if l % 2 == 0:
            xt, a, b = _even_mixer(xt, ffn_w(l, 0), g, ev_w_in[i], gm_w_s[i], gm_b[i], nsa_cmp_pe[i],
                                   nsa_cmp_w1[i], nsa_cmp_w2[i], consts, B=B, S=S)
            w_out = ev_w_out[i]
        else:
            xt, a, b = _odd_mixer(xt, ffn_w(l, 0), g, od_w_in[i], pool_w[i], pool_scale[i], s5_lam_re[i],
                                  s5_lam_im[i], s5_b_re[i], s5_b_im[i], s5_c_re[i], s5_c_im[i], s5_d[i],
                                  s5_log_dt[i], s5_w_glu[i], B=B, S=S)
            w_out = od_w_out[i]
        xt = _mix_ffn(a, b, w_out.astype(BF16), xt, *ffn_w(l, 1), gf, final=(l == DEPTH - 1))
    return xt
```

```python
import functools

import jax
import jax.numpy as jnp
from jax import lax
from jax.experimental import pallas as pl
from jax.experimental.pallas import tpu as pltpu

F32 = jnp.float32
BF16 = jnp.bfloat16

D_MODEL = 1024
DEPTH = 4
HEAD_DIM = 64
HALF = HEAD_DIM // 2
ROPE_THETA = 10000.0
N_HEADS = 8
GM_CHUNK = 128
NSA_KV = 2
NSA_HPG = 4
CMP_LEN = 32
CMP_STRIDE = 16
SEL_BLOCK = 64
SEL_TOPN = 8
WINDOW = 512
Q_BLOCK = 256
POOL_WINDOWS = (2, 4, 8, 16)
POOL_GROUP = 128
S5_GROUPS = 32
S5_GROUP_CH = 16
S5_STATE = 64
S5_WIDTH = 512
S5_MODES = S5_GROUPS * S5_STATE
S5_SUPER = 4
S5_SUPER_CH = S5_WIDTH // S5_SUPER
S5_SUPER_MODES = S5_MODES // S5_SUPER
FFN_DIM = 2816
RMS_EPS = 1e-6
LN_EPS = 1e-5
NEG_INF = -1e30
SEL_FORCE = 1e4
LOG2E = 1.4426950408889634

LANES = 128
HALF_W = 512
EVEN_IN = 2328
EVEN_IN_PAD = 2432
COL_U, COL_V, COL_Q = 0, 512, 1024
COL_KC, COL_VC, COL_KS, COL_VS, COL_KW, COL_VW, COL_GL = 1536, 1664, 1792, 1920, 2048, 2176, 2304
SEL_STEP = 256
WIN_PIECE = 256
ONES_LANE = HEAD_DIM

VMEM_LIMIT = 56 * 1024 * 1024


def _cparams(n_axes):
    return pltpu.CompilerParams(dimension_semantics=("arbitrary",) * n_axes,
                                vmem_limit_bytes=VMEM_LIMIT)


def _resident(shape):
    nd = len(shape)
    return pl.BlockSpec(shape, lambda *_: (0,) * nd, pipeline_mode=pl.Buffered(1))


def _rows(tm, width, S):
    spt = S // tm
    return pl.BlockSpec((None, tm, width), lambda i: (i // spt, i % spt, 0))


def _rms(x, g):
    ms = jnp.mean(x * x, axis=-1, keepdims=True)
    return x * lax.rsqrt(ms + RMS_EPS) * g


def _dot(a, b):
    return jnp.dot(a, b, preferred_element_type=F32)


def _dot_nt(a, b):
    return lax.dot_general(a, b, (((1,), (1,)), ((), ())), preferred_element_type=F32)


def _split_bf16(x):
    hi = x.astype(BF16)
    return hi, (x - hi.astype(F32)).astype(BF16)


def _swiglu_half_step(x, g, wg_ref, wu_ref, wd_ref, fc):
    h = _rms(x, g).astype(BF16)
    acc = jnp.zeros_like(x)
    for c in range(FFN_DIM // fc):
        sl = slice(c * fc, (c + 1) * fc)
        a = _dot(h, wg_ref[:, sl])
        b = _dot(h, wu_ref[:, sl])
        t = (a * jax.nn.sigmoid(a) * b).astype(BF16)
        acc = acc + _dot(t, wd_ref[sl, :])
    return x + 0.5 * acc


def _mix_ffn_kernel(a_ref, b_ref, wo_ref, x_ref, g_ref, wg_ref, wu_ref, wd_ref, gf_ref, o_ref, *, fc, final):
    x = x_ref[...] + _dot(a_ref[...].astype(BF16), wo_ref[0:HALF_W, :])
    x = x + _dot(b_ref[...].astype(BF16), wo_ref[HALF_W:2 * HALF_W, :])
    x = _swiglu_half_step(x, g_ref[...], wg_ref, wu_ref, wd_ref, fc)
    o_ref[...] = _rms(x, gf_ref[...]) if final else x


def _ffn_weight_specs(l, k):
    pick = lambda rows, cols: pl.BlockSpec((None, None, rows, cols), lambda *_: (l, k, 0, 0),
                                           pipeline_mode=pl.Buffered(1))
    return [_resident((1, D_MODEL)), pick(D_MODEL, FFN_DIM), pick(D_MODEL, FFN_DIM), pick(FFN_DIM, D_MODEL)]


def _mix_ffn(a, b, wo, x, g, wg, wu, wd, l, k, gf, *, final, tm=512, fc=256):
    B, S, _ = x.shape
    return pl.pallas_call(
        functools.partial(_mix_ffn_kernel, fc=fc, final=final),
        out_shape=jax.ShapeDtypeStruct((B, S, D_MODEL), F32),
        grid=(B * S // tm,),
        in_specs=[_rows(tm, HALF_W, S), _rows(tm, HALF_W, S), _resident((D_MODEL, D_MODEL)),
                  _rows(tm, D_MODEL, S)] + _ffn_weight_specs(l, k) + [_resident((1, D_MODEL))],
        out_specs=_rows(tm, D_MODEL, S),
        compiler_params=_cparams(1),
        name="mix_ffn",
    )(a, b, wo, x, g, wg, wu, wd, gf)


def _ffn_odd_proj_kernel(x_ref, g_ref, wg_ref, wu_ref, wd_ref, gp_ref, wp_ref, xo_ref, zc_ref, u_ref,
                         *, fc, nb, ts):
    x = _swiglu_half_step(x_ref[...].reshape(nb * ts, D_MODEL), g_ref[...], wg_ref, wu_ref, wd_ref, fc)
    xo_ref[...] = x.reshape(nb, ts, D_MODEL)
    z = _dot(_rms(x, gp_ref[...]).astype(BF16), wp_ref[...])
    zc_ref[...] = z[:, :HALF_W].reshape(nb, ts, HALF_W)
    for sb in range(S5_SUPER):
        c0 = HALF_W + sb * S5_SUPER_CH
        for b in range(nb):
            u_ref[sb, pl.ds(b, ts, stride=nb), :] = z[b * ts:(b + 1) * ts, c0:c0 + S5_SUPER_CH]


def _ffn_odd_proj(x, g, wg, wu, wd, l, k, gp, wp, *, B, S, ts=128, fc=256):
    whole = lambda width: pl.BlockSpec((B, ts, width), lambda i: (0, i, 0))
    return pl.pallas_call(
        functools.partial(_ffn_odd_proj_kernel, fc=fc, nb=B, ts=ts),
        out_shape=(jax.ShapeDtypeStruct((B, S, D_MODEL), F32), jax.ShapeDtypeStruct((B, S, HALF_W), F32),
                   jax.ShapeDtypeStruct((S5_SUPER, S * B, S5_SUPER_CH), F32)),
        grid=(S // ts,),
        in_specs=[whole(D_MODEL)] + _ffn_weight_specs(l, k)
                 + [_resident((1, D_MODEL)), _resident((D_MODEL, 2 * HALF_W))],
        out_specs=(whole(D_MODEL), whole(HALF_W),
                   pl.BlockSpec((S5_SUPER, ts * B, S5_SUPER_CH), lambda i: (0, i, 0))),
        compiler_params=_cparams(1),
        name="ffn_odd_proj",
    )(x, g, wg, wu, wd, gp, wp)


def _ffn_even_proj_kernel(x_ref, g_ref, wg_ref, wu_ref, wd_ref, gp_ref, w_ref, cos_ref, sin_ref, blk_ref,
                          xo_ref, uv_ref, q_ref, gl_ref, kc_ref, vc_ref, ksa_ref, vsa_ref, kwa_ref, vwa_ref,
                          *, fc):
    x = _swiglu_half_step(x_ref[...], g_ref[...], wg_ref, wu_ref, wd_ref, fc)
    xo_ref[...] = x
    z = _dot(_rms(x, gp_ref[...]).astype(BF16), w_ref[...])
    cos = cos_ref[...]
    sin = sin_ref[...]
    lane = lax.broadcasted_iota(jnp.int32, cos.shape, 1)
    first_half = (lane % HEAD_DIM) < HALF
    low = lane < HEAD_DIM

    def slab(c0, rotary):
        xs = z[:, c0:c0 + LANES]
        if not rotary:
            return xs
        rot = jnp.where(first_half, pltpu.roll(xs, LANES - HALF, 1), pltpu.roll(xs, HALF, 1))
        return xs * cos + rot * sin

    uv_ref[...] = z[:, COL_U:COL_Q]
    for c in range(HALF_W // LANES):
        q_ref[:, c * LANES:(c + 1) * LANES] = slab(COL_Q + c * LANES, True)
    gl_ref[...] = z[:, COL_GL:COL_GL + LANES]
    kc_ref[...] = slab(COL_KC, False)
    vc_ref[...] = slab(COL_VC, False)
    ones = (lane == ONES_LANE).astype(F32)
    for c0, rotary, extra, dst in ((COL_KS, True, blk_ref[...], ksa_ref), (COL_VS, False, ones, vsa_ref),
                                   (COL_KW, True, 0.0, kwa_ref), (COL_VW, False, ones, vwa_ref)):
        xs = slab(c0, rotary)
        dst[0, 0] = jnp.where(low, xs, extra).astype(BF16)
        dst[0, 1] = jnp.where(low, pltpu.roll(xs, HEAD_DIM, 1), extra).astype(BF16)


def _ffn_even_proj(x, g, wg, wu, wd, l, k, gp, w, cos, sin, blk, *, B, S, tm=512, fc=256):
    spt = S // tm
    tok = lambda width: _rows(tm, width, S)
    pos = pl.BlockSpec((tm, LANES), lambda i: (i % spt, 0))
    grp = pl.BlockSpec((1, NSA_KV, tm, LANES), lambda i: (i // spt, 0, i % spt, 0))
    aug = jax.ShapeDtypeStruct((B, NSA_KV, S, LANES), BF16)
    tokens = lambda width: jax.ShapeDtypeStruct((B, S, width), F32)
    return pl.pallas_call(
        functools.partial(_ffn_even_proj_kernel, fc=fc),
        out_shape=(tokens(D_MODEL), tokens(2 * HALF_W), tokens(HALF_W), tokens(LANES), tokens(LANES),
                   tokens(LANES), aug, aug, aug, aug),
        grid=(B * S // tm,),
        in_specs=[tok(D_MODEL)] + _ffn_weight_specs(l, k)
                 + [_resident((1, D_MODEL)), _resident((D_MODEL, EVEN_IN_PAD)), pos, pos, pos],
        out_specs=(tok(D_MODEL), tok(2 * HALF_W), tok(HALF_W), tok(LANES), tok(LANES), tok(LANES),
                   grp, grp, grp, grp),
        compiler_params=_cparams(1),
        name="ffn_even_proj",
    )(x, g, wg, wu, wd, gp, w, cos, sin, blk)


def _gmlp_kernel(u_ref, v_ref, ws_ref, avg_ref, bias_ref, o_ref, *, chunks):
    rows_all = N_HEADS * GM_CHUNK
    r = lax.broadcasted_iota(jnp.int32, (rows_all, GM_CHUNK), 0) & (GM_CHUNK - 1)
    c = lax.broadcasted_iota(jnp.int32, (rows_all, GM_CHUNK), 1)
    w_all = jnp.where(c <= r, ws_ref[...], 0.0).astype(BF16)
    avg = avg_ref[...]

    def head_mean(x):
        return _dot(x.astype(BF16), avg)

    v = jax.nn.gelu(v_ref[...])
    d = v - head_mean(v)
    vn = (d * lax.rsqrt(head_mean(d * d) + LN_EPS)).astype(BF16)
    lane_head = lax.broadcasted_iota(jnp.int32, (GM_CHUNK, HALF_W), 1) // HEAD_DIM
    for ci in range(chunks):
        rows = slice(ci * GM_CHUNK, (ci + 1) * GM_CHUNK)
        res = _dot(w_all, vn[rows, :])
        s = res[0:GM_CHUNK, :]
        for h in range(1, N_HEADS):
            s = jnp.where(lane_head == h, res[h * GM_CHUNK:(h + 1) * GM_CHUNK, :], s)
        o_ref[rows, :] = jax.nn.gelu(u_ref[rows, :]) * (s + bias_ref[...])


def _gmlp(uv, ws, avg, bias, *, chunks=4):
    B, S, _ = uv.shape
    tm = chunks * GM_CHUNK
    spt = S // tm
    return pl.pallas_call(
        functools.partial(_gmlp_kernel, chunks=chunks),
        out_shape=jax.ShapeDtypeStruct((B, S, HALF_W), F32),
        grid=(B * S // tm,),
        in_specs=[pl.BlockSpec((None, tm, HALF_W), lambda i: (i // spt, i % spt, 0)),
                  pl.BlockSpec((None, tm, HALF_W), lambda i: (i // spt, i % spt, 1)),
                  _resident((N_HEADS * GM_CHUNK, GM_CHUNK)),
                  _resident((HALF_W, HALF_W)),
                  _resident((GM_CHUNK, HALF_W))],
        out_specs=_rows(tm, HALF_W, S),
        compiler_params=_cparams(1),
        name="gmlp",
    )(uv, uv, ws, avg, bias)


def _cmp_kernel(k_ref, v_ref, pe_ref, w1_ref, w2_ref, cos_ref, sin_ref, ko_ref, vo_ref, *, nblk):
    lane = lax.broadcasted_iota(jnp.int32, (nblk, LANES), 1)
    low = lane < HEAD_DIM
    first_half = (lane % HEAD_DIM) < HALF

    def both_groups(w):
        z = jnp.zeros_like(w)
        return jnp.concatenate([jnp.concatenate([w, z], axis=1), jnp.concatenate([z, w], axis=1)],
                               axis=0).astype(BF16)

    for which, (src, dst) in enumerate(((k_ref, ko_ref), (v_ref, vo_ref))):
        pre = jnp.zeros((nblk, LANES), F32)
        for r in range(CMP_STRIDE):
            rows = src[pl.ds(r, nblk, stride=CMP_STRIDE), :]
            nxt = pltpu.roll(rows, nblk - 1, 0)
            pre = pre + _dot((rows + pe_ref[which, r]).astype(BF16), both_groups(w1_ref[which, r]))
            pre = pre + _dot((nxt + pe_ref[which, CMP_STRIDE + r]).astype(BF16),
                             both_groups(w1_ref[which, CMP_STRIDE + r]))
        cmp = _dot(jax.nn.gelu(pre).astype(BF16), both_groups(w2_ref[which]))
        if which == 0:
            rot = jnp.where(first_half, pltpu.roll(cmp, LANES - HALF, 1), pltpu.roll(cmp, HALF, 1))
            cmp = cmp * cos_ref[...] + rot * sin_ref[...]
        dst[0, 0] = jnp.where(low, cmp, 0.0).astype(BF16)
        dst[0, 1] = jnp.where(low, pltpu.roll(cmp, HEAD_DIM, 1), 0.0).astype(BF16)


def _compress(kc, vc, pe, w1, w2, cosc, sinc, *, B, S):
    nblk = S // CMP_STRIDE
    out = jax.ShapeDtypeStruct((B, NSA_KV, nblk, LANES), BF16)
    out_spec = pl.BlockSpec((1, NSA_KV, nblk, LANES), lambda b: (b, 0, 0, 0))
    return pl.pallas_call(
        functools.partial(_cmp_kernel, nblk=nblk),
        out_shape=(out, out),
        grid=(B,),
        in_specs=[pl.BlockSpec((None, S, LANES), lambda b: (b, 0, 0)),
                  pl.BlockSpec((None, S, LANES), lambda b: (b, 0, 0)),
                  _resident((2, CMP_LEN, 1, LANES)),
                  _resident((2, CMP_LEN, HEAD_DIM, HEAD_DIM)),
                  _resident((2, HEAD_DIM, HEAD_DIM)),
                  _resident((nblk, LANES)),
                  _resident((nblk, LANES))],
        out_specs=(out_spec, out_spec),
        compiler_params=_cparams(1),
        name="nsa_compress",
    )(kc, vc, pe, w1, w2, cosc, sinc)


def _softmax_pv(pieces):
    ms = []
    for group in pieces:
        m = None
        for s, _ in group:
            pm = jnp.max(s, axis=-1, keepdims=True)
            m = pm if m is None else jnp.maximum(m, pm)
        ms.append(m)
    accs = [None] * len(pieces)
    for j in range(len(pieces[0])):
        for g, group in enumerate(pieces):
            s, vv = group[j]
            part = _dot(jnp.exp2(s - ms[g]).astype(BF16), vv)
            accs[g] = part if accs[g] is None else accs[g] + part
    return accs


def _nsa_kernel(q_ref, gl_ref, ksa_ref, vsa_ref, kwa_ref, vwa_ref, kc_ref, vc_ref, ovt_ref, o_ref,
                osel_ref, *, S):
    TQ = Q_BLOCK
    R = NSA_HPG * TQ
    G = range(NSA_KV)
    ncmp = S // CMP_STRIDE
    nsel = S // SEL_BLOCK
    i = pl.program_id(1)
    t0 = i * TQ

    q = q_ref[...] * (HEAD_DIM ** -0.5 * LOG2E)
    gate = jax.nn.sigmoid(gl_ref[...])
    lane = lax.broadcasted_iota(jnp.int32, (TQ, LANES), 1)
    lane4 = lax.broadcasted_iota(jnp.int32, (R, LANES), 1)
    trow = t0 + lax.broadcasted_iota(jnp.int32, (TQ, 1), 0)

    def mask_heads(s, allowed):
        return jnp.concatenate([jnp.where(allowed, s[h * TQ:(h + 1) * TQ], NEG_INF) for h in range(NSA_HPG)],
                               axis=0)

    def stack_heads(g):
        heads = []
        for hq in range(NSA_HPG):
            h = g * NSA_HPG + hq
            slab = q[:, (h // 2) * LANES:(h // 2 + 1) * LANES]
            if h % 2 == 1:
                slab = pltpu.roll(slab, HEAD_DIM, 1)
            heads.append(jnp.where(lane < HEAD_DIM, slab, 0.0))
        return jnp.concatenate(heads, axis=0)

    q4 = [stack_heads(g) for g in G]
    q4b = [x.astype(BF16) for x in q4]

    s_c = [_dot_nt(q4b[g], kc_ref[0, g]) for g in G]
    k0 = pl.multiple_of(jnp.maximum(i - WINDOW // TQ, 0) * TQ, TQ)
    win = [[] for _ in G]
    span = WINDOW + TQ
    for lo, hi in [(lo, min(lo + WIN_PIECE, span)) for lo in range(0, span, WIN_PIECE)]:
        kpos = k0 + lo + lax.broadcasted_iota(jnp.int32, (TQ, hi - lo), 1)
        allowed = (kpos <= trow) & (trow - kpos < WINDOW)
        for g in G:
            s = _dot_nt(q4b[g], kwa_ref[0, g, pl.ds(k0 + lo, hi - lo), :])
            win[g].append((mask_heads(s, allowed), vwa_ref[0, g, pl.ds(k0 + lo, hi - lo), :]))

    ncol = lax.broadcasted_iota(jnp.int32, (TQ, ncmp), 1)
    valid_c = (ncol * CMP_STRIDE + (CMP_LEN - 1) <= trow) & (ncol < ncmp - 1)
    any_c = jnp.concatenate([(trow >= CMP_LEN - 1).astype(F32)] * NSA_HPG, axis=0)
    p_c = []
    for g in G:
        s = mask_heads(s_c[g], valid_c)
        e = jnp.exp2(s - jnp.max(s, axis=-1, keepdims=True))
        p_c.append(e / jnp.sum(e, axis=-1, keepdims=True) * any_c)
    o_cmp = [_dot(p_c[g].astype(BF16), vc_ref[0, g]) for g in G]

    ovt = ovt_ref[...]
    jrow = lax.broadcasted_iota(jnp.int32, (nsel, TQ), 0)
    tl = t0 + lax.broadcasted_iota(jnp.int32, (nsel, TQ), 1)
    cur = jnp.right_shift(tl, 6)
    forced = (jrow == 0) | (jrow == cur) | (jrow == cur - 1)
    causal_blk = jrow * SEL_BLOCK <= tl
    imp_t = []
    for g in G:
        p = p_c[g]
        p_hi, p_lo = _split_bf16((p[0:TQ] + p[TQ:2 * TQ]) + (p[2 * TQ:3 * TQ] + p[3 * TQ:4 * TQ]))
        imp = (_dot_nt(ovt, p_hi) + _dot_nt(ovt, p_lo))[0:nsel, :]
        imp_t.append(jnp.where(forced, SEL_FORCE, jnp.where(causal_blk, imp, -SEL_FORCE)))

    SUB = 8
    sub = lax.broadcasted_iota(jnp.int32, (SUB, TQ), 0)
    chunks = [[imp_t[g][c * SUB:(c + 1) * SUB] for c in range(nsel // SUB)] for g in G]
    rank = [[jnp.zeros((SUB, TQ), F32) for _ in range(nsel // SUB)] for _ in G]
    for jp in range(nsel):
        for g in G:
            row = imp_t[g][jp:jp + 1, :]
            for c, x in enumerate(chunks[g]):
                ge = jnp.where(row >= x, 1.0, 0.0)
                gt = jnp.where(row > x, 1.0, 0.0)
                if jp < c * SUB:
                    inc = ge
                elif jp >= (c + 1) * SUB:
                    inc = gt
                else:
                    inc = jnp.where(sub > jp - c * SUB, ge, gt)
                rank[g][c] = rank[g][c] + inc

    q4a = []
    for g in G:
        sel_t = (jnp.concatenate(rank[g], axis=0) < SEL_TOPN).astype(F32)
        if nsel < LANES:
            sel_t = jnp.concatenate([sel_t, jnp.zeros((LANES - nsel, TQ), F32)], axis=0)
        bias = (pltpu.roll(sel_t.T, HEAD_DIM, 1) - 1.0) * (-NEG_INF)
        q4a.append(jnp.where(lane4 < HEAD_DIM, q4[g], jnp.concatenate([bias] * NSA_HPG, axis=0)).astype(BF16))

    o_win = _softmax_pv(win)

    col_step = lax.broadcasted_iota(jnp.int32, (TQ, SEL_STEP), 1)
    for n in range(S // SEL_STEP):
        @pl.when(i // (SEL_STEP // TQ) == n)
        def _(n=n):
            pieces = [[] for _ in G]
            for j in range(n + 1):
                lo, hi = j * SEL_STEP, (j + 1) * SEL_STEP
                for g in G:
                    s = _dot_nt(q4a[g], ksa_ref[0, g, lo:hi, :])
                    if j == n:
                        s = mask_heads(s, lo + col_step <= trow)
                    pieces[g].append((s, vsa_ref[0, g, lo:hi, :]))
            for g, o in enumerate(_softmax_pv(pieces)):
                osel_ref[g] = o

    def gate_col(g, branch):
        cols = [3 * (g * NSA_HPG + hq) + branch for hq in range(NSA_HPG)]
        return jnp.concatenate([gate[:, c:c + 1] for c in cols], axis=0)

    for g in G:
        acc_s, acc_w = osel_ref[g], o_win[g]
        mixed = (gate_col(g, 0) * o_cmp[g]
                 + (gate_col(g, 1) / acc_s[:, ONES_LANE:ONES_LANE + 1]) * acc_s
                 + (gate_col(g, 2) / acc_w[:, ONES_LANE:ONES_LANE + 1]) * acc_w)
        for pair in range(NSA_HPG // 2):
            c0 = (g * NSA_HPG + 2 * pair) * HEAD_DIM
            first = mixed[2 * pair * TQ:(2 * pair + 1) * TQ]
            second = mixed[(2 * pair + 1) * TQ:(2 * pair + 2) * TQ]
            o_ref[:, c0:c0 + LANES] = jnp.where(lane < HEAD_DIM, first, pltpu.roll(second, HEAD_DIM, 1))


def _nsa(q, gl, ksa, vsa, kwa, vwa, kcmp, vcmp, ovt, *, B, S):
    nq = S // Q_BLOCK
    ncmp = S // CMP_STRIDE
    per_batch = lambda rows: pl.BlockSpec((1, NSA_KV, rows, LANES), lambda b, i: (b, 0, 0, 0))
    return pl.pallas_call(
        functools.partial(_nsa_kernel, S=S),
        out_shape=jax.ShapeDtypeStruct((B, S, HALF_W), F32),
        grid=(B, nq),
        in_specs=[pl.BlockSpec((None, Q_BLOCK, HALF_W), lambda b, i: (b, i, 0)),
                  pl.BlockSpec((None, Q_BLOCK, LANES), lambda b, i: (b, i, 0)),
                  per_batch(S), per_batch(S), per_batch(S), per_batch(S),
                  per_batch(ncmp), per_batch(ncmp),
                  _resident((LANES, ncmp))],
        out_specs=pl.BlockSpec((None, Q_BLOCK, HALF_W), lambda b, i: (b, i, 0)),
        scratch_shapes=[pltpu.VMEM((NSA_KV, NSA_HPG * Q_BLOCK, LANES), F32)],
        compiler_params=_cparams(2),
        name="nsa_attn",
    )(q, gl, ksa, vsa, kwa, vwa, kcmp, vcmp, ovt)


def _pool_kernel(z_ref, pw_ref, ps_ref, o_ref):
    S = z_ref.shape[0]
    row = lax.broadcasted_iota(jnp.int32, (S, POOL_GROUP), 0)
    for gi, w in enumerate(POOL_WINDOWS):
        cols = slice(gi * POOL_GROUP, (gi + 1) * POOL_GROUP)
        zg = z_ref[:, cols]
        acc = zg
        k = 1
        while k < w:
            acc = acc + jnp.where(row >= k, pltpu.roll(acc, k, 0), 0.0)
            k *= 2
        cnt = jnp.minimum(row + 1, w).astype(F32)
        pooled = acc / cnt - zg
        y = _dot(pooled.astype(BF16), pw_ref[gi].astype(BF16))
        o_ref[:, cols] = y * ps_ref[:, cols]


def _pool(z, pw, ps, *, B, S):
    return pl.pallas_call(
        _pool_kernel,
        out_shape=jax.ShapeDtypeStruct((B, S, HALF_W), F32),
        grid=(B,),
        in_specs=[pl.BlockSpec((None, S, HALF_W), lambda b: (b, 0, 0)),
                  _resident((len(POOL_WINDOWS), POOL_GROUP, POOL_GROUP)),
                  _resident((1, HALF_W))],
        out_specs=pl.BlockSpec((None, S, HALF_W), lambda b: (b, 0, 0)),
        compiler_params=_cparams(1),
        name="pool",
    )(z, pw, ps)


def _s5_disc_kernel(lr_ref, li_ref, ldt_ref, lbr_ref, lbi_ref, cr_ref, ci_ref):
    lr = lr_ref[...]
    li = li_ref[...]
    dt = jnp.exp(ldt_ref[...])
    mag = jnp.exp(lr * dt)
    ang = li * dt
    lbr = mag * jnp.cos(ang)
    lbi = mag * jnp.sin(ang)
    nr = lbr - 1.0
    den = lr * lr + li * li
    lbr_ref[...] = lbr
    lbi_ref[...] = lbi
    cr_ref[...] = (nr * lr + lbi * li) / den
    ci_ref[...] = (lbi * lr - nr * li) / den


def _s5_bbar_kernel(cr_ref, ci_ref, br_ref, bi_ref, or_ref, oi_ref):
    cr = cr_ref[...]
    ci = ci_ref[...]
    br = br_ref[...]
    bi = bi_ref[...]
    or_ref[...] = cr * br - ci * bi
    oi_ref[...] = cr * bi + ci * br


def _s5_params(lam_re, lam_im, log_dt, b_re, b_im):
    gp = jax.ShapeDtypeStruct((S5_GROUPS, S5_STATE), F32)
    lbr, lbi, cr, ci = pl.pallas_call(_s5_disc_kernel, out_shape=(gp, gp, gp, gp), name="s5_disc")(
        lam_re, lam_im, log_dt.reshape(S5_GROUPS, 1))
    flat = jax.ShapeDtypeStruct((S5_MODES, S5_GROUP_CH), F32)
    bbr, bbi = pl.pallas_call(_s5_bbar_kernel, out_shape=(flat, flat), name="s5_bbar")(
        cr.reshape(S5_MODES, 1), ci.reshape(S5_MODES, 1),
        b_re.reshape(S5_MODES, S5_GROUP_CH), b_im.reshape(S5_MODES, S5_GROUP_CH))
    return lbr, lbi, bbr, bbi


def _s5_kernel(u_ref, bsb_ref, lr_ref, li_ref, cre_ref, cim_ref, d_ref, wglu_ref, o_ref,
               xs_ref, st_ref, ys_ref, *, steps, nb):
    M = S5_MODES
    W = S5_SUPER_MODES

    @pl.when(pl.program_id(0) == 0)
    def _():
        st_ref[...] = jnp.zeros_like(st_ref)

    half = steps // 2 * nb
    halves = (slice(0, half), slice(half, 2 * half))

    def input_drive(rows, sb):
        bu = _dot(u_ref[sb, rows, :].astype(BF16), bsb_ref[sb])
        xs_ref[rows, sb * W:(sb + 1) * W] = bu[:, :W]
        xs_ref[rows, M + sb * W:M + (sb + 1) * W] = bu[:, W:]

    def readout(rows, sb):
        y = (_dot(xs_ref[rows, sb * W:(sb + 1) * W].astype(BF16), cre_ref[sb])
             - _dot(xs_ref[rows, M + sb * W:M + (sb + 1) * W].astype(BF16), cim_ref[sb]))
        return y + d_ref[:, sb * S5_SUPER_CH:(sb + 1) * S5_SUPER_CH] * u_ref[sb, rows, :]

    def scan(first, carry, side_work):
        xr, xi = carry
        lr = lr_ref[...]
        li = li_ref[...]
        every = (steps // 2) // len(side_work)
        done = []
        for t in range(steps // 2):
            if t % every == 0:
                done.append(side_work[t // every]())
            r0 = (first + t) * nb
            nr = lr * xr - li * xi + xs_ref[r0:r0 + nb, 0:M]
            ni = lr * xi + li * xr + xs_ref[r0:r0 + nb, M:2 * M]
            xs_ref[r0:r0 + nb, 0:M] = nr
            xs_ref[r0:r0 + nb, M:2 * M] = ni
            xr, xi = nr, ni
        return (xr, xi), done

    for sb in range(S5_SUPER):
        input_drive(halves[0], sb)
    carry = (st_ref[:, 0:M], st_ref[:, M:2 * M])
    carry, _ = scan(0, carry, [functools.partial(input_drive, halves[1], sb) for sb in range(S5_SUPER)])
    carry, ys0 = scan(steps // 2, carry, [functools.partial(readout, halves[0], sb) for sb in range(S5_SUPER)])
    st_ref[:, 0:M] = carry[0]
    st_ref[:, M:2 * M] = carry[1]
    ys1 = [readout(halves[1], sb) for sb in range(S5_SUPER)]

    for rows, ys in zip(halves, (ys0, ys1)):
        y = jax.nn.gelu(jnp.concatenate(ys, axis=1))
        ab = _dot(y.astype(BF16), wglu_ref[...])
        out = ab[:, :S5_WIDTH] * jax.nn.sigmoid(ab[:, S5_WIDTH:])
        for sb in range(S5_SUPER):
            ys_ref[sb, rows, :] = out[:, sb * S5_SUPER_CH:(sb + 1) * S5_SUPER_CH]
    for sb in range(S5_SUPER):
        for b in range(nb):
            o_ref[b, :, sb * S5_SUPER_CH:(sb + 1) * S5_SUPER_CH] = ys_ref[sb, pl.ds(b, steps, stride=nb), :]


def _s5(u_slabs, bsb, lr_b, li_b, cre, cim, d, wglu, *, B, S, steps=64):
    rows = steps * B
    return pl.pallas_call(
        functools.partial(_s5_kernel, steps=steps, nb=B),
        out_shape=jax.ShapeDtypeStruct((B, S, S5_WIDTH), F32),
        grid=(S // steps,),
        in_specs=[pl.BlockSpec((S5_SUPER, rows, S5_SUPER_CH), lambda i: (0, i, 0)),
                  _resident((S5_SUPER, S5_SUPER_CH, 2 * S5_SUPER_MODES)),
                  _resident((B, S5_MODES)),
                  _resident((B, S5_MODES)),
                  _resident((S5_SUPER, S5_SUPER_MODES, S5_SUPER_CH)),
                  _resident((S5_SUPER, S5_SUPER_MODES, S5_SUPER_CH)),
                  _resident((1, S5_WIDTH)),
                  _resident((S5_WIDTH, 2 * S5_WIDTH))],
        out_specs=pl.BlockSpec((B, steps, S5_WIDTH), lambda i: (0, i, 0)),
        scratch_shapes=[pltpu.VMEM((rows, 2 * S5_MODES), F32),
                        pltpu.VMEM((B, 2 * S5_MODES), F32),
                        pltpu.VMEM((S5_SUPER, rows, S5_SUPER_CH), F32)],
        compiler_params=_cparams(1),
        name="s5",
    )(u_slabs, bsb, lr_b, li_b, cre, cim, d, wglu)


def _rope_tables(S):
    inv = ROPE_THETA ** (-jnp.arange(HALF, dtype=F32) / HALF)
    ang = jnp.arange(S, dtype=F32)[:, None] * inv[None, :]
    cos, sin = jnp.cos(ang), jnp.sin(ang)
    cos_t = jnp.tile(cos, (1, LANES // HALF))
    sin_t = jnp.tile(jnp.concatenate([-sin, sin], axis=1), (1, LANES // HEAD_DIM))
    end_pos = (jnp.arange(S // CMP_STRIDE, dtype=F32) * CMP_STRIDE + (CMP_LEN - 1))[:, None] * inv[None, :]
    cosc = jnp.tile(jnp.cos(end_pos), (1, LANES // HALF))
    sinc = jnp.tile(jnp.concatenate([-jnp.sin(end_pos), jnp.sin(end_pos)], axis=1), (1, LANES // HEAD_DIM))
    return cos_t, sin_t, cosc, sinc


def _structure_constants(S):
    ncmp = S // CMP_STRIDE
    nsel = S // SEL_BLOCK
    assert nsel <= LANES - HEAD_DIM
    j = jnp.arange(LANES, dtype=jnp.int32)[:, None]
    n = jnp.arange(ncmp, dtype=jnp.int32)[None, :]
    ovt = ((n * CMP_STRIDE < (j + 1) * SEL_BLOCK) & (n * CMP_STRIDE + CMP_LEN > j * SEL_BLOCK)
           & (j < nsel) & (n < ncmp - 1)).astype(BF16)
    key_blk = (jnp.arange(S, dtype=jnp.int32) // SEL_BLOCK)[:, None]
    blk_tab = (key_blk + HEAD_DIM == jnp.arange(LANES, dtype=jnp.int32)[None, :]).astype(F32)
    c = jnp.arange(HALF_W, dtype=jnp.int32)
    avg = ((c[:, None] // HEAD_DIM == c[None, :] // HEAD_DIM).astype(F32) / HEAD_DIM).astype(BF16)
    return ovt, blk_tab, avg


def _even_mixer(x, ffn_args, g, w_in, gm_w_s, gm_b, pe, w1, w2, consts, *, B, S):
    cos_t, sin_t, cosc, sinc, ovt, blk_tab, avg = consts
    w_pad = jnp.pad(w_in, ((0, 0), (0, EVEN_IN_PAD - EVEN_IN))).astype(BF16)
    x, uv, q, gl, kc, vc, ksa, vsa, kwa, vwa = _ffn_even_proj(x, *ffn_args, g, w_pad, cos_t, sin_t, blk_tab,
                                                              B=B, S=S)
    out_a = _gmlp(uv, gm_w_s.reshape(N_HEADS * GM_CHUNK, GM_CHUNK), avg,
                  jnp.repeat(gm_b.T, HEAD_DIM, axis=1))
    kcmp, vcmp = _compress(kc, vc, jnp.tile(pe.reshape(2, CMP_LEN, 1, HEAD_DIM), (1, 1, 1, NSA_KV)),
                           w1.reshape(2, CMP_LEN, HEAD_DIM, HEAD_DIM), w2, cosc, sinc, B=B, S=S)
    out_b = _nsa(q, gl, ksa, vsa, kwa, vwa, kcmp, vcmp, ovt, B=B, S=S)
    return x, out_a, out_b


def _odd_mixer(x, ffn_args, g, w_in, pool_w, pool_scale, lam_re, lam_im, b_re, b_im, c_re, c_im,
               d_skip, log_dt, w_glu, *, B, S):
    x, zc, u_slabs = _ffn_odd_proj(x, *ffn_args, g, w_in.astype(BF16), B=B, S=S)
    y_c = _pool(zc, pool_w, pool_scale.reshape(1, HALF_W), B=B, S=S)

    lbr, lbi, bbr, bbi = _s5_params(lam_re, lam_im, log_dt, b_re, b_im)
    gps = S5_GROUPS // S5_SUPER
    eye = jnp.eye(gps, dtype=F32)
    blockdiag_in = lambda bb: jnp.einsum(
        'sgpc,gh->sgchp', bb.reshape(S5_SUPER, gps, S5_STATE, S5_GROUP_CH), eye
    ).reshape(S5_SUPER, S5_SUPER_CH, S5_SUPER_MODES)
    blockdiag_out = lambda cc: jnp.einsum(
        'sgcp,gh->sgphc', cc.reshape(S5_SUPER, gps, S5_GROUP_CH, S5_STATE), eye
    ).reshape(S5_SUPER, S5_SUPER_MODES, S5_SUPER_CH)
    bsb = jnp.concatenate([blockdiag_in(bbr), blockdiag_in(bbi)], axis=2).astype(BF16)
    bcast = lambda a: jnp.broadcast_to(a.reshape(1, S5_MODES), (B, S5_MODES))
    y_d = _s5(u_slabs, bsb, bcast(lbr), bcast(lbi),
              blockdiag_out(c_re).astype(BF16), blockdiag_out(c_im).astype(BF16),
              d_skip.reshape(1, S5_WIDTH), w_glu.astype(BF16), B=B, S=S)
    return x, y_c, y_d


def kernel(x, norm_w, ffn_w_gate, ffn_w_up, ffn_w_down, final_norm_w, ev_w_in, ev_w_out, gm_w_s, gm_b,
           nsa_cmp_pe, nsa_cmp_w1, nsa_cmp_w2, od_w_in, od_w_out, pool_w, pool_scale, s5_lam_re,
           s5_lam_im, s5_b_re, s5_b_im, s5_c_re, s5_c_im, s5_d, s5_log_dt, s5_w_glu):
    B, S, _ = x.shape
    consts = _rope_tables(S) + _structure_constants(S)
    wg, wu, wd = ffn_w_gate.astype(BF16), ffn_w_up.astype(BF16), ffn_w_down.astype(BF16)
    ffn_w = lambda l, k: (norm_w[l, 2 * k].reshape(1, D_MODEL), wg, wu, wd, l, k)
    gf = final_norm_w.reshape(1, D_MODEL)
    xt = x
    for l in range(DEPTH):
        g = norm_w[l, 1].reshape(1, D_MODEL)
        i = l // 2
        if l % 2 == 0:
            xt, a, b = _even_mixer(xt, ffn_w(l, 0), g, ev_w_in[i], gm_w_s[i], gm_b[i], nsa_cmp_pe[i],
                                   nsa_cmp_w1[i], nsa_cmp_w2[i], consts, B=B, S=S)
            w_out = ev_w_out[i]
        else:
            xt, a, b = _odd_mixer(xt, ffn_w(l, 0), g, od_w_in[i], pool_w[i], pool_scale[i], s5_lam_re[i],
                                  s5_lam_im[i], s5_b_re[i], s5_b_im[i], s5_c_re[i], s5_c_im[i], s5_d[i],
                                  s5_log_dt[i], s5_w_glu[i], B=B, S=S)
            w_out = od_w_out[i]
        xt = _mix_ffn(a, b, w_out.astype(BF16), xt, *ffn_w(l, 1), gf, final=(l == DEPTH - 1))
    return xt
```

```python
import functools

import jax
import jax.numpy as jnp
from jax import lax
from jax.experimental import pallas as pl
from jax.experimental.pallas import tpu as pltpu

F32 = jnp.float32
BF16 = jnp.bfloat16

D_MODEL = 1024
DEPTH = 4
HEAD_DIM = 64
HALF = HEAD_DIM // 2
ROPE_THETA = 10000.0
N_HEADS = 8
GM_CHUNK = 128
NSA_KV = 2
NSA_HPG = 4
CMP_LEN = 32
CMP_STRIDE = 16
SEL_BLOCK = 64
SEL_TOPN = 8
WINDOW = 512
Q_BLOCK = 256
POOL_WINDOWS = (2, 4, 8, 16)
POOL_GROUP = 128
S5_GROUPS = 32
S5_GROUP_CH = 16
S5_STATE = 64
S5_WIDTH = 512
S5_MODES = S5_GROUPS * S5_STATE
S5_SUPER = 4
S5_SUPER_CH = S5_WIDTH // S5_SUPER
S5_SUPER_MODES = S5_MODES // S5_SUPER
FFN_DIM = 2816
RMS_EPS = 1e-6
LN_EPS = 1e-5
NEG_INF = -1e30
SEL_FORCE = 1e4
LOG2E = 1.4426950408889634

LANES = 128
HALF_W = 512
EVEN_IN = 2328
EVEN_IN_PAD = 2432
COL_U, COL_V, COL_Q = 0, 512, 1024
COL_KC, COL_VC, COL_KS, COL_VS, COL_KW, COL_VW, COL_GL = 1536, 1664, 1792, 1920, 2048, 2176, 2304
SEL_STEP = 256
WIN_PIECE = 768
ONES_LANE = HEAD_DIM

VMEM_LIMIT = 56 * 1024 * 1024


def _cparams(n_axes):
    return pltpu.CompilerParams(dimension_semantics=("arbitrary",) * n_axes,
                                vmem_limit_bytes=VMEM_LIMIT)


def _resident(shape):
    nd = len(shape)
    return pl.BlockSpec(shape, lambda *_: (0,) * nd, pipeline_mode=pl.Buffered(1))


def _rows(tm, width, S):
    spt = S // tm
    return pl.BlockSpec((None, tm, width), lambda i: (i // spt, i % spt, 0))


def _rms(x, g):
    ms = jnp.mean(x * x, axis=-1, keepdims=True)
    return x * lax.rsqrt(ms + RMS_EPS) * g


def _dot(a, b):
    return jnp.dot(a, b, preferred_element_type=F32)


def _dot_nt(a, b):
    return lax.dot_general(a, b, (((1,), (1,)), ((), ())), preferred_element_type=F32)


def _split_bf16(x):
    hi = x.astype(BF16)
    return hi, (x - hi.astype(F32)).astype(BF16)


def _swiglu_half_step(x, g, wg_ref, wu_ref, wd_ref, fc):
    h = _rms(x, g).astype(BF16)
    acc = jnp.zeros_like(x)
    for c in range(FFN_DIM // fc):
        sl = slice(c * fc, (c + 1) * fc)
        a = _dot(h, wg_ref[:, sl])
        b = _dot(h, wu_ref[:, sl])
        t = (a * jax.nn.sigmoid(a) * b).astype(BF16)
        acc = acc + _dot(t, wd_ref[sl, :])
    return x + 0.5 * acc


def _mix_ffn_kernel(a_ref, b_ref, wo_ref, x_ref, g_ref, wg_ref, wu_ref, wd_ref, gf_ref, o_ref, *, fc, final):
    x = x_ref[...] + _dot(a_ref[...].astype(BF16), wo_ref[0:HALF_W, :])
    x = x + _dot(b_ref[...].astype(BF16), wo_ref[HALF_W:2 * HALF_W, :])
    x = _swiglu_half_step(x, g_ref[...], wg_ref, wu_ref, wd_ref, fc)
    o_ref[...] = _rms(x, gf_ref[...]) if final else x


def _ffn_weight_specs(l, k):
    pick = lambda rows, cols: pl.BlockSpec((None, None, rows, cols), lambda *_: (l, k, 0, 0),
                                           pipeline_mode=pl.Buffered(1))
    return [_resident((1, D_MODEL)), pick(D_MODEL, FFN_DIM), pick(D_MODEL, FFN_DIM), pick(FFN_DIM, D_MODEL)]


def _mix_ffn(a, b, wo, x, g, wg, wu, wd, l, k, gf, *, final, tm=512, fc=256):
    B, S, _ = x.shape
    return pl.pallas_call(
        functools.partial(_mix_ffn_kernel, fc=fc, final=final),
        out_shape=jax.ShapeDtypeStruct((B, S, D_MODEL), F32),
        grid=(B * S // tm,),
        in_specs=[_rows(tm, HALF_W, S), _rows(tm, HALF_W, S), _resident((D_MODEL, D_MODEL)),
                  _rows(tm, D_MODEL, S)] + _ffn_weight_specs(l, k) + [_resident((1, D_MODEL))],
        out_specs=_rows(tm, D_MODEL, S),
        compiler_params=_cparams(1),
        name="mix_ffn",
    )(a, b, wo, x, g, wg, wu, wd, gf)


def _ffn_odd_proj_kernel(x_ref, g_ref, wg_ref, wu_ref, wd_ref, gp_ref, wp_ref, xo_ref, zc_ref, u_ref,
                         *, fc, nb, ts):
    x = _swiglu_half_step(x_ref[...].reshape(nb * ts, D_MODEL), g_ref[...], wg_ref, wu_ref, wd_ref, fc)
    xo_ref[...] = x.reshape(nb, ts, D_MODEL)
    z = _dot(_rms(x, gp_ref[...]).astype(BF16), wp_ref[...])
    zc_ref[...] = z[:, :HALF_W].reshape(nb, ts, HALF_W)
    for sb in range(S5_SUPER):
        c0 = HALF_W + sb * S5_SUPER_CH
        for b in range(nb):
            u_ref[sb, pl.ds(b, ts, stride=nb), :] = z[b * ts:(b + 1) * ts, c0:c0 + S5_SUPER_CH]


def _ffn_odd_proj(x, g, wg, wu, wd, l, k, gp, wp, *, B, S, ts=128, fc=256):
    whole = lambda width: pl.BlockSpec((B, ts, width), lambda i: (0, i, 0))
    return pl.pallas_call(
        functools.partial(_ffn_odd_proj_kernel, fc=fc, nb=B, ts=ts),
        out_shape=(jax.ShapeDtypeStruct((B, S, D_MODEL), F32), jax.ShapeDtypeStruct((B, S, HALF_W), F32),
                   jax.ShapeDtypeStruct((S5_SUPER, S * B, S5_SUPER_CH), F32)),
        grid=(S // ts,),
        in_specs=[whole(D_MODEL)] + _ffn_weight_specs(l, k)
                 + [_resident((1, D_MODEL)), _resident((D_MODEL, 2 * HALF_W))],
        out_specs=(whole(D_MODEL), whole(HALF_W),
                   pl.BlockSpec((S5_SUPER, ts * B, S5_SUPER_CH), lambda i: (0, i, 0))),
        compiler_params=_cparams(1),
        name="ffn_odd_proj",
    )(x, g, wg, wu, wd, gp, wp)


def _ffn_even_proj_kernel(x_ref, g_ref, wg_ref, wu_ref, wd_ref, gp_ref, w_ref, cos_ref, sin_ref, blk_ref,
                          xo_ref, uv_ref, q_ref, gl_ref, kc_ref, vc_ref, ksa_ref, vsa_ref, kwa_ref, vwa_ref,
                          *, fc):
    x = _swiglu_half_step(x_ref[...], g_ref[...], wg_ref, wu_ref, wd_ref, fc)
    xo_ref[...] = x
    z = _dot(_rms(x, gp_ref[...]).astype(BF16), w_ref[...])
    cos = cos_ref[...]
    sin = sin_ref[...]
    lane = lax.broadcasted_iota(jnp.int32, cos.shape, 1)
    first_half = (lane % HEAD_DIM) < HALF
    low = lane < HEAD_DIM

    def slab(c0, rotary):
        xs = z[:, c0:c0 + LANES]
        if not rotary:
            return xs
        rot = jnp.where(first_half, pltpu.roll(xs, LANES - HALF, 1), pltpu.roll(xs, HALF, 1))
        return xs * cos + rot * sin

    uv_ref[...] = z[:, COL_U:COL_Q]
    for c in range(HALF_W // LANES):
        q_ref[:, c * LANES:(c + 1) * LANES] = slab(COL_Q + c * LANES, True)
    gl_ref[...] = z[:, COL_GL:COL_GL + LANES]
    kc_ref[...] = slab(COL_KC, False)
    vc_ref[...] = slab(COL_VC, False)
    ones = (lane == ONES_LANE).astype(F32)
    for c0, rotary, extra, dst in ((COL_KS, True, blk_ref[...], ksa_ref), (COL_VS, False, ones, vsa_ref),
                                   (COL_KW, True, 0.0, kwa_ref), (COL_VW, False, ones, vwa_ref)):
        xs = slab(c0, rotary)
        dst[0, 0] = jnp.where(low, xs, extra).astype(BF16)
        dst[0, 1] = jnp.where(low, pltpu.roll(xs, HEAD_DIM, 1), extra).astype(BF16)


def _ffn_even_proj(x, g, wg, wu, wd, l, k, gp, w, cos, sin, blk, *, B, S, tm=512, fc=256):
    spt = S // tm
    tok = lambda width: _rows(tm, width, S)
    pos = pl.BlockSpec((tm, LANES), lambda i: (i % spt, 0))
    grp = pl.BlockSpec((1, NSA_KV, tm, LANES), lambda i: (i // spt, 0, i % spt, 0))
    aug = jax.ShapeDtypeStruct((B, NSA_KV, S, LANES), BF16)
    tokens = lambda width: jax.ShapeDtypeStruct((B, S, width), F32)
    return pl.pallas_call(
        functools.partial(_ffn_even_proj_kernel, fc=fc),
        out_shape=(tokens(D_MODEL), tokens(2 * HALF_W), tokens(HALF_W), tokens(LANES), tokens(LANES),
                   tokens(LANES), aug, aug, aug, aug),
        grid=(B * S // tm,),
        in_specs=[tok(D_MODEL)] + _ffn_weight_specs(l, k)
                 + [_resident((1, D_MODEL)), _resident((D_MODEL, EVEN_IN_PAD)), pos, pos, pos],
        out_specs=(tok(D_MODEL), tok(2 * HALF_W), tok(HALF_W), tok(LANES), tok(LANES), tok(LANES),
                   grp, grp, grp, grp),
        compiler_params=_cparams(1),
        name="ffn_even_proj",
    )(x, g, wg, wu, wd, gp, w, cos, sin, blk)


def _gmlp_kernel(u_ref, v_ref, ws_ref, avg_ref, bias_ref, o_ref, *, chunks):
    rows_all = N_HEADS * GM_CHUNK
    r = lax.broadcasted_iota(jnp.int32, (rows_all, GM_CHUNK), 0) & (GM_CHUNK - 1)
    c = lax.broadcasted_iota(jnp.int32, (rows_all, GM_CHUNK), 1)
    w_all = jnp.where(c <= r, ws_ref[...], 0.0).astype(BF16)
    avg = avg_ref[...]

    def head_mean(x):
        return _dot(x.astype(BF16), avg)

    v = jax.nn.gelu(v_ref[...])
    d = v - head_mean(v)
    vn = (d * lax.rsqrt(head_mean(d * d) + LN_EPS)).astype(BF16)
    lane_head = lax.broadcasted_iota(jnp.int32, (GM_CHUNK, HALF_W), 1) // HEAD_DIM
    for ci in range(chunks):
        rows = slice(ci * GM_CHUNK, (ci + 1) * GM_CHUNK)
        res = _dot(w_all, vn[rows, :])
        s = res[0:GM_CHUNK, :]
        for h in range(1, N_HEADS):
            s = jnp.where(lane_head == h, res[h * GM_CHUNK:(h + 1) * GM_CHUNK, :], s)
        o_ref[rows, :] = jax.nn.gelu(u_ref[rows, :]) * (s + bias_ref[...])


def _gmlp(uv, ws, avg, bias, *, chunks=4):
    B, S, _ = uv.shape
    tm = chunks * GM_CHUNK
    spt = S // tm
    return pl.pallas_call(
        functools.partial(_gmlp_kernel, chunks=chunks),
        out_shape=jax.ShapeDtypeStruct((B, S, HALF_W), F32),
        grid=(B * S // tm,),
        in_specs=[pl.BlockSpec((None, tm, HALF_W), lambda i: (i // spt, i % spt, 0)),
                  pl.BlockSpec((None, tm, HALF_W), lambda i: (i // spt, i % spt, 1)),
                  _resident((N_HEADS * GM_CHUNK, GM_CHUNK)),
                  _resident((HALF_W, HALF_W)),
                  _resident((GM_CHUNK, HALF_W))],
        out_specs=_rows(tm, HALF_W, S),
        compiler_params=_cparams(1),
        name="gmlp",
    )(uv, uv, ws, avg, bias)


def _cmp_kernel(k_ref, v_ref, pe_ref, w1_ref, w2_ref, cos_ref, sin_ref, ko_ref, vo_ref, *, nblk):
    lane = lax.broadcasted_iota(jnp.int32, (nblk, LANES), 1)
    low = lane < HEAD_DIM
    first_half = (lane % HEAD_DIM) < HALF

    def both_groups(w):
        z = jnp.zeros_like(w)
        return jnp.concatenate([jnp.concatenate([w, z], axis=1), jnp.concatenate([z, w], axis=1)],
                               axis=0).astype(BF16)

    for which, (src, dst) in enumerate(((k_ref, ko_ref), (v_ref, vo_ref))):
        pre = jnp.zeros((nblk, LANES), F32)
        for r in range(CMP_STRIDE):
            rows = src[pl.ds(r, nblk, stride=CMP_STRIDE), :]
            nxt = pltpu.roll(rows, nblk - 1, 0)
            pre = pre + _dot((rows + pe_ref[which, r]).astype(BF16), both_groups(w1_ref[which, r]))
            pre = pre + _dot((nxt + pe_ref[which, CMP_STRIDE + r]).astype(BF16),
                             both_groups(w1_ref[which, CMP_STRIDE + r]))
        cmp = _dot(jax.nn.gelu(pre).astype(BF16), both_groups(w2_ref[which]))
        if which == 0:
            rot = jnp.where(first_half, pltpu.roll(cmp, LANES - HALF, 1), pltpu.roll(cmp, HALF, 1))
            cmp = cmp * cos_ref[...] + rot * sin_ref[...]
        dst[0, 0] = jnp.where(low, cmp, 0.0).astype(BF16)
        dst[0, 1] = jnp.where(low, pltpu.roll(cmp, HEAD_DIM, 1), 0.0).astype(BF16)


def _compress(kc, vc, pe, w1, w2, cosc, sinc, *, B, S):
    nblk = S // CMP_STRIDE
    out = jax.ShapeDtypeStruct((B, NSA_KV, nblk, LANES), BF16)
    out_spec = pl.BlockSpec((1, NSA_KV, nblk, LANES), lambda b: (b, 0, 0, 0))
    return pl.pallas_call(
        functools.partial(_cmp_kernel, nblk=nblk),
        out_shape=(out, out),
        grid=(B,),
        in_specs=[pl.BlockSpec((None, S, LANES), lambda b: (b, 0, 0)),
                  pl.BlockSpec((None, S, LANES), lambda b: (b, 0, 0)),
                  _resident((2, CMP_LEN, 1, LANES)),
                  _resident((2, CMP_LEN, HEAD_DIM, HEAD_DIM)),
                  _resident((2, HEAD_DIM, HEAD_DIM)),
                  _resident((nblk, LANES)),
                  _resident((nblk, LANES))],
        out_specs=(out_spec, out_spec),
        compiler_params=_cparams(1),
        name="nsa_compress",
    )(kc, vc, pe, w1, w2, cosc, sinc)


def _softmax_pv(pieces):
    ms = []
    for group in pieces:
        m = None
        for s, _ in group:
            pm = jnp.max(s, axis=-1, keepdims=True)
            m = pm if m is None else jnp.maximum(m, pm)
        ms.append(m)
    accs = [None] * len(pieces)
    for j in range(len(pieces[0])):
        for g, group in enumerate(pieces):
            s, vv = group[j]
            part = _dot(jnp.exp2(s - ms[g]).astype(BF16), vv)
            accs[g] = part if accs[g] is None else accs[g] + part
    return [acc / acc[:, ONES_LANE:ONES_LANE + 1] for acc in accs]


def _nsa_kernel(q_ref, gl_ref, ksa_ref, vsa_ref, kwa_ref, vwa_ref, kc_ref, vc_ref, ovt_ref, o_ref,
                osel_ref, *, S):
    TQ = Q_BLOCK
    R = NSA_HPG * TQ
    G = range(NSA_KV)
    ncmp = S // CMP_STRIDE
    nsel = S // SEL_BLOCK
    i = pl.program_id(1)
    t0 = i * TQ

    q = q_ref[...] * (HEAD_DIM ** -0.5 * LOG2E)
    gate = jax.nn.sigmoid(gl_ref[...])
    lane = lax.broadcasted_iota(jnp.int32, (TQ, LANES), 1)
    lane4 = lax.broadcasted_iota(jnp.int32, (R, LANES), 1)
    trow = t0 + lax.broadcasted_iota(jnp.int32, (TQ, 1), 0)

    def mask_heads(s, allowed):
        return jnp.concatenate([jnp.where(allowed, s[h * TQ:(h + 1) * TQ], NEG_INF) for h in range(NSA_HPG)],
                               axis=0)

    def stack_heads(g):
        heads = []
        for hq in range(NSA_HPG):
            h = g * NSA_HPG + hq
            slab = q[:, (h // 2) * LANES:(h // 2 + 1) * LANES]
            if h % 2 == 1:
                slab = pltpu.roll(slab, HEAD_DIM, 1)
            heads.append(jnp.where(lane < HEAD_DIM, slab, 0.0))
        return jnp.concatenate(heads, axis=0)

    q4 = [stack_heads(g) for g in G]
    q4b = [x.astype(BF16) for x in q4]

    s_c = [_dot_nt(q4b[g], kc_ref[0, g]) for g in G]
    k0 = pl.multiple_of(jnp.maximum(i - WINDOW // TQ, 0) * TQ, TQ)
    win = [[] for _ in G]
    span = WINDOW + TQ
    for lo, hi in [(lo, min(lo + WIN_PIECE, span)) for lo in range(0, span, WIN_PIECE)]:
        kpos = k0 + lo + lax.broadcasted_iota(jnp.int32, (TQ, hi - lo), 1)
        allowed = (kpos <= trow) & (trow - kpos < WINDOW)
        for g in G:
            s = _dot_nt(q4b[g], kwa_ref[0, g, pl.ds(k0 + lo, hi - lo), :])
            win[g].append((mask_heads(s, allowed), vwa_ref[0, g, pl.ds(k0 + lo, hi - lo), :]))

    ncol = lax.broadcasted_iota(jnp.int32, (TQ, ncmp), 1)
    valid_c = (ncol * CMP_STRIDE + (CMP_LEN - 1) <= trow) & (ncol < ncmp - 1)
    any_c = jnp.concatenate([(trow >= CMP_LEN - 1).astype(F32)] * NSA_HPG, axis=0)
    p_c = []
    for g in G:
        s = mask_heads(s_c[g], valid_c)
        e = jnp.exp2(s - jnp.max(s, axis=-1, keepdims=True))
        p_c.append(e / jnp.sum(e, axis=-1, keepdims=True) * any_c)
    o_cmp = [_dot(p_c[g].astype(BF16), vc_ref[0, g]) for g in G]

    ovt = ovt_ref[...]
    jrow = lax.broadcasted_iota(jnp.int32, (nsel, TQ), 0)
    tl = t0 + lax.broadcasted_iota(jnp.int32, (nsel, TQ), 1)
    cur = jnp.right_shift(tl, 6)
    forced = (jrow == 0) | (jrow == cur) | (jrow == cur - 1)
    causal_blk = jrow * SEL_BLOCK <= tl
    imp_t = []
    for g in G:
        p = p_c[g]
        p_hi, p_lo = _split_bf16((p[0:TQ] + p[TQ:2 * TQ]) + (p[2 * TQ:3 * TQ] + p[3 * TQ:4 * TQ]))
        imp = (_dot_nt(ovt, p_hi) + _dot_nt(ovt, p_lo))[0:nsel, :]
        imp_t.append(jnp.where(forced, SEL_FORCE, jnp.where(causal_blk, imp, -SEL_FORCE)))

    SUB = 8
    sub = lax.broadcasted_iota(jnp.int32, (SUB, TQ), 0)
    chunks = [[imp_t[g][c * SUB:(c + 1) * SUB] for c in range(nsel // SUB)] for g in G]
    rank = [[jnp.zeros((SUB, TQ), F32) for _ in range(nsel // SUB)] for _ in G]
    for jp in range(nsel):
        for g in G:
            row = imp_t[g][jp:jp + 1, :]
            for c, x in enumerate(chunks[g]):
                ge = jnp.where(row >= x, 1.0, 0.0)
                gt = jnp.where(row > x, 1.0, 0.0)
                if jp < c * SUB:
                    inc = ge
                elif jp >= (c + 1) * SUB:
                    inc = gt
                else:
                    inc = jnp.where(sub > jp - c * SUB, ge, gt)
                rank[g][c] = rank[g][c] + inc

    q4a = []
    for g in G:
        sel_t = (jnp.concatenate(rank[g], axis=0) < SEL_TOPN).astype(F32)
        if nsel < LANES:
            sel_t = jnp.concatenate([sel_t, jnp.zeros((LANES - nsel, TQ), F32)], axis=0)
        bias = (pltpu.roll(sel_t.T, HEAD_DIM, 1) - 1.0) * (-NEG_INF)
        q4a.append(jnp.where(lane4 < HEAD_DIM, q4[g], jnp.concatenate([bias] * NSA_HPG, axis=0)).astype(BF16))

    o_win = _softmax_pv(win)

    col_step = lax.broadcasted_iota(jnp.int32, (TQ, SEL_STEP), 1)
    for n in range(S // SEL_STEP):
        @pl.when(i // (SEL_STEP // TQ) == n)
        def _(n=n):
            pieces = [[] for _ in G]
            for j in range(n + 1):
                lo, hi = j * SEL_STEP, (j + 1) * SEL_STEP
                for g in G:
                    s = _dot_nt(q4a[g], ksa_ref[0, g, lo:hi, :])
                    if j == n:
                        s = mask_heads(s, lo + col_step <= trow)
                    pieces[g].append((s, vsa_ref[0, g, lo:hi, :]))
            for g, o in enumerate(_softmax_pv(pieces)):
                osel_ref[g] = o

    for g in G:
        o_s = osel_ref[g]
        outs = []
        for hq in range(NSA_HPG):
            h = g * NSA_HPG + hq
            rows = slice(hq * TQ, (hq + 1) * TQ)
            outs.append(gate[:, 3 * h:3 * h + 1] * o_cmp[g][rows] + gate[:, 3 * h + 1:3 * h + 2] * o_s[rows]
                        + gate[:, 3 * h + 2:3 * h + 3] * o_win[g][rows])
        for pair in range(NSA_HPG // 2):
            c0 = (g * NSA_HPG + 2 * pair) * HEAD_DIM
            o_ref[:, c0:c0 + LANES] = jnp.where(lane < HEAD_DIM, outs[2 * pair],
                                                pltpu.roll(outs[2 * pair + 1], HEAD_DIM, 1))


def _nsa(q, gl, ksa, vsa, kwa, vwa, kcmp, vcmp, ovt, *, B, S):
    nq = S // Q_BLOCK
    ncmp = S // CMP_STRIDE
    per_batch = lambda rows: pl.BlockSpec((1, NSA_KV, rows, LANES), lambda b, i: (b, 0, 0, 0))
    return pl.pallas_call(
        functools.partial(_nsa_kernel, S=S),
        out_shape=jax.ShapeDtypeStruct((B, S, HALF_W), F32),
        grid=(B, nq),
        in_specs=[pl.BlockSpec((None, Q_BLOCK, HALF_W), lambda b, i: (b, i, 0)),
                  pl.BlockSpec((None, Q_BLOCK, LANES), lambda b, i: (b, i, 0)),
                  per_batch(S), per_batch(S), per_batch(S), per_batch(S),
                  per_batch(ncmp), per_batch(ncmp),
                  _resident((LANES, ncmp))],
        out_specs=pl.BlockSpec((None, Q_BLOCK, HALF_W), lambda b, i: (b, i, 0)),
        scratch_shapes=[pltpu.VMEM((NSA_KV, NSA_HPG * Q_BLOCK, LANES), F32)],
        compiler_params=_cparams(2),
        name="nsa_attn",
    )(q, gl, ksa, vsa, kwa, vwa, kcmp, vcmp, ovt)


def _pool_kernel(z_ref, pw_ref, ps_ref, o_ref):
    S = z_ref.shape[0]
    row = lax.broadcasted_iota(jnp.int32, (S, POOL_GROUP), 0)
    for gi, w in enumerate(POOL_WINDOWS):
        cols = slice(gi * POOL_GROUP, (gi + 1) * POOL_GROUP)
        zg = z_ref[:, cols]
        acc = zg
        k = 1
        while k < w:
            acc = acc + jnp.where(row >= k, pltpu.roll(acc, k, 0), 0.0)
            k *= 2
        cnt = jnp.minimum(row + 1, w).astype(F32)
        pooled = acc / cnt - zg
        y = _dot(pooled.astype(BF16), pw_ref[gi].astype(BF16))
        o_ref[:, cols] = y * ps_ref[:, cols]


def _pool(z, pw, ps, *, B, S):
    return pl.pallas_call(
        _pool_kernel,
        out_shape=jax.ShapeDtypeStruct((B, S, HALF_W), F32),
        grid=(B,),
        in_specs=[pl.BlockSpec((None, S, HALF_W), lambda b: (b, 0, 0)),
                  _resident((len(POOL_WINDOWS), POOL_GROUP, POOL_GROUP)),
                  _resident((1, HALF_W))],
        out_specs=pl.BlockSpec((None, S, HALF_W), lambda b: (b, 0, 0)),
        compiler_params=_cparams(1),
        name="pool",
    )(z, pw, ps)


def _s5_disc_kernel(lr_ref, li_ref, ldt_ref, lbr_ref, lbi_ref, cr_ref, ci_ref):
    lr = lr_ref[...]
    li = li_ref[...]
    dt = jnp.exp(ldt_ref[...])
    mag = jnp.exp(lr * dt)
    ang = li * dt
    lbr = mag * jnp.cos(ang)
    lbi = mag * jnp.sin(ang)
    nr = lbr - 1.0
    den = lr * lr + li * li
    lbr_ref[...] = lbr
    lbi_ref[...] = lbi
    cr_ref[...] = (nr * lr + lbi * li) / den
    ci_ref[...] = (lbi * lr - nr * li) / den


def _s5_bbar_kernel(cr_ref, ci_ref, br_ref, bi_ref, or_ref, oi_ref):
    cr = cr_ref[...]
    ci = ci_ref[...]
    br = br_ref[...]
    bi = bi_ref[...]
    or_ref[...] = cr * br - ci * bi
    oi_ref[...] = cr * bi + ci * br


def _s5_params(lam_re, lam_im, log_dt, b_re, b_im):
    gp = jax.ShapeDtypeStruct((S5_GROUPS, S5_STATE), F32)
    lbr, lbi, cr, ci = pl.pallas_call(_s5_disc_kernel, out_shape=(gp, gp, gp, gp), name="s5_disc")(
        lam_re, lam_im, log_dt.reshape(S5_GROUPS, 1))
    flat = jax.ShapeDtypeStruct((S5_MODES, S5_GROUP_CH), F32)
    bbr, bbi = pl.pallas_call(_s5_bbar_kernel, out_shape=(flat, flat), name="s5_bbar")(
        cr.reshape(S5_MODES, 1), ci.reshape(S5_MODES, 1),
        b_re.reshape(S5_MODES, S5_GROUP_CH), b_im.reshape(S5_MODES, S5_GROUP_CH))
    return lbr, lbi, bbr, bbi


def _s5_kernel(u_ref, bsb_ref, lr_ref, li_ref, cre_ref, cim_ref, d_ref, wglu_ref, o_ref,
               xs_ref, st_ref, ys_ref, *, steps, nb):
    M = S5_MODES
    W = S5_SUPER_MODES

    @pl.when(pl.program_id(0) == 0)
    def _():
        st_ref[...] = jnp.zeros_like(st_ref)

    half = steps // 2 * nb
    halves = (slice(0, half), slice(half, 2 * half))

    def input_drive(rows, sb):
        bu = _dot(u_ref[sb, rows, :].astype(BF16), bsb_ref[sb])
        xs_ref[rows, sb * W:(sb + 1) * W] = bu[:, :W]
        xs_ref[rows, M + sb * W:M + (sb + 1) * W] = bu[:, W:]

    def readout(rows, sb):
        y = (_dot(xs_ref[rows, sb * W:(sb + 1) * W].astype(BF16), cre_ref[sb])
             - _dot(xs_ref[rows, M + sb * W:M + (sb + 1) * W].astype(BF16), cim_ref[sb]))
        return y + d_ref[:, sb * S5_SUPER_CH:(sb + 1) * S5_SUPER_CH] * u_ref[sb, rows, :]

    def scan(first, carry, side_work):
        xr, xi = carry
        lr = lr_ref[...]
        li = li_ref[...]
        every = (steps // 2) // len(side_work)
        done = []
        for t in range(steps // 2):
            if t % every == 0:
                done.append(side_work[t // every]())
            r0 = (first + t) * nb
            nr = lr * xr - li * xi + xs_ref[r0:r0 + nb, 0:M]
            ni = lr * xi + li * xr + xs_ref[r0:r0 + nb, M:2 * M]
            xs_ref[r0:r0 + nb, 0:M] = nr
            xs_ref[r0:r0 + nb, M:2 * M] = ni
            xr, xi = nr, ni
        return (xr, xi), done

    for sb in range(S5_SUPER):
        input_drive(halves[0], sb)
    carry = (st_ref[:, 0:M], st_ref[:, M:2 * M])
    carry, _ = scan(0, carry, [functools.partial(input_drive, halves[1], sb) for sb in range(S5_SUPER)])
    carry, ys0 = scan(steps // 2, carry, [functools.partial(readout, halves[0], sb) for sb in range(S5_SUPER)])
    st_ref[:, 0:M] = carry[0]
    st_ref[:, M:2 * M] = carry[1]
    ys1 = [readout(halves[1], sb) for sb in range(S5_SUPER)]

    for rows, ys in zip(halves, (ys0, ys1)):
        y = jax.nn.gelu(jnp.concatenate(ys, axis=1))
        ab = _dot(y.astype(BF16), wglu_ref[...])
        out = ab[:, :S5_WIDTH] * jax.nn.sigmoid(ab[:, S5_WIDTH:])
        for sb in range(S5_SUPER):
            ys_ref[sb, rows, :] = out[:, sb * S5_SUPER_CH:(sb + 1) * S5_SUPER_CH]
    for sb in range(S5_SUPER):
        for b in range(nb):
            o_ref[b, :, sb * S5_SUPER_CH:(sb + 1) * S5_SUPER_CH] = ys_ref[sb, pl.ds(b, steps, stride=nb), :]


def _s5(u_slabs, bsb, lr_b, li_b, cre, cim, d, wglu, *, B, S, steps=64):
    rows = steps * B
    return pl.pallas_call(
        functools.partial(_s5_kernel, steps=steps, nb=B),
        out_shape=jax.ShapeDtypeStruct((B, S, S5_WIDTH), F32),
        grid=(S // steps,),
        in_specs=[pl.BlockSpec((S5_SUPER, rows, S5_SUPER_CH), lambda i: (0, i, 0)),
                  _resident((S5_SUPER, S5_SUPER_CH, 2 * S5_SUPER_MODES)),
                  _resident((B, S5_MODES)),
                  _resident((B, S5_MODES)),
                  _resident((S5_SUPER, S5_SUPER_MODES, S5_SUPER_CH)),
                  _resident((S5_SUPER, S5_SUPER_MODES, S5_SUPER_CH)),
                  _resident((1, S5_WIDTH)),
                  _resident((S5_WIDTH, 2 * S5_WIDTH))],
        out_specs=pl.BlockSpec((B, steps, S5_WIDTH), lambda i: (0, i, 0)),
        scratch_shapes=[pltpu.VMEM((rows, 2 * S5_MODES), F32),
                        pltpu.VMEM((B, 2 * S5_MODES), F32),
                        pltpu.VMEM((S5_SUPER, rows, S5_SUPER_CH), F32)],
        compiler_params=_cparams(1),
        name="s5",
    )(u_slabs, bsb, lr_b, li_b, cre, cim, d, wglu)


def _rope_tables(S):
    inv = ROPE_THETA ** (-jnp.arange(HALF, dtype=F32) / HALF)
    ang = jnp.arange(S, dtype=F32)[:, None] * inv[None, :]
    cos, sin = jnp.cos(ang), jnp.sin(ang)
    cos_t = jnp.tile(cos, (1, LANES // HALF))
    sin_t = jnp.tile(jnp.concatenate([-sin, sin], axis=1), (1, LANES // HEAD_DIM))
    end_pos = (jnp.arange(S // CMP_STRIDE, dtype=F32) * CMP_STRIDE + (CMP_LEN - 1))[:, None] * inv[None, :]
    cosc = jnp.tile(jnp.cos(end_pos), (1, LANES // HALF))
    sinc = jnp.tile(jnp.concatenate([-jnp.sin(end_pos), jnp.sin(end_pos)], axis=1), (1, LANES // HEAD_DIM))
    return cos_t, sin_t, cosc, sinc


def _structure_constants(S):
    ncmp = S // CMP_STRIDE
    nsel = S // SEL_BLOCK
    assert nsel <= LANES - HEAD_DIM
    j = jnp.arange(LANES, dtype=jnp.int32)[:, None]
    n = jnp.arange(ncmp, dtype=jnp.int32)[None, :]
    ovt = ((n * CMP_STRIDE < (j + 1) * SEL_BLOCK) & (n * CMP_STRIDE + CMP_LEN > j * SEL_BLOCK)
           & (j < nsel) & (n < ncmp - 1)).astype(BF16)
    key_blk = (jnp.arange(S, dtype=jnp.int32) // SEL_BLOCK)[:, None]
    blk_tab = (key_blk + HEAD_DIM == jnp.arange(LANES, dtype=jnp.int32)[None, :]).astype(F32)
    c = jnp.arange(HALF_W, dtype=jnp.int32)
    avg = ((c[:, None] // HEAD_DIM == c[None, :] // HEAD_DIM).astype(F32) / HEAD_DIM).astype(BF16)
    return ovt, blk_tab, avg


def _even_mixer(x, ffn_args, g, w_in, gm_w_s, gm_b, pe, w1, w2, consts, *, B, S):
    cos_t, sin_t, cosc, sinc, ovt, blk_tab, avg = consts
    w_pad = jnp.pad(w_in, ((0, 0), (0, EVEN_IN_PAD - EVEN_IN))).astype(BF16)
    x, uv, q, gl, kc, vc, ksa, vsa, kwa, vwa = _ffn_even_proj(x, *ffn_args, g, w_pad, cos_t, sin_t, blk_tab,
                                                              B=B, S=S)
    out_a = _gmlp(uv, gm_w_s.reshape(N_HEADS * GM_CHUNK, GM_CHUNK), avg,
                  jnp.repeat(gm_b.T, HEAD_DIM, axis=1))
    kcmp, vcmp = _compress(kc, vc, jnp.tile(pe.reshape(2, CMP_LEN, 1, HEAD_DIM), (1, 1, 1, NSA_KV)),
                           w1.reshape(2, CMP_LEN, HEAD_DIM, HEAD_DIM), w2, cosc, sinc, B=B, S=S)
    out_b = _nsa(q, gl, ksa, vsa, kwa, vwa, kcmp, vcmp, ovt, B=B, S=S)
    return x, out_a, out_b


def _odd_mixer(x, ffn_args, g, w_in, pool_w, pool_scale, lam_re, lam_im, b_re, b_im, c_re, c_im,
               d_skip, log_dt, w_glu, *, B, S):
    x, zc, u_slabs = _ffn_odd_proj(x, *ffn_args, g, w_in.astype(BF16), B=B, S=S)
    y_c = _pool(zc, pool_w, pool_scale.reshape(1, HALF_W), B=B, S=S)

    lbr, lbi, bbr, bbi = _s5_params(lam_re, lam_im, log_dt, b_re, b_im)
    gps = S5_GROUPS // S5_SUPER
    eye = jnp.eye(gps, dtype=F32)
    blockdiag_in = lambda bb: jnp.einsum(
        'sgpc,gh->sgchp', bb.reshape(S5_SUPER, gps, S5_STATE, S5_GROUP_CH), eye
    ).reshape(S5_SUPER, S5_SUPER_CH, S5_SUPER_MODES)
    blockdiag_out = lambda cc: jnp.einsum(
        'sgcp,gh->sgphc', cc.reshape(S5_SUPER, gps, S5_GROUP_CH, S5_STATE), eye
    ).reshape(S5_SUPER, S5_SUPER_MODES, S5_SUPER_CH)
    bsb = jnp.concatenate([blockdiag_in(bbr), blockdiag_in(bbi)], axis=2).astype(BF16)
    bcast = lambda a: jnp.broadcast_to(a.reshape(1, S5_MODES), (B, S5_MODES))
    y_d = _s5(u_slabs, bsb, bcast(lbr), bcast(lbi),
              blockdiag_out(c_re).astype(BF16), blockdiag_out(c_im).astype(BF16),
              d_skip.reshape(1, S5_WIDTH), w_glu.astype(BF16), B=B, S=S)
    return x, y_c, y_d


def kernel(x, norm_w, ffn_w_gate, ffn_w_up, ffn_w_down, final_norm_w, ev_w_in, ev_w_out, gm_w_s, gm_b,
           nsa_cmp_pe, nsa_cmp_w1, nsa_cmp_w2, od_w_in, od_w_out, pool_w, pool_scale, s5_lam_re,
           s5_lam_im, s5_b_re, s5_b_im, s5_c_re, s5_c_im, s5_d, s5_log_dt, s5_w_glu):
    B, S, _ = x.shape
    consts = _rope_tables(S) + _structure_constants(S)
    wg, wu, wd = ffn_w_gate.astype(BF16), ffn_w_up.astype(BF16), ffn_w_down.astype(BF16)
    ffn_w = lambda l, k: (norm_w[l, 2 * k].reshape(1, D_MODEL), wg, wu, wd, l, k)
    gf = final_norm_w.reshape(1, D_MODEL)
    xt = x
    for l in range(DEPTH):
        g = norm_w[l, 1].reshape(1, D_MODEL)
        i = l // 2
        if l % 2 == 0:
            xt, a, b = _even_mixer(xt, ffn_w(l, 0), g, ev_w_in[i], gm_w_s[i], gm_b[i], nsa_cmp_pe[i],
                                   nsa_cmp_w1[i], nsa_cmp_w2[i], consts, B=B, S=S)
            w_out = ev_w_out[i]
        else:
            xt, a, b = _odd_mixer(xt, ffn_w(l, 0), g, od_w_in[i], pool_w[i], pool_scale[i], s5_lam_re[i],
                                  s5_lam_im[i], s5_b_re[i], s5_b_im[i], s5_c_re[i], s5_c_im[i], s5_d[i],
                                  s5_log_dt[i], s5_w_glu[i], B=B, S=S)
            w_out = od_w_out[i]
        xt = _mix_ffn(a, b, w_out.astype(BF16), xt, *ffn_w(l, 1), gf, final=(l == DEPTH - 1))
    return xt
```
